```python
import functools
import jax, jax.numpy as jnp
from jax import lax
import numpy as np

D_MODEL = 4096
BATCH = 8
SEQ = 2048
DEPTH = 2
DEC_BATCH = 8
DEC_SEQ = 64
PAST_LEN = 1024

CHUNK = 64
HEAD_DIM = 128
W_MIX = D_MODEL
W_A = W_MIX // 4
W_B = W_MIX // 4
W_C = W_MIX // 4
W_D = W_MIX - W_A - W_B - W_C
H_B = W_B // HEAD_DIM
H_C = W_C // HEAD_DIM
ROPE_DIM = HEAD_DIM // 4
ROPE_THETA = 500000.0
IDX_HEADS = 16
IDX_DIM = 64
IDX_ROPE_DIM = IDX_DIM // 4
TOPK_MAX = 256
CONV_A_W = 3
CONV_D_W = 4
LRU_BLOCKS = 8
LRU_BW = W_D // LRU_BLOCKS
LRU_C = 8.0
D_FF = 2 * D_MODEL
IN_SIZES = (W_A, W_A, W_A, W_B, W_B, W_B, IDX_HEADS * IDX_DIM, IDX_DIM, IDX_HEADS, W_C, W_C, W_C, H_C, W_D, W_D)
N_IN = sum(IN_SIZES)
ALPHA = (2.0 * DEPTH) ** 0.25
BETA = (8.0 * DEPTH) ** -0.25
QBLK_FOX = 128
QBLK_DSA = CHUNK
LN_EPS = 1e-5

kernel_name = "hybrid_streaming_encoder_step"


def split_cols(p, sizes):
    out, start = [], 0
    for s in sizes:
        out.append(p[..., start:start + s])
        start += s
    return out


def layer_norm(x, g, b):
    xf = x.astype(jnp.float32)
    mu = xf.mean(-1, keepdims=True)
    var = jnp.square(xf - mu).mean(-1, keepdims=True)
    return ((xf - mu) * lax.rsqrt(var + LN_EPS) * g + b).astype(x.dtype)


def swiglu(x, w13, w2):
    g, u = jnp.split(x @ w13, 2, axis=-1)
    return (jax.nn.silu(g) * u) @ w2


def partial_rope(x, pos, rot_dim):
    half = rot_dim // 2
    inv = ROPE_THETA ** (-jnp.arange(half, dtype=jnp.float32) / half)
    ang = pos.astype(jnp.float32)[:, None] * inv[None, :]
    ang = ang.reshape((ang.shape[0],) + (1,) * (x.ndim - 3) + (half,))
    cos, sin = jnp.cos(ang), jnp.sin(ang)
    xf = x[..., :rot_dim].astype(jnp.float32)
    x1, x2 = xf[..., :half], xf[..., half:]
    rot = jnp.concatenate([x1 * cos - x2 * sin, x2 * cos + x1 * sin], axis=-1).astype(x.dtype)
    return jnp.concatenate([rot, x[..., rot_dim:]], axis=-1)


def causal_dwconv(x, state, w, b=None):
    width = w.shape[0]
    T = x.shape[1]
    xp = jnp.concatenate([state.astype(x.dtype), x], axis=1)
    out = xp[:, 0:T] * w[0]
    for j in range(1, width):
        out = out + xp[:, j:j + T] * w[j]
    if b is not None:
        out = out + b
    return out, xp[:, T:]


def rg_lru(x, h0, wa, ba, wx, bx, lam):
    B, T, C = x.shape
    xb = x.reshape(B, T, LRU_BLOCKS, LRU_BW)
    r = jax.nn.sigmoid((jnp.einsum('btnc,ncd->btnd', xb, wa).reshape(B, T, C) + ba).astype(jnp.float32))
    i = jax.nn.sigmoid((jnp.einsum('btnc,ncd->btnd', xb, wx).reshape(B, T, C) + bx).astype(jnp.float32))
    log_a = -LRU_C * r * jax.nn.softplus(-lam.astype(jnp.float32))
    a = jnp.exp(log_a)
    u = jnp.sqrt(-jnp.expm1(2.0 * log_a)) * i * x.astype(jnp.float32)
    u = u.at[:, 0].add(a[:, 0] * h0.astype(jnp.float32))

    def combine(e1, e2):
        a1, b1 = e1
        a2, b2 = e2
        return a1 * a2, a2 * b1 + b2

    _, hs = lax.associative_scan(combine, (a, u), axis=1)
    return hs.astype(x.dtype), hs[:, -1].astype(x.dtype)


def dsa_block(q, qi, wi, q_pos, k, v, ki, k_pos, n_sel):
    s_idx = jnp.einsum('bqhd,bsd->bqhs', qi, ki).astype(jnp.float32) * IDX_DIM ** -0.5
    score = jnp.einsum('bqhs,bqh->bqs', jax.nn.relu(s_idx), wi.astype(jnp.float32))
    adm = (k_pos // CHUNK)[None, :] <= (q_pos // CHUNK)[:, None]
    score = jnp.where(adm[None], score, -jnp.inf)
    top_val, top_idx = lax.top_k(score, n_sel)
    valid = jnp.isfinite(top_val)
    gather = jax.vmap(lambda rows, ids: rows[ids])
    kg = gather(k, top_idx)
    vg = gather(v, top_idx)
    logits = jnp.einsum('bqhd,bqkhd->bqhk', q, kg).astype(jnp.float32) * HEAD_DIM ** -0.5
    logits = jnp.where(valid[:, :, None, :], logits, -jnp.inf)
    p = jax.nn.softmax(logits, axis=-1).astype(v.dtype)
    return jnp.einsum('bqhk,bqkhd->bqhd', p, vg)


def fox_block(q, fq, q_pos, k, v, fk, k_pos):
    logits = jnp.einsum('bqhd,bshd->bhqs', q, k).astype(jnp.float32) * HEAD_DIM ** -0.5
    logits = logits + jnp.transpose(fq, (0, 2, 1))[..., :, None] - jnp.transpose(fk, (0, 2, 1))[..., None, :]
    causal = k_pos[None, :] <= q_pos[:, None]
    logits = jnp.where(causal[None, None], logits, -jnp.inf)
    p = jax.nn.softmax(logits, axis=-1).astype(v.dtype)
    return jnp.einsum('bhqs,bshd->bqhd', p, v)


def sweep_queries(fn, blk, q_args, q_pos, kv_args):
    T = q_pos.shape[0]
    blk = min(blk, T)
    nb = T // blk

    def to_blocks(a):
        return jnp.moveaxis(a.reshape((a.shape[0], nb, blk) + a.shape[2:]), 1, 0)

    qb = tuple(to_blocks(a) for a in q_args)
    pb = q_pos.reshape(nb, blk)
    out = lax.map(lambda args: fn(*args[0], args[1], *kv_args), (qb, pb))
    out = jnp.moveaxis(out, 0, 1)
    return out.reshape((out.shape[0], T) + out.shape[3:])


def token_mix(h, past, w_in, b_f, conv_a_w, conv_d_w, conv_d_b, lru_wa, lru_ba, lru_wx, lru_bx, lru_lam, w_out):
    pk_b, pv_b, pki_b, pk_c, pv_c, plf_c, st_a, st_d, h0 = past
    B, T, _ = h.shape
    pos0 = pk_b.shape[1]
    (a_h, a_b, a_c, bq, bk, bv, bqi, bki, bwi, cq, ck, cv, cf, dx, dg) = split_cols(h @ w_in, IN_SIZES)
    q_pos = pos0 + jnp.arange(T, dtype=jnp.int32)

    conv_out, new_st_a = causal_dwconv(a_c * a_h, st_a, conv_a_w)
    y_a = a_b * conv_out

    bq = partial_rope(bq.reshape(B, T, H_B, HEAD_DIM), q_pos, ROPE_DIM)
    bk = partial_rope(bk.reshape(B, T, H_B, HEAD_DIM), q_pos, ROPE_DIM)
    bv = bv.reshape(B, T, H_B, HEAD_DIM)
    bqi = partial_rope(bqi.reshape(B, T, IDX_HEADS, IDX_DIM), q_pos, IDX_ROPE_DIM)
    bki = partial_rope(bki, q_pos, IDX_ROPE_DIM)
    kb_all = jnp.concatenate([pk_b.astype(h.dtype), bk], axis=1)
    vb_all = jnp.concatenate([pv_b.astype(h.dtype), bv], axis=1)
    kib_all = jnp.concatenate([pki_b.astype(h.dtype), bki], axis=1)
    S = pos0 + T
    k_pos = jnp.arange(S, dtype=jnp.int32)
    n_sel = min(TOPK_MAX, S // 4)
    y_b = sweep_queries(functools.partial(dsa_block, n_sel=n_sel), QBLK_DSA,
                        (bq, bqi, bwi * IDX_HEADS ** -0.5), q_pos, (kb_all, vb_all, kib_all, k_pos))

    logf = jax.nn.log_sigmoid((cf + b_f).astype(jnp.float32))
    F = jnp.cumsum(jnp.concatenate([plf_c.astype(jnp.float32), logf], axis=1), axis=1)
    cq = cq.reshape(B, T, H_C, HEAD_DIM)
    ck = ck.reshape(B, T, H_C, HEAD_DIM)
    cv = cv.reshape(B, T, H_C, HEAD_DIM)
    kc_all = jnp.concatenate([pk_c.astype(h.dtype), ck], axis=1)
    vc_all = jnp.concatenate([pv_c.astype(h.dtype), cv], axis=1)
    y_c = sweep_queries(fox_block, QBLK_FOX, (cq, F[:, pos0:]), q_pos, (kc_all, vc_all, F, k_pos))

    xc, new_st_d = causal_dwconv(dx, st_d, conv_d_w, conv_d_b)
    h_lru, h_last = rg_lru(xc, h0, lru_wa, lru_ba, lru_wx, lru_bx, lru_lam)
    y_d = jax.nn.gelu(dg) * h_lru

    y = jnp.concatenate([y_a, y_b.reshape(B, T, W_B), y_c.reshape(B, T, W_C), y_d], axis=-1) @ w_out
    new_state = (bk, bv, bki, ck, cv, logf.astype(h.dtype), new_st_a, new_st_d, h_last)
    return y, new_state


def trunk(x, past, ln_g, ln_b, ffn_w13, ffn_w2, w_in, fox_b_f, conv_a_w, conv_d_w, conv_d_b,
          lru_wa, lru_ba, lru_wx, lru_bx, lru_lambda, w_out):
    layer_states = []
    for l in range(DEPTH):
        x = layer_norm(ALPHA * x + 0.5 * swiglu(x, ffn_w13[l, 0], ffn_w2[l, 0]), ln_g[l, 0], ln_b[l, 0])
        y, st = token_mix(x, tuple(c[l] for c in past), w_in[l], fox_b_f[l], conv_a_w[l], conv_d_w[l],
                          conv_d_b[l], lru_wa[l], lru_ba[l], lru_wx[l], lru_bx[l], lru_lambda[l], w_out[l])
        x = layer_norm(ALPHA * x + y, ln_g[l, 1], ln_b[l, 1])
        x = layer_norm(ALPHA * x + 0.5 * swiglu(x, ffn_w13[l, 1], ffn_w2[l, 1]), ln_g[l, 2], ln_b[l, 2])
        layer_states.append(st)
    new_state = tuple(jnp.stack([st[i] for st in layer_states]) for i in range(len(layer_states[0])))
    return x, new_state


def setup_inputs(seed: int = 0) -> dict:
    key = jax.random.key(seed)
    ks = jax.random.split(key, 32)

    def nrm(k, shape, scale):
        return jax.random.normal(k, shape, jnp.float32) * scale

    u = jax.random.uniform(ks[24], (DEPTH, W_D), jnp.float32, 0.9, 0.999)
    a = u ** (1.0 / LRU_C)
    return {
        "x_prompt": nrm(ks[0], (BATCH, SEQ, D_MODEL), 1.0),
        "x_sample": nrm(ks[1], (DEC_BATCH, DEC_SEQ, D_MODEL), 1.0),
        "cache_dsa_k": nrm(ks[2], (DEPTH, DEC_BATCH, PAST_LEN, H_B, HEAD_DIM), 1.0),
        "cache_dsa_v": nrm(ks[3], (DEPTH, DEC_BATCH, PAST_LEN, H_B, HEAD_DIM), 1.0),
        "cache_dsa_kidx": nrm(ks[4], (DEPTH, DEC_BATCH, PAST_LEN, IDX_DIM), 1.0),
        "cache_fox_k": nrm(ks[5], (DEPTH, DEC_BATCH, PAST_LEN, H_C, HEAD_DIM), 1.0),
        "cache_fox_v": nrm(ks[6], (DEPTH, DEC_BATCH, PAST_LEN, H_C, HEAD_DIM), 1.0),
        "cache_fox_logf": jax.nn.log_sigmoid(nrm(ks[7], (DEPTH, DEC_BATCH, PAST_LEN, H_C), 1.0) + 3.0),
        "state_conv_a": nrm(ks[8], (DEPTH, DEC_BATCH, CONV_A_W - 1, W_A), 1.0),
        "state_conv_d": nrm(ks[9], (DEPTH, DEC_BATCH, CONV_D_W - 1, W_D), 1.0),
        "state_lru": nrm(ks[10], (DEPTH, DEC_BATCH, W_D), 0.5),
        "ln_g": 1.0 + nrm(ks[11], (DEPTH, 3, D_MODEL), 0.02),
        "ln_b": nrm(ks[12], (DEPTH, 3, D_MODEL), 0.02),
        "ffn_w13": nrm(ks[13], (DEPTH, 2, D_MODEL, 2 * D_FF), D_MODEL ** -0.5),
        "ffn_w2": nrm(ks[14], (DEPTH, 2, D_FF, D_MODEL), BETA * D_FF ** -0.5),
        "w_in": nrm(ks[15], (DEPTH, D_MODEL, N_IN), D_MODEL ** -0.5),
        "fox_b_f": jax.random.uniform(ks[16], (DEPTH, H_C), jnp.float32, 1.0, 5.0),
        "conv_a_w": nrm(ks[17], (DEPTH, CONV_A_W, W_A), CONV_A_W ** -0.5),
        "conv_d_w": nrm(ks[18], (DEPTH, CONV_D_W, W_D), CONV_D_W ** -0.5),
        "conv_d_b": nrm(ks[19], (DEPTH, W_D), 0.02),
        "lru_wa": nrm(ks[20], (DEPTH, LRU_BLOCKS, LRU_BW, LRU_BW), LRU_BW ** -0.5),
        "lru_ba": nrm(ks[21], (DEPTH, W_D), 0.02),
        "lru_wx": nrm(ks[22], (DEPTH, LRU_BLOCKS, LRU_BW, LRU_BW), LRU_BW ** -0.5),
        "lru_bx": nrm(ks[23], (DEPTH, W_D), 0.02),
        "lru_lambda": jnp.log(a) - jnp.log1p(-a),
        "w_out": nrm(ks[25], (DEPTH, W_MIX, D_MODEL), BETA * W_MIX ** -0.5),
    }


def reference(x_prompt, x_sample, cache_dsa_k, cache_dsa_v, cache_dsa_kidx, cache_fox_k, cache_fox_v,
              cache_fox_logf, state_conv_a, state_conv_d, state_lru, ln_g, ln_b, ffn_w13, ffn_w2, w_in,
              fox_b_f, conv_a_w, conv_d_w, conv_d_b, lru_wa, lru_ba, lru_wx, lru_bx, lru_lambda, w_out):
    dt = x_prompt.dtype
    B = x_prompt.shape[0]
    empty_past = (
        jnp.zeros((DEPTH, B, 0, H_B, HEAD_DIM), dt),
        jnp.zeros((DEPTH, B, 0, H_B, HEAD_DIM), dt),
        jnp.zeros((DEPTH, B, 0, IDX_DIM), dt),
        jnp.zeros((DEPTH, B, 0, H_C, HEAD_DIM), dt),
        jnp.zeros((DEPTH, B, 0, H_C, HEAD_DIM), dt),
        jnp.zeros((DEPTH, B, 0, H_C), dt),
        jnp.zeros((DEPTH, B, CONV_A_W - 1, W_A), dt),
        jnp.zeros((DEPTH, B, CONV_D_W - 1, W_D), dt),
        jnp.zeros((DEPTH, B, W_D), dt),
    )
    sample_past = (cache_dsa_k, cache_dsa_v, cache_dsa_kidx, cache_fox_k, cache_fox_v, cache_fox_logf,
                   state_conv_a, state_conv_d, state_lru)
    y_prompt, p_state = trunk(x_prompt, empty_past, ln_g, ln_b, ffn_w13, ffn_w2, w_in, fox_b_f, conv_a_w,
                              conv_d_w, conv_d_b, lru_wa, lru_ba, lru_wx, lru_bx, lru_lambda, w_out)
    y_sample, s_state = trunk(x_sample, sample_past, ln_g, ln_b, ffn_w13, ffn_w2, w_in, fox_b_f, conv_a_w,
                              conv_d_w, conv_d_b, lru_wa, lru_ba, lru_wx, lru_bx, lru_lambda, w_out)
    p_dsa_k, p_dsa_v, p_dsa_kidx, p_fox_k, p_fox_v, p_fox_logf, p_conv_a, p_conv_d, p_lru = p_state
    s_dsa_k, s_dsa_v, s_dsa_kidx, s_fox_k, s_fox_v, s_fox_logf, s_conv_a, s_conv_d, s_lru = s_state
    return (y_prompt, y_sample, p_dsa_k, p_dsa_v, p_dsa_kidx, p_fox_k, p_fox_v, p_fox_logf, p_conv_a, p_conv_d, p_lru,
            s_dsa_k, s_dsa_v, s_dsa_kidx, s_fox_k, s_fox_v, s_fox_logf, s_conv_a, s_conv_d, s_lru)
```

```python
import functools
import math

import jax
import jax.numpy as jnp
from jax import lax
from jax.experimental import pallas as pl
from jax.experimental.pallas import tpu as pltpu

F32 = jnp.float32
BF16 = jnp.bfloat16

CHUNK = 64
HEAD_DIM = 128
ROPE_DIM = HEAD_DIM // 4
ROPE_THETA = 500000.0
IDX_HEADS = 16
IDX_DIM = 64
IDX_ROPE_DIM = IDX_DIM // 4
TOPK_MAX = 256
CONV_A_W = 3
CONV_D_W = 4
LRU_C = 8.0
LN_EPS = 1e-5
LANES = 128
SMALL_W = 512
VMEM_LIMIT = 56 * 1024 * 1024
INT_MIN = -2 ** 31


def _cparams(n_axes):
    return pltpu.CompilerParams(dimension_semantics=("arbitrary",) * n_axes,
                                vmem_limit_bytes=VMEM_LIMIT)


def _pick(n, cands):
    for c in cands:
        if n % c == 0:
            return c
    raise ValueError(f"no tile for {n} in {cands}")


def _mm_kernel(x_ref, w_ref, o_ref):
    o_ref[...] = jnp.dot(x_ref[...], w_ref[...], preferred_element_type=F32).astype(o_ref.dtype)


def matmul(x, w, out_dtype):
    M, K = x.shape
    N = w.shape[1]
    tm = _pick(M, (768, 512, 384, 256, 128, 64))
    tn = _pick(N, (512, 256, 128))
    return pl.pallas_call(
        _mm_kernel,
        grid=(M // tm, N // tn),
        in_specs=[pl.BlockSpec((tm, K), lambda i, j: (i, 0)),
                  pl.BlockSpec((K, tn), lambda i, j: (0, j))],
        out_specs=pl.BlockSpec((tm, tn), lambda i, j: (i, j)),
        out_shape=jax.ShapeDtypeStruct((M, N), out_dtype),
        compiler_params=_cparams(2),
    )(x, w)


def _swiglu_kernel(x_ref, wg_ref, wu_ref, o_ref):
    x = x_ref[...]
    g = jnp.dot(x, wg_ref[...], preferred_element_type=F32)
    u = jnp.dot(x, wu_ref[...], preferred_element_type=F32)
    o_ref[...] = (g * jax.nn.sigmoid(g) * u).astype(o_ref.dtype)


def swiglu_up(x, w13):
    M, K = x.shape
    F = w13.shape[1] // 2
    tm = _pick(M, (768, 512, 384, 256, 128, 64))
    tn = _pick(F, (512, 256, 128))
    nb = F // tn
    return pl.pallas_call(
        _swiglu_kernel,
        grid=(M // tm, nb),
        in_specs=[pl.BlockSpec((tm, K), lambda i, j: (i, 0)),
                  pl.BlockSpec((K, tn), lambda i, j: (0, j)),
                  pl.BlockSpec((K, tn), lambda i, j: (0, j + nb))],
        out_specs=pl.BlockSpec((tm, tn), lambda i, j: (i, j)),
        out_shape=jax.ShapeDtypeStruct((M, F), BF16),
        compiler_params=_cparams(2),
    )(x, w13, w13)


def _res_ln_kernel(x_ref, y_ref, g_ref, b_ref, o_ref, ob_ref, *, alpha, yscale):
    z = alpha * x_ref[...] + yscale * y_ref[...]
    mu = jnp.mean(z, axis=-1, keepdims=True)
    zc = z - mu
    var = jnp.mean(zc * zc, axis=-1, keepdims=True)
    o = zc * lax.rsqrt(var + LN_EPS) * g_ref[...] + b_ref[...]
    o_ref[...] = o
    ob_ref[...] = o.astype(BF16)


def residual_layer_norm(x, y, g, b, alpha, yscale):
    M, D = x.shape
    tm = _pick(M, (256, 128, 64))
    row = pl.BlockSpec((tm, D), lambda i: (i, 0))
    vec = pl.BlockSpec((1, D), lambda i: (0, 0))
    return pl.pallas_call(
        functools.partial(_res_ln_kernel, alpha=alpha, yscale=yscale),
        grid=(M // tm,),
        in_specs=[row, row, vec, vec],
        out_specs=[row, row],
        out_shape=[jax.ShapeDtypeStruct((M, D), F32), jax.ShapeDtypeStruct((M, D), BF16)],
        compiler_params=_cparams(1),
    )(x, y, g.reshape(1, D), b.reshape(1, D))


def _rope_tables(pos, period, half):
    inv = ROPE_THETA ** (-jnp.arange(half, dtype=F32) / half)
    ang = pos.astype(F32)[:, None] * inv[None, :]
    cos, sin = jnp.cos(ang), jnp.sin(ang)
    T = pos.shape[0]
    reps = LANES // period
    zeros = jnp.zeros((T, period - 2 * half), F32)
    c = jnp.concatenate([cos, cos, jnp.ones((T, period - 2 * half), F32)], axis=1)
    s1 = jnp.concatenate([-sin, jnp.zeros((T, half), F32), zeros], axis=1)
    s2 = jnp.concatenate([jnp.zeros((T, half), F32), sin, zeros], axis=1)
    return [jnp.tile(t, (1, reps)) for t in (c, s1, s2)]


def _rope(x, c, s1, s2, half):
    outs = []
    for g in range(x.shape[-1] // LANES):
        xg = x[:, g * LANES:(g + 1) * LANES]
        outs.append(xg * c + pltpu.roll(xg, LANES - half, 1) * s1 + pltpu.roll(xg, half, 1) * s2)
    return outs[0] if len(outs) == 1 else jnp.concatenate(outs, axis=1)


def _prep_kernel(bq_ref, bk_ref, bv_ref, bqi_ref, cq_ref, ck_ref, cv_ref, sm_ref, tab_ref, bf_ref,
                 q_o, kf_o, kb_o, vb_o, qi_o, smf_o, kilo_o, kihi_o, cq_o, ck_o, cv_o, lf_o):
    t = tab_ref[...]
    q_o[...] = _rope(bq_ref[...], t[0], t[1], t[2], ROPE_DIM // 2).astype(BF16)
    k = _rope(bk_ref[...], t[0], t[1], t[2], ROPE_DIM // 2)
    kf_o[...] = k
    kb_o[...] = k.astype(BF16)
    vb_o[...] = bv_ref[...].astype(BF16)
    qi_o[...] = _rope(bqi_ref[...], t[3], t[4], t[5], IDX_ROPE_DIM // 2).astype(BF16)
    sm = _rope(sm_ref[...], t[6], t[7], t[8], IDX_ROPE_DIM // 2)
    smf_o[...] = sm
    lane = lax.broadcasted_iota(jnp.int32, sm.shape, 1)
    ki_lo = jnp.where(lane < IDX_DIM, sm, 0.0)
    kilo_o[...] = ki_lo.astype(BF16)
    kihi_o[...] = pltpu.roll(ki_lo, IDX_DIM, 1).astype(BF16)
    cq_o[...] = cq_ref[...].astype(BF16)
    ck_o[...] = ck_ref[...].astype(BF16)
    cv_o[...] = cv_ref[...].astype(BF16)
    z = -(sm + bf_ref[...])
    lf_o[...] = -(jnp.maximum(z, 0.0) + jnp.log1p(jnp.exp(-jnp.abs(z))))


def prep_group(P, tabs, bf_row, grp, W):
    row0, B, T = grp["row0"], grp["B"], grp["T"]
    tt = _pick(T, (256, 128, 64))
    nt = T // tt
    rb0 = row0 // tt
    wb = W // LANES * LANES
    assert wb == W

    def col(cb, width):
        return pl.BlockSpec((tt, width), lambda b, i, cb=cb: (rb0 + b * nt + i, cb))

    def out(width):
        return pl.BlockSpec((tt, width), lambda b, i: (b * nt + i, 0))

    R = B * T
    small_cb = (12 * W) // LANES
    in_specs = [col(3, W), col(4, W), col(5, W), col(6, W), col(7, W), col(8, W), col(9, W),
                col(small_cb, LANES),
                pl.BlockSpec((9, tt, LANES), lambda b, i: (0, i, 0)),
                pl.BlockSpec((1, LANES), lambda b, i: (0, 0))]
    outs = [(W, BF16), (W, F32), (W, BF16), (W, BF16), (W, BF16), (LANES, F32), (LANES, BF16),
            (LANES, BF16), (W, BF16), (W, BF16), (W, BF16), (LANES, F32)]
    return pl.pallas_call(
        _prep_kernel,
        grid=(B, nt),
        in_specs=in_specs,
        out_specs=[out(w) for w, _ in outs],
        out_shape=[jax.ShapeDtypeStruct((R, w), d) for w, d in outs],
        compiler_params=_cparams(2),
    )(P, P, P, P, P, P, P, P, tabs, bf_row)


def _shift_rows(x, d, fill_rows):
    T = x.shape[0]
    row = lax.broadcasted_iota(jnp.int32, x.shape, 0)
    y = pltpu.roll(x, d, 0)
    for t in range(d):
        y = jnp.where(row == t, fill_rows[t:t + 1, :], y)
    return y


def _conv_a_kernel(ah_ref, ab_ref, ac_ref, st_ref, w_ref, y_ref, nst_ref):
    z = ac_ref[...] * ah_ref[...]
    st = st_ref[0]
    w = w_ref[...]
    T = z.shape[0]
    z1 = _shift_rows(z, 1, st[1:2, :])
    z2 = _shift_rows(z, 2, st)
    conv = z2 * w[0:1, :] + z1 * w[1:2, :] + z * w[2:3, :]
    y_ref[...] = (ab_ref[...] * conv).astype(y_ref.dtype)
    nst_ref[0] = z[T - 2:T, :]


def conv_a_group(P, state_a, conv_w, grp, W):
    row0, B, T = grp["row0"], grp["B"], grp["T"]
    assert T >= CONV_A_W - 1
    tc = _pick(W, (256, 128))
    nc = W // tc
    rb0 = row0 // T

    def col(k):
        return pl.BlockSpec((T, tc), lambda b, c, k=k: (rb0 + b, k * nc + c))

    return pl.pallas_call(
        _conv_a_kernel,
        grid=(B, nc),
        in_specs=[col(0), col(1), col(2),
                  pl.BlockSpec((1, CONV_A_W - 1, tc), lambda b, c: (b, 0, c)),
                  pl.BlockSpec((CONV_A_W, tc), lambda b, c: (0, c))],
        out_specs=[pl.BlockSpec((T, tc), lambda b, c: (b, c)),
                   pl.BlockSpec((1, CONV_A_W - 1, tc), lambda b, c: (b, 0, c))],
        out_shape=[jax.ShapeDtypeStruct((B * T, W), BF16),
                   jax.ShapeDtypeStruct((B, CONV_A_W - 1, W), F32)],
        compiler_params=_cparams(2),
    )(P, P, P, state_a, conv_w)


def _lru_kernel(dx_ref, dg_ref, st_ref, h0_ref, cw_ref, cb_ref, wa_ref, ba_ref, wx_ref, bx_ref,
                lam_ref, y_ref, nst_ref, hl_ref):
    x = dx_ref[...]
    T = x.shape[0]
    st = st_ref[0]
    w = cw_ref[...]
    x1 = _shift_rows(x, 1, st[2:3, :])
    x2 = _shift_rows(x, 2, st[1:3, :])
    x3 = _shift_rows(x, 3, st)
    xc = x3 * w[0:1, :] + x2 * w[1:2, :] + x1 * w[2:3, :] + x * w[3:4, :] + cb_ref[...]
    xb = xc.astype(BF16)
    r = jax.nn.sigmoid(jnp.dot(xb, wa_ref[0], preferred_element_type=F32) + ba_ref[...])
    i = jax.nn.sigmoid(jnp.dot(xb, wx_ref[0], preferred_element_type=F32) + bx_ref[...])
    nl = -lam_ref[...]
    sp = jnp.maximum(nl, 0.0) + jnp.log1p(jnp.exp(-jnp.abs(nl)))
    log_a = -LRU_C * r * sp
    a = jnp.exp(log_a)
    u = jnp.sqrt(-jnp.tanh(log_a) * (a * a + 1.0)) * i * xc
    row = lax.broadcasted_iota(jnp.int32, x.shape, 0)
    u = jnp.where(row == 0, u + a * h0_ref[0], u)
    d = 1
    while d < T:
        keep = row >= d
        a_prev = jnp.where(keep, pltpu.roll(a, d, 0), 1.0)
        u_prev = jnp.where(keep, pltpu.roll(u, d, 0), 0.0)
        u = a * u_prev + u
        a = a * a_prev
        d *= 2
    g = dg_ref[...]
    y_ref[...] = (jax.nn.gelu(g) * u).astype(y_ref.dtype)
    nst_ref[0] = x[T - 3:T, :]
    hl_ref[0] = u[T - 1:T, :]


def lru_group(P, state_d, h0, conv_w, conv_b, wa, ba, wx, bx, lam, grp, W):
    row0, B, T = grp["row0"], grp["B"], grp["T"]
    assert T >= CONV_D_W - 1
    nblk, bw = wa.shape[0], wa.shape[1]
    assert bw % LANES == 0 and nblk * bw == W
    rb0 = row0 // T
    cb = (10 * W) // bw

    def col(k):
        return pl.BlockSpec((T, bw), lambda b, c, k=k: (rb0 + b, cb + k * nblk + c))

    vec = pl.BlockSpec((1, bw), lambda b, c: (0, c))
    return pl.pallas_call(
        _lru_kernel,
        grid=(B, nblk),
        in_specs=[col(0), col(1),
                  pl.BlockSpec((1, CONV_D_W - 1, bw), lambda b, c: (b, 0, c)),
                  pl.BlockSpec((1, 1, bw), lambda b, c: (b, 0, c)),
                  pl.BlockSpec((CONV_D_W, bw), lambda b, c: (0, c)),
                  vec,
                  pl.BlockSpec((1, bw, bw), lambda b, c: (c, 0, 0)), vec,
                  pl.BlockSpec((1, bw, bw), lambda b, c: (c, 0, 0)), vec,
                  vec],
        out_specs=[pl.BlockSpec((T, bw), lambda b, c: (b, c)),
                   pl.BlockSpec((1, CONV_D_W - 1, bw), lambda b, c: (b, 0, c)),
                   pl.BlockSpec((1, 1, bw), lambda b, c: (b, 0, c))],
        out_shape=[jax.ShapeDtypeStruct((B * T, W), BF16),
                   jax.ShapeDtypeStruct((B, CONV_D_W - 1, W), F32),
                   jax.ShapeDtypeStruct((B, 1, W), F32)],
        compiler_params=_cparams(2),
    )(P, P, state_d, h0.reshape(B, 1, W), conv_w, conv_b.reshape(1, W), wa, ba.reshape(1, W),
      wx, bx.reshape(1, W), lam.reshape(1, W))


def _order_key(x):
    bits = pltpu.bitcast(x + 0.0, jnp.int32)
    return jnp.where(bits < 0, bits ^ jnp.int32(0x7FFFFFFF), bits)


def _dsa_kernel(q_ref, qi_ref, sm_ref, k_ref, v_ref, kilo_ref, kihi_ref, o_ref, *,
                chunk0, s_valid, n_sel, n_heads):
    c = pl.program_id(1)
    tq = q_ref.shape[0]
    S = k_ref.shape[0]
    nt = (((1,), (1,)), ((), ()))
    sm = sm_ref[...]
    kilo = kilo_ref[...]
    kihi = kihi_ref[...]
    score = jnp.zeros((tq, S), F32)
    for j in range(IDX_HEADS // 2):
        qp = qi_ref[:, j * LANES:(j + 1) * LANES]
        for half, kk in ((0, kilo), (1, kihi)):
            h = 2 * j + half
            s = lax.dot_general(qp, kk, nt, preferred_element_type=F32) * (IDX_DIM ** -0.5)
            w = sm[:, IDX_DIM + h:IDX_DIM + h + 1] * (IDX_HEADS ** -0.5)
            score = score + jnp.maximum(s, 0.0) * w
    kpos = lax.broadcasted_iota(jnp.int32, (tq, S), 1)
    lim = jnp.minimum((chunk0 + c + 1) * CHUNK, s_valid)
    adm = kpos < lim
    key = jnp.where(adm, _order_key(score), jnp.int32(INT_MIN))

    def bit_step(b, t):
        cand = t + lax.shift_left(jnp.int32(1), 31 - b)
        cnt = jnp.sum(jnp.where(key >= cand, 1.0, 0.0), axis=-1, keepdims=True)
        return jnp.where(cnt >= float(n_sel), cand, t)

    thr = lax.fori_loop(0, 32, bit_step, jnp.full((tq, 1), INT_MIN, jnp.int32))
    sel = (key >= thr) & adm

    scale = HEAD_DIM ** -0.5
    for h in range(n_heads):
        sl = slice(h * HEAD_DIM, (h + 1) * HEAD_DIM)
        lg = lax.dot_general(q_ref[:, sl], k_ref[:, sl], nt, preferred_element_type=F32) * scale
        lg = jnp.where(sel, lg, -jnp.inf)
        m = jnp.max(lg, axis=-1, keepdims=True)
        p = jnp.exp(lg - m)
        l = jnp.sum(p, axis=-1, keepdims=True)
        o = jnp.dot(p.astype(BF16), v_ref[:, sl], preferred_element_type=F32)
        o_ref[:, sl] = (o / l).astype(o_ref.dtype)


def dsa_group(q, qi, smf, k_all, v_all, kilo_all, kihi_all, grp, W):
    B, T, S, S_pad, pos0 = grp["B"], grp["T"], grp["S"], grp["S_pad"], grp["pos0"]
    assert pos0 % CHUNK == 0 and T % CHUNK == 0
    nq = T // CHUNK
    qrow = lambda w: pl.BlockSpec((CHUNK, w), lambda b, c: (b * nq + c, 0))
    krow = lambda w: pl.BlockSpec((S_pad, w), lambda b, c: (b, 0))
    kern = functools.partial(_dsa_kernel, chunk0=pos0 // CHUNK, s_valid=S,
                             n_sel=min(TOPK_MAX, S // 4), n_heads=W // HEAD_DIM)
    return pl.pallas_call(
        kern,
        grid=(B, nq),
        in_specs=[qrow(W), qrow(IDX_HEADS * IDX_DIM), qrow(LANES),
                  krow(W), krow(W), krow(LANES), krow(LANES)],
        out_specs=qrow(W),
        out_shape=jax.ShapeDtypeStruct((B * T, W), BF16),
        compiler_params=_cparams(2),
    )(q, qi, smf, k_all, v_all, kilo_all, kihi_all)


def _cumsum_kernel(x_ref, o_ref):
    x = x_ref[...]
    n = x.shape[0]
    row = lax.broadcasted_iota(jnp.int32, x.shape, 0)
    d = 1
    while d < n:
        x = x + jnp.where(row >= d, pltpu.roll(x, d, 0), 0.0)
        d *= 2
    o_ref[...] = x


def cumsum_rows(x, B, n):
    spec = pl.BlockSpec((n, LANES), lambda b: (b, 0))
    return pl.pallas_call(
        _cumsum_kernel, grid=(B,), in_specs=[spec], out_specs=spec,
        out_shape=jax.ShapeDtypeStruct((B * n, LANES), F32),
        compiler_params=_cparams(1),
    )(x)


def _fox_kernel(q_ref, fq_ref, k_ref, v_ref, fk_ref, o_ref, *, pos0, s_valid, fcol0, n_heads):
    i = pl.program_id(1)
    tq = q_ref.shape[0]
    S = k_ref.shape[0]
    nt = (((1,), (1,)), ((), ()))
    qpos = pos0 + i * tq + lax.broadcasted_iota(jnp.int32, (tq, S), 0)
    kpos = lax.broadcasted_iota(jnp.int32, (tq, S), 1)
    ok = (kpos <= qpos) & (kpos < s_valid)
    fq = fq_ref[...]
    fk = fk_ref[0]
    scale = HEAD_DIM ** -0.5
    for h in range(n_heads):
        sl = slice(h * HEAD_DIM, (h + 1) * HEAD_DIM)
        lg = lax.dot_general(q_ref[:, sl], k_ref[:, sl], nt, preferred_element_type=F32) * scale
        lg = lg + fq[:, fcol0 + h:fcol0 + h + 1] - fk[h:h + 1, :]
        lg = jnp.where(ok, lg, -jnp.inf)
        m = jnp.max(lg, axis=-1, keepdims=True)
        p = jnp.exp(lg - m)
        l = jnp.sum(p, axis=-1, keepdims=True)
        o = jnp.dot(p.astype(BF16), v_ref[:, sl], preferred_element_type=F32)
        o_ref[:, sl] = (o / l).astype(o_ref.dtype)


def fox_group(q, Fq, k_all, v_all, Fk_t, grp, W, fcol0):
    B, T, S, S_pad, pos0 = grp["B"], grp["T"], grp["S"], grp["S_pad"], grp["pos0"]
    H = W // HEAD_DIM
    tq = _pick(T, (256, 128, 64))
    nq = T // tq
    qrow = lambda w: pl.BlockSpec((tq, w), lambda b, i: (b * nq + i, 0))
    krow = lambda w: pl.BlockSpec((S_pad, w), lambda b, i: (b, 0))
    kern = functools.partial(_fox_kernel, pos0=pos0, s_valid=S, fcol0=fcol0, n_heads=H)
    return pl.pallas_call(
        kern,
        grid=(B, nq),
        in_specs=[qrow(W), qrow(LANES), krow(W), krow(W),
                  pl.BlockSpec((1, H, S_pad), lambda b, i: (b, 0, 0))],
        out_specs=qrow(W),
        out_shape=jax.ShapeDtypeStruct((B * T, W), BF16),
        compiler_params=_cparams(2),
    )(q, Fq, k_all, v_all, Fk_t)


def _pack_w_in(w_in, W, H_C):
    sizes = (W, W, W, W, W, W, IDX_HEADS * IDX_DIM, IDX_DIM, IDX_HEADS, W, W, W, H_C, W, W)
    offs = [0]
    for s in sizes:
        offs.append(offs[-1] + s)
    seg = lambda k: w_in[:, offs[k]:offs[k + 1]]
    D = w_in.shape[0]
    small = jnp.concatenate([seg(7), seg(8), seg(12),
                             jnp.zeros((D, SMALL_W - IDX_DIM - IDX_HEADS - H_C), w_in.dtype)], axis=1)
    order = [0, 1, 2, 3, 4, 5, 6, 9, 10, 11, 13, 14]
    return jnp.concatenate([seg(k) for k in order] + [small], axis=1).astype(BF16)


def _pad_rows(parts, B):
    x = jnp.concatenate(parts, axis=1)
    return x.reshape(B * x.shape[1], x.shape[2])


def _token_mix(xb, layer_w, groups, pasts):
    W = layer_w["W"]
    H = W // HEAD_DIM
    fcol0 = IDX_DIM + IDX_HEADS
    P = matmul(xb, layer_w["w_in"], F32)
    bf_row = jnp.zeros((1, LANES), F32).at[0, fcol0:fcol0 + H].set(layer_w["fox_b_f"])
    ys, states = [], []
    for grp, past in zip(groups, pasts):
        B, T, S, S_pad, pos0 = grp["B"], grp["T"], grp["S"], grp["S_pad"], grp["pos0"]
        pk_b, pv_b, pki_b, pk_c, pv_c, plf_c, st_a, st_d, h0 = past
        npast = S - T
        (q_b, k_f, k_b, v_b, qi_b, sm_f, kilo, kihi, cq_b, ck_b, cv_b, lf) = prep_group(
            P, grp["tabs"], bf_row, grp, W)

        y_a, nst_a = conv_a_group(P, st_a, layer_w["conv_a_w"], grp, W)
        y_d, nst_d, h_last = lru_group(P, st_d, h0, layer_w["conv_d_w"], layer_w["conv_d_b"],
                                       layer_w["lru_wa"], layer_w["lru_ba"], layer_w["lru_wx"],
                                       layer_w["lru_bx"], layer_w["lru_lambda"], grp, W)

        def with_past(new, past_arr, dtype, width):
            new = new.reshape(B, T, width)
            parts = []
            if npast:
                parts.append(past_arr.reshape(B, npast, -1).astype(dtype))
            parts.append(new)
            if S_pad > S:
                parts.append(jnp.zeros((B, S_pad - S, width), dtype))
            if len(parts) == 1:
                return new.reshape(B * T, width)
            return _pad_rows(parts, B)

        if npast:
            pki = jnp.pad(pki_b, ((0, 0), (0, 0), (0, LANES - IDX_DIM))).astype(BF16)
            pki_hi = jnp.pad(pki_b, ((0, 0), (0, 0), (IDX_DIM, 0))).astype(BF16)
            plf = jnp.pad(plf_c.astype(F32), ((0, 0), (0, 0), (fcol0, LANES - fcol0 - H)))
        else:
            pki = pki_hi = plf = None
        k_all = with_past(k_b, pk_b, BF16, W)
        v_all = with_past(v_b, pv_b, BF16, W)
        kilo_all = with_past(kilo, pki, BF16, LANES)
        kihi_all = with_past(kihi, pki_hi, BF16, LANES)
        y_b = dsa_group(q_b, qi_b, sm_f, k_all, v_all, kilo_all, kihi_all, grp, W)

        lf_all = with_past(lf, plf, F32, LANES)
        F_all = cumsum_rows(lf_all, B, S_pad)
        F3 = F_all.reshape(B, S_pad, LANES)
        Fq = F3[:, npast:npast + T, :].reshape(B * T, LANES)
        Fk_t = jnp.swapaxes(F3[:, :, fcol0:fcol0 + H], 1, 2)
        kc_all = with_past(ck_b, pk_c, BF16, W)
        vc_all = with_past(cv_b, pv_c, BF16, W)
        y_c = fox_group(cq_b, Fq, kc_all, vc_all, Fk_t, grp, W, fcol0)

        ys.append(jnp.concatenate([y_a, y_b, y_c, y_d], axis=1))
        r0 = grp["row0"]
        sl = lambda cb: P[r0:r0 + B * T, cb * W:(cb + 1) * W]
        states.append((
            k_f.reshape(B, T, H, HEAD_DIM),
            sl(5).reshape(B, T, H, HEAD_DIM),
            sm_f[:, :IDX_DIM].reshape(B, T, IDX_DIM),
            sl(8).reshape(B, T, H, HEAD_DIM),
            sl(9).reshape(B, T, H, HEAD_DIM),
            lf[:, fcol0:fcol0 + H].reshape(B, T, H),
            nst_a, nst_d, h_last.reshape(B, W)))
    ycat = jnp.concatenate(ys, axis=0)
    y = matmul(ycat, layer_w["w_out"], F32)
    return y, states


def kernel(x_prompt, x_sample, cache_dsa_k, cache_dsa_v, cache_dsa_kidx, cache_fox_k, cache_fox_v,
           cache_fox_logf, state_conv_a, state_conv_d, state_lru, ln_g, ln_b, ffn_w13, ffn_w2, w_in,
           fox_b_f, conv_a_w, conv_d_w, conv_d_b, lru_wa, lru_ba, lru_wx, lru_bx, lru_lambda, w_out):
    Bp, Tp, D = x_prompt.shape
    Bs, Ts, _ = x_sample.shape
    depth = ln_g.shape[0]
    past_len = cache_dsa_k.shape[2]
    W = D // 4
    H = W // HEAD_DIM
    alpha = (2.0 * depth) ** 0.25
    dt = x_prompt.dtype

    def group(row0, B, T, pos0):
        S = pos0 + T
        S_pad = -(-S // LANES) * LANES
        pos = pos0 + jnp.arange(T, dtype=jnp.int32)
        tabs = jnp.stack(_rope_tables(pos, HEAD_DIM, ROPE_DIM // 2)
                         + _rope_tables(pos, IDX_DIM, IDX_ROPE_DIM // 2)
                         + _rope_tables(pos, LANES, IDX_ROPE_DIM // 2))
        return dict(row0=row0, B=B, T=T, pos0=pos0, S=S, S_pad=S_pad, tabs=tabs)

    groups = [group(0, Bp, Tp, 0), group(Bp * Tp, Bs, Ts, past_len)]

    x = jnp.concatenate([x_prompt.reshape(Bp * Tp, D), x_sample.reshape(Bs * Ts, D)], axis=0)
    xb = x.astype(BF16)

    out_states = [[], []]
    for l in range(depth):
        lw = dict(W=W, w_in=_pack_w_in(w_in[l], W, H), fox_b_f=fox_b_f[l], conv_a_w=conv_a_w[l],
                  conv_d_w=conv_d_w[l], conv_d_b=conv_d_b[l], lru_wa=lru_wa[l].astype(BF16),
                  lru_ba=lru_ba[l], lru_wx=lru_wx[l].astype(BF16), lru_bx=lru_bx[l],
                  lru_lambda=lru_lambda[l], w_out=w_out[l].astype(BF16))
        empty = (None,) * 6 + (jnp.zeros((Bp, CONV_A_W - 1, W), dt),
                               jnp.zeros((Bp, CONV_D_W - 1, W), dt), jnp.zeros((Bp, W), dt))
        sample_past = (cache_dsa_k[l], cache_dsa_v[l], cache_dsa_kidx[l], cache_fox_k[l],
                       cache_fox_v[l], cache_fox_logf[l], state_conv_a[l], state_conv_d[l],
                       state_lru[l])

        h = swiglu_up(xb, ffn_w13[l, 0].astype(BF16))
        f = matmul(h, ffn_w2[l, 0].astype(BF16), F32)
        x, xb = residual_layer_norm(x, f, ln_g[l, 0], ln_b[l, 0], alpha, 0.5)

        y, states = _token_mix(xb, lw, groups, [empty, sample_past])
        x, xb = residual_layer_norm(x, y, ln_g[l, 1], ln_b[l, 1], alpha, 1.0)

        h = swiglu_up(xb, ffn_w13[l, 1].astype(BF16))
        f = matmul(h, ffn_w2[l, 1].astype(BF16), F32)
        x, xb = residual_layer_norm(x, f, ln_g[l, 2], ln_b[l, 2], alpha, 0.5)
        for g in range(2):
            out_states[g].append(states[g])

    y_prompt = x[:Bp * Tp].reshape(Bp, Tp, D)
    y_sample = x[Bp * Tp:].reshape(Bs, Ts, D)
    stacked = [tuple(jnp.stack([st[i] for st in out_states[g]]) for i in range(9)) for g in range(2)]
    return (y_prompt, y_sample) + stacked[0] + stacked[1]
```

```python
import functools
import math

import jax
import jax.numpy as jnp
from jax import lax
from jax.experimental import pallas as pl
from jax.experimental.pallas import tpu as pltpu

F32 = jnp.float32
BF16 = jnp.bfloat16

CHUNK = 64
HEAD_DIM = 128
ROPE_DIM = HEAD_DIM // 4
ROPE_THETA = 500000.0
IDX_HEADS = 16
IDX_DIM = 64
IDX_ROPE_DIM = IDX_DIM // 4
TOPK_MAX = 256
CONV_A_W = 3
CONV_D_W = 4
LRU_C = 8.0
LN_EPS = 1e-5
LANES = 128
SMALL_W = 512
VMEM_LIMIT = 58 * 1024 * 1024
INT_MIN = -2 ** 31
NT = (((1,), (1,)), ((), ()))


def _pcall(body, *, name, grid, in_specs, inputs, out_specs, out_shape, carries=()):
    n_in, n_c = len(inputs), len(carries)

    def kern(*refs):
        body(*refs[:n_in], *refs[n_in + n_c:])

    return pl.pallas_call(
        kern, name=name, grid=grid,
        in_specs=list(in_specs) + [pl.BlockSpec(memory_space=pl.ANY)] * n_c,
        out_specs=out_specs, out_shape=out_shape,
        input_output_aliases={n_in + k: oi for k, (oi, _) in enumerate(carries)},
        compiler_params=pltpu.CompilerParams(dimension_semantics=("arbitrary",) * len(grid),
                                             vmem_limit_bytes=VMEM_LIMIT),
    )(*inputs, *[a for _, a in carries])


def _pick(n, cands):
    for c in cands:
        if n % c == 0:
            return c
    raise ValueError(f"no tile for {n} in {cands}")


def _mm_kernel(x_ref, w_ref, o_ref):
    o_ref[...] = jnp.dot(x_ref[...], w_ref[...].astype(BF16),
                         preferred_element_type=F32).astype(o_ref.dtype)


def matmul(x, w, out_dtype, name):
    M, K = x.shape
    N = w.shape[1]
    tm = _pick(M, (768, 384, 128, 64))
    tn = _pick(N, (512, 256, 128) if K <= 4096 else (256, 128))
    return _pcall(
        _mm_kernel, name=name, grid=(M // tm, N // tn),
        in_specs=[pl.BlockSpec((tm, K), lambda i, j: (i, 0)),
                  pl.BlockSpec((K, tn), lambda i, j: (0, j))],
        inputs=[x, w],
        out_specs=pl.BlockSpec((tm, tn), lambda i, j: (i, j)),
        out_shape=jax.ShapeDtypeStruct((M, N), out_dtype))


def _swiglu_kernel(x_ref, wg_ref, wu_ref, o_ref):
    x = x_ref[...]
    g = jnp.dot(x, wg_ref[...].astype(BF16), preferred_element_type=F32)
    u = jnp.dot(x, wu_ref[...].astype(BF16), preferred_element_type=F32)
    o_ref[...] = (g * jax.nn.sigmoid(g) * u).astype(o_ref.dtype)


def swiglu_up(x, w13, name):
    M, K = x.shape
    F = w13.shape[1] // 2
    tm = _pick(M, (1536, 768, 384, 128, 64))
    tn = _pick(F, (256, 128))
    nb = F // tn
    return _pcall(
        _swiglu_kernel, name=name, grid=(M // tm, nb),
        in_specs=[pl.BlockSpec((tm, K), lambda i, j: (i, 0)),
                  pl.BlockSpec((K, tn), lambda i, j: (0, j)),
                  pl.BlockSpec((K, tn), lambda i, j: (0, j + nb))],
        inputs=[x, w13, w13],
        out_specs=pl.BlockSpec((tm, tn), lambda i, j: (i, j)),
        out_shape=jax.ShapeDtypeStruct((M, F), BF16))


def _res_ln_kernel(x_ref, y_ref, g_ref, b_ref, *o_refs, alpha, yscale):
    z = alpha * x_ref[...] + yscale * y_ref[...]
    mu = jnp.mean(z, axis=-1, keepdims=True)
    zc = z - mu
    var = jnp.mean(zc * zc, axis=-1, keepdims=True)
    o = zc * lax.rsqrt(var + LN_EPS) * g_ref[...] + b_ref[...]
    o_refs[0][...] = o
    if len(o_refs) > 1:
        o_refs[1][...] = o.astype(BF16)


def residual_layer_norm(x, y, g, b, alpha, yscale, name, rows=None):
    M, D = x.shape
    row0, n = rows if rows is not None else (0, M)
    tm = _pick(n, (256, 128, 64))
    assert row0 % tm == 0
    rb0 = row0 // tm
    src = pl.BlockSpec((tm, D), lambda i: (rb0 + i, 0))
    dst = pl.BlockSpec((tm, D), lambda i: (i, 0))
    vec = pl.BlockSpec((1, D), lambda i: (0, 0))
    shapes = [jax.ShapeDtypeStruct((n, D), F32)]
    if rows is None:
        shapes.append(jax.ShapeDtypeStruct((n, D), BF16))
    return _pcall(
        functools.partial(_res_ln_kernel, alpha=alpha, yscale=yscale), name=name, grid=(n // tm,),
        in_specs=[src, src, vec, vec], inputs=[x, y, g.reshape(1, D), b.reshape(1, D)],
        out_specs=[dst] * len(shapes), out_shape=shapes)


def _rope_tables(pos, period, half):
    inv = ROPE_THETA ** (-jnp.arange(half, dtype=F32) / half)
    ang = pos.astype(F32)[:, None] * inv[None, :]
    cos, sin = jnp.cos(ang), jnp.sin(ang)
    T = pos.shape[0]
    reps = LANES // period
    zeros = jnp.zeros((T, period - 2 * half), F32)
    c = jnp.concatenate([cos, cos, jnp.ones((T, period - 2 * half), F32)], axis=1)
    s1 = jnp.concatenate([-sin, jnp.zeros((T, half), F32), zeros], axis=1)
    s2 = jnp.concatenate([jnp.zeros((T, half), F32), sin, zeros], axis=1)
    return [jnp.tile(t, (1, reps)) for t in (c, s1, s2)]


def _rope(x, c, s1, s2, half):
    outs = []
    for g in range(x.shape[-1] // LANES):
        xg = x[:, g * LANES:(g + 1) * LANES]
        outs.append(xg * c + pltpu.roll(xg, LANES - half, 1) * s1 + pltpu.roll(xg, half, 1) * s2)
    return outs[0] if len(outs) == 1 else jnp.concatenate(outs, axis=1)


def _prep_kernel(bq_ref, bk_ref, bv_ref, bqi_ref, cq_ref, ck_ref, cv_ref, sm_ref, tab_ref, bf_ref,
                 q_o, kb_o, v_o, qi_o, sm_o, kiki_o, cq_o, ck_o, cv_o, lf_o,
                 ks_o, vs_o, cks_o, cvs_o, *, transposed):
    t = tab_ref[...]
    q_o[...] = _rope(bq_ref[...], t[0], t[1], t[2], ROPE_DIM // 2).astype(BF16)
    k = _rope(bk_ref[...], t[0], t[1], t[2], ROPE_DIM // 2)
    ks_o[...] = k
    kb_o[...] = k.astype(BF16)
    v = bv_ref[...]
    vs_o[...] = v
    qi_o[...] = _rope(bqi_ref[...], t[3], t[4], t[5], IDX_ROPE_DIM // 2).astype(BF16)
    sm = _rope(sm_ref[...], t[6], t[7], t[8], IDX_ROPE_DIM // 2)
    if transposed:
        v_o[0] = v.T.astype(BF16)
        sm_o[0] = sm.T
    else:
        v_o[...] = v.astype(BF16)
        sm_o[...] = sm
    lane = lax.broadcasted_iota(jnp.int32, sm.shape, 1)
    ki_lo = jnp.where(lane < IDX_DIM, sm, 0.0)
    kiki_o[...] = (ki_lo + pltpu.roll(ki_lo, IDX_DIM, 1)).astype(BF16)
    cq_o[...] = cq_ref[...].astype(BF16)
    ck = ck_ref[...]
    cks_o[...] = ck
    ck_o[...] = ck.astype(BF16)
    cv = cv_ref[...]
    cvs_o[...] = cv
    cv_o[...] = cv.astype(BF16)
    z = -(sm + bf_ref[...])
    lf_o[...] = jnp.where(lane < IDX_DIM, sm, -(jnp.maximum(z, 0.0) + jnp.log1p(jnp.exp(-jnp.abs(z)))))


def prep_group(P, bf_row, grp, W, layer, depth, state_bufs):
    row0, B, T = grp["row0"], grp["B"], grp["T"]
    tr = grp["transposed"]
    tt = _pick(T, (256, 128, 64))
    nt = T // tt
    rb0 = row0 // tt
    R = B * T

    def col(cb, width):
        return pl.BlockSpec((tt, width), lambda b, i, cb=cb: (rb0 + b * nt + i, cb))

    rows = lambda w: pl.BlockSpec((tt, w), lambda b, i: (b * nt + i, 0))
    srows = pl.BlockSpec((tt, W), lambda b, i: (layer * (R // tt) + b * nt + i, 0))
    bf = lambda w: (rows(w), jax.ShapeDtypeStruct((R, w), BF16))
    if tr:
        v_out = (pl.BlockSpec((1, W, tt), lambda b, i: (b, 0, i)), jax.ShapeDtypeStruct((B, W, T), BF16))
        sm_out = (pl.BlockSpec((1, LANES, tt), lambda b, i: (b, 0, i)),
                  jax.ShapeDtypeStruct((B, LANES, T), F32))
    else:
        v_out = bf(W)
        sm_out = (rows(LANES), jax.ShapeDtypeStruct((R, LANES), F32))
    state = (srows, jax.ShapeDtypeStruct((depth * R, W), F32))
    outs = [bf(W), bf(W), v_out, bf(IDX_HEADS * IDX_DIM), sm_out, bf(LANES),
            bf(W), bf(W), bf(W), (rows(LANES), jax.ShapeDtypeStruct((R, LANES), F32)),
            state, state, state, state]
    carries = [] if state_bufs is None else [(10 + k, a) for k, a in enumerate(state_bufs)]
    small_cb = (12 * W) // LANES
    res = _pcall(
        functools.partial(_prep_kernel, transposed=tr), name=f"prep_{grp['name']}", grid=(B, nt),
        in_specs=[col(3, W), col(4, W), col(5, W), col(6, W), col(7, W), col(8, W), col(9, W),
                  col(small_cb, LANES),
                  pl.BlockSpec((9, tt, LANES), lambda b, i: (0, i, 0)),
                  pl.BlockSpec((1, LANES), lambda b, i: (0, 0))],
        inputs=[P] * 8 + [grp["tabs"], bf_row],
        out_specs=[s for s, _ in outs], out_shape=[s for _, s in outs], carries=carries)
    return res[:10], list(res[10:])


def _shift_rows(x, d, fill_rows):
    row = lax.broadcasted_iota(jnp.int32, x.shape, 0)
    y = pltpu.roll(x, d, 0)
    for t in range(d):
        y = jnp.where(row == t, fill_rows[t:t + 1, :], y)
    return y


def _conv_a_kernel(ah_ref, ab_ref, ac_ref, st_ref, w_ref, y_ref, nst_ref):
    z = ac_ref[...] * ah_ref[...]
    st = st_ref[0]
    w = w_ref[...]
    T = z.shape[0]
    z1 = _shift_rows(z, 1, st[1:2, :])
    z2 = _shift_rows(z, 2, st)
    conv = z2 * w[0:1, :] + z1 * w[1:2, :] + z * w[2:3, :]
    y_ref[...] = (ab_ref[...] * conv).astype(y_ref.dtype)
    nst_ref[0] = z[T - 2:T, :]


def conv_a_group(P, state_a, conv_w, grp, W, ybuf, ybuf_shape):
    row0, B, T = grp["row0"], grp["B"], grp["T"]
    assert T >= CONV_A_W - 1 and row0 % T == 0
    tc = _pick(W, (256, 128))
    nc = W // tc
    rb0 = row0 // T

    def col(k):
        return pl.BlockSpec((T, tc), lambda b, c, k=k: (rb0 + b, k * nc + c))

    return _pcall(
        _conv_a_kernel, name=f"conv_a_{grp['name']}", grid=(B, nc),
        in_specs=[col(0), col(1), col(2),
                  pl.BlockSpec((1, CONV_A_W - 1, tc), lambda b, c: (b, 0, c)),
                  pl.BlockSpec((CONV_A_W, tc), lambda b, c: (0, c))],
        inputs=[P, P, P, state_a, conv_w],
        out_specs=[pl.BlockSpec((T, tc), lambda b, c: (rb0 + b, c)),
                   pl.BlockSpec((1, CONV_A_W - 1, tc), lambda b, c: (b, 0, c))],
        out_shape=[ybuf_shape, jax.ShapeDtypeStruct((B, CONV_A_W - 1, W), F32)],
        carries=[] if ybuf is None else [(0, ybuf)])


def _lru_kernel(dx_ref, dg_ref, st_ref, h0_ref, cw_ref, cb_ref, wa_ref, ba_ref, wx_ref, bx_ref,
                lam_ref, y_ref, nst_ref, hl_ref):
    x = dx_ref[...]
    T = x.shape[0]
    st = st_ref[0]
    w = cw_ref[...]
    x1 = _shift_rows(x, 1, st[2:3, :])
    x2 = _shift_rows(x, 2, st[1:3, :])
    x3 = _shift_rows(x, 3, st)
    xc = x3 * w[0:1, :] + x2 * w[1:2, :] + x1 * w[2:3, :] + x * w[3:4, :] + cb_ref[...]
    xb = xc.astype(BF16)
    r = jax.nn.sigmoid(jnp.dot(xb, wa_ref[0].astype(BF16), preferred_element_type=F32) + ba_ref[...])
    i = jax.nn.sigmoid(jnp.dot(xb, wx_ref[0].astype(BF16), preferred_element_type=F32) + bx_ref[...])
    nl = -lam_ref[...]
    sp = jnp.maximum(nl, 0.0) + jnp.log1p(jnp.exp(-jnp.abs(nl)))
    log_a = -LRU_C * r * sp
    a = jnp.exp(log_a)
    u = jnp.sqrt(-jnp.tanh(log_a) * (a * a + 1.0)) * i * xc
    row = lax.broadcasted_iota(jnp.int32, x.shape, 0)
    u = jnp.where(row == 0, u + a * h0_ref[0], u)
    d = 1
    while d < T:
        keep = row >= d
        a_prev = jnp.where(keep, pltpu.roll(a, d, 0), 1.0)
        u_prev = jnp.where(keep, pltpu.roll(u, d, 0), 0.0)
        u = a * u_prev + u
        a = a * a_prev
        d *= 2
    y_ref[...] = (jax.nn.gelu(dg_ref[...]) * u).astype(y_ref.dtype)
    nst_ref[0] = x[T - 3:T, :]
    hl_ref[0] = u[T - 1:T, :]


def lru_group(P, state_d, h0, lw, grp, W, ybuf, ybuf_shape):
    row0, B, T = grp["row0"], grp["B"], grp["T"]
    wa, wx = lw["lru_wa"], lw["lru_wx"]
    nblk, bw = wa.shape[0], wa.shape[1]
    assert T >= CONV_D_W - 1 and row0 % T == 0 and bw % LANES == 0 and nblk * bw == W
    rb0 = row0 // T
    cb = (10 * W) // bw
    yb = (3 * W) // bw

    def col(k):
        return pl.BlockSpec((T, bw), lambda b, c, k=k: (rb0 + b, cb + k * nblk + c))

    vec = pl.BlockSpec((1, bw), lambda b, c: (0, c))
    mat = pl.BlockSpec((1, bw, bw), lambda b, c: (c, 0, 0))
    row = lambda a: a.reshape(1, W)
    return _pcall(
        _lru_kernel, name=f"lru_{grp['name']}", grid=(B, nblk),
        in_specs=[col(0), col(1),
                  pl.BlockSpec((1, CONV_D_W - 1, bw), lambda b, c: (b, 0, c)),
                  pl.BlockSpec((1, 1, bw), lambda b, c: (b, 0, c)),
                  pl.BlockSpec((CONV_D_W, bw), lambda b, c: (0, c)),
                  vec, mat, vec, mat, vec, vec],
        inputs=[P, P, state_d, h0.reshape(B, 1, W), lw["conv_d_w"], row(lw["conv_d_b"]), wa,
                row(lw["lru_ba"]), wx, row(lw["lru_bx"]), row(lw["lru_lambda"])],
        out_specs=[pl.BlockSpec((T, bw), lambda b, c: (rb0 + b, yb + c)),
                   pl.BlockSpec((1, CONV_D_W - 1, bw), lambda b, c: (b, 0, c)),
                   pl.BlockSpec((1, 1, bw), lambda b, c: (b, 0, c))],
        out_shape=[ybuf_shape, jax.ShapeDtypeStruct((B, CONV_D_W - 1, W), F32),
                   jax.ShapeDtypeStruct((B, 1, W), F32)],
        carries=[(0, ybuf)])


def _order_key(x):
    bits = pltpu.bitcast(x + 0.0, jnp.int32)
    return jnp.where(bits < 0, bits ^ jnp.int32(0x7FFFFFFF), bits)


def _count(mask, axis):
    ind = jnp.where(mask, 1.0, 0.0)
    if axis == 1:
        return jnp.sum(ind, axis=1, keepdims=True)
    S, tq = ind.shape
    g = 8 if S % 64 == 0 else 1
    part = ind.reshape(g, S // (8 * g), 8, tq).sum(axis=1)
    return part.sum(axis=0).sum(axis=0, keepdims=True)


def _select_topk(key, adm, tri, n_sel, axis):
    shape = tuple(1 if a == axis else s for a, s in enumerate(key.shape))

    def bit_step(b, t):
        cand = t + lax.shift_left(jnp.int32(1), 31 - b)
        return jnp.where(_count(key >= cand, axis) >= float(n_sel), cand, t)

    thr = lax.fori_loop(0, 32, bit_step, jnp.full(shape, INT_MIN, jnp.int32))
    eq = jnp.where(key == thr, 1.0, 0.0).astype(BF16)
    blk = tri.shape[0]
    total = jnp.zeros(shape, F32)
    ranks = []
    for b in range(key.shape[axis] // blk):
        if axis == 0:
            r = jnp.dot(tri, eq[b * blk:(b + 1) * blk, :], preferred_element_type=F32) + total
            total = r[blk - 1:blk, :]
        else:
            r = jnp.dot(eq[:, b * blk:(b + 1) * blk], tri, preferred_element_type=F32) + total
            total = r[:, blk - 1:blk]
        ranks.append(r)
    rank = jnp.concatenate(ranks, axis=axis)
    need = float(n_sel) - _count(key > thr, axis)
    ninf = -jnp.inf
    tie = jnp.where(key == thr, jnp.where(rank <= need, 0.0, ninf), ninf)
    return jnp.where(adm, jnp.where(key > thr, 0.0, tie), ninf)


IDX_SCALE = (IDX_DIM ** -0.5) * (IDX_HEADS ** -0.5)
EXP2_SCALE = (HEAD_DIM ** -0.5) * math.log2(math.e)


def _tri(n, lower):
    r = lax.broadcasted_iota(jnp.int32, (n, n), 0)
    c = lax.broadcasted_iota(jnp.int32, (n, n), 1)
    return jnp.where(r >= c if lower else r <= c, 1.0, 0.0).astype(BF16)


def _head_pair_rows(x):
    lane = lax.broadcasted_iota(jnp.int32, x.shape, 1)
    zero = jnp.zeros_like(x)
    half = x.shape[1] // 2
    return jnp.concatenate([jnp.where(lane < half, x, zero), jnp.where(lane >= half, x, zero)], axis=0)


def _dsa_rows_kernel(q_ref, qi_ref, sm_ref, k_ref, v_ref, kiki_ref, tri_ref, o_ref, *,
                     chunk0, s_valid, n_sel, n_heads):
    c = pl.program_id(1)
    tq = q_ref.shape[0]
    S = k_ref.shape[0]
    sm = sm_ref[...]
    kiki = kiki_ref[...]
    score = jnp.zeros((tq, S), F32)
    for j in range(IDX_HEADS // 2):
        qq = _head_pair_rows(qi_ref[:, j * LANES:(j + 1) * LANES])
        s2 = lax.dot_general(qq, kiki, NT, preferred_element_type=F32)
        for half in range(2):
            h = 2 * j + half
            w = sm[:, IDX_DIM + h:IDX_DIM + h + 1] * IDX_SCALE
            score = score + jnp.maximum(s2[half * tq:(half + 1) * tq, :], 0.0) * w
    kpos = lax.broadcasted_iota(jnp.int32, (tq, S), 1)
    adm = kpos < jnp.minimum((chunk0 + c + 1) * CHUNK, s_valid)
    key = jnp.where(adm, _order_key(score), jnp.int32(INT_MIN))
    bias = _select_topk(key, adm, tri_ref[...], n_sel, 1)

    for h in range(n_heads):
        sl = slice(h * HEAD_DIM, (h + 1) * HEAD_DIM)
        lg = lax.dot_general(q_ref[:, sl], k_ref[:, sl], NT, preferred_element_type=F32) + bias
        p = jnp.exp2((lg - jnp.max(lg, axis=-1, keepdims=True)) * EXP2_SCALE)
        l = jnp.sum(p, axis=-1, keepdims=True)
        o = jnp.dot(p.astype(BF16), v_ref[:, sl], preferred_element_type=F32)
        o_ref[:, sl] = (o / l).astype(o_ref.dtype)


def _dsa_cols_kernel(q_ref, qi_ref, wt_ref, k_ref, vt_ref, kiki_ref, tri_ref, o_ref, *,
                     tile0, n_sel, n_heads):
    i = pl.program_id(1) + tile0
    tq = q_ref.shape[0]
    S = k_ref.shape[1]
    wt = wt_ref[0]
    kiki = kiki_ref[0]
    score = jnp.zeros((S, tq), F32)
    for j in range(IDX_HEADS // 2):
        qq = _head_pair_rows(qi_ref[:, j * LANES:(j + 1) * LANES])
        s2 = lax.dot_general(kiki, qq, NT, preferred_element_type=F32)
        for half in range(2):
            h = 2 * j + half
            w = wt[IDX_DIM + h:IDX_DIM + h + 1, :] * IDX_SCALE
            score = score + jnp.maximum(s2[:, half * tq:(half + 1) * tq], 0.0) * w
    kpos = lax.broadcasted_iota(jnp.int32, (S, tq), 0)
    qpos = i * tq + lax.broadcasted_iota(jnp.int32, (1, tq), 1)
    adm = kpos < (lax.shift_right_logical(qpos, 6) + 1) * CHUNK
    key = jnp.where(adm, _order_key(score), jnp.int32(INT_MIN))
    bias = _select_topk(key, adm, tri_ref[...], n_sel, 0)

    for hp in range(n_heads // 2):
        sl2 = slice(2 * hp * HEAD_DIM, (2 * hp + 2) * HEAD_DIM)
        lg2 = lax.dot_general(k_ref[0, :, sl2], _head_pair_rows(q_ref[:, sl2]), NT,
                              preferred_element_type=F32)
        for half in range(2):
            sl = slice((2 * hp + half) * HEAD_DIM, (2 * hp + half + 1) * HEAD_DIM)
            lg = lg2[:, half * tq:(half + 1) * tq] + bias
            p = jnp.exp2((lg - jnp.max(lg, axis=0, keepdims=True)) * EXP2_SCALE)
            l = jnp.sum(p, axis=0, keepdims=True)
            ot = jnp.dot(vt_ref[0, sl, :], p.astype(BF16), preferred_element_type=F32)
            o_ref[:, sl] = (ot / l).T.astype(o_ref.dtype)


def _tri_block(n):
    return 256 if n % 256 == 0 else LANES


def dsa_rows_group(q, qi, smf, k_all, v_all, kiki_all, grp, W, ybuf, ybuf_shape):
    B, T, S, S_pad, pos0, row0 = (grp[k] for k in ("B", "T", "S", "S_pad", "pos0", "row0"))
    assert pos0 % CHUNK == 0 and T % CHUNK == 0
    nq = T // CHUNK
    blk = _tri_block(S_pad)
    qrow = lambda w: pl.BlockSpec((CHUNK, w), lambda b, c: (b * nq + c, 0))
    krow = lambda w: pl.BlockSpec((S_pad, w), lambda b, c: (b, 0))
    kern = functools.partial(_dsa_rows_kernel, chunk0=pos0 // CHUNK, s_valid=S,
                             n_sel=min(TOPK_MAX, S // 4), n_heads=W // HEAD_DIM)
    return _pcall(
        kern, name=f"dsa_{grp['name']}", grid=(B, nq),
        in_specs=[qrow(W), qrow(IDX_HEADS * IDX_DIM), qrow(LANES),
                  krow(W), krow(W), krow(LANES),
                  pl.BlockSpec((blk, blk), lambda b, c: (0, 0))],
        inputs=[q, qi, smf, k_all, v_all, kiki_all, _tri(blk, False)],
        out_specs=pl.BlockSpec((CHUNK, W), lambda b, c: (row0 // CHUNK + b * nq + c, 1)),
        out_shape=ybuf_shape, carries=[(0, ybuf)])


def dsa_cols_group(q, qi, smT, k_b, vT, kiki, grp, W, ybuf, ybuf_shape):
    B, T, row0 = grp["B"], grp["T"], grp["row0"]
    tq = LANES
    H = W // HEAD_DIM
    assert grp["pos0"] == 0 and T % tq == 0 and row0 % tq == 0 and H % 2 == 0
    nq = T // tq
    kb = max(tq, T // 4)
    tpc = kb // tq
    blk = _tri_block(kb)
    k3, kiki3 = k_b.reshape(B, T, W), kiki.reshape(B, T, LANES)
    for cls in range(nq // tpc):
        s_eff = (cls + 1) * kb
        t0 = cls * tpc
        qrow = lambda w: pl.BlockSpec((tq, w), lambda b, i: (b * nq + t0 + i, 0))
        keys = lambda w: pl.BlockSpec((1, s_eff, w), lambda b, i: (b, 0, 0))
        kern = functools.partial(_dsa_cols_kernel, tile0=t0, n_sel=min(TOPK_MAX, T // 4), n_heads=H)
        ybuf = _pcall(
            kern, name=f"dsa_{grp['name']}_{cls}", grid=(B, tpc),
            in_specs=[qrow(W), qrow(IDX_HEADS * IDX_DIM),
                      pl.BlockSpec((1, LANES, tq), lambda b, i: (b, 0, t0 + i)),
                      keys(W), pl.BlockSpec((1, W, s_eff), lambda b, i: (b, 0, 0)), keys(LANES),
                      pl.BlockSpec((blk, blk), lambda b, i: (0, 0))],
            inputs=[q, qi, smT, k3, vT, kiki3, _tri(blk, True)],
            out_specs=pl.BlockSpec((tq, W), lambda b, i: (row0 // tq + b * nq + t0 + i, 1)),
            out_shape=ybuf_shape, carries=[(0, ybuf)])
    return ybuf


def _cumsum_kernel(x_ref, o_ref):
    x = x_ref[...]
    n = x.shape[0]
    row = lax.broadcasted_iota(jnp.int32, x.shape, 0)
    d = 1
    while d < n:
        x = x + jnp.where(row >= d, pltpu.roll(x, d, 0), 0.0)
        d *= 2
    o_ref[...] = x


def cumsum_rows(x, B, n, name):
    spec = pl.BlockSpec((n, LANES), lambda b: (b, 0))
    return _pcall(_cumsum_kernel, name=name, grid=(B,), in_specs=[spec], inputs=[x], out_specs=spec,
                  out_shape=jax.ShapeDtypeStruct((B * n, LANES), F32))


def _fox_kernel(q_ref, fq_ref, k_ref, v_ref, fk_ref, o_ref, *, pos0, s_valid, fcol0, n_heads):
    i = pl.program_id(1)
    tq = q_ref.shape[0]
    S = k_ref.shape[0]
    qpos = pos0 + i * tq + lax.broadcasted_iota(jnp.int32, (tq, S), 0)
    kpos = lax.broadcasted_iota(jnp.int32, (tq, S), 1)
    ok = (kpos <= qpos) & (kpos < s_valid)
    fq = fq_ref[...]
    fk = fk_ref[0]
    scale = HEAD_DIM ** -0.5
    for h in range(n_heads):
        sl = slice(h * HEAD_DIM, (h + 1) * HEAD_DIM)
        lg = lax.dot_general(q_ref[:, sl], k_ref[:, sl], NT, preferred_element_type=F32) * scale
        lg = lg + fq[:, fcol0 + h:fcol0 + h + 1] - fk[h:h + 1, :]
        lg = jnp.where(ok, lg, -jnp.inf)
        m = jnp.max(lg, axis=-1, keepdims=True)
        p = jnp.exp(lg - m)
        l = jnp.sum(p, axis=-1, keepdims=True)
        o = jnp.dot(p.astype(BF16), v_ref[:, sl], preferred_element_type=F32)
        o_ref[:, sl] = (o / l).astype(o_ref.dtype)


def fox_group(q, Fq, k_all, v_all, Fk_t, grp, W, fcol0, ybuf, ybuf_shape):
    B, T, S, S_pad, pos0, row0 = (grp[k] for k in ("B", "T", "S", "S_pad", "pos0", "row0"))
    H = W // HEAD_DIM
    tq = _pick(T, (256, 128, 64))
    nq = T // tq
    qrow = lambda w: pl.BlockSpec((tq, w), lambda b, i: (b * nq + i, 0))
    krow = lambda w: pl.BlockSpec((S_pad, w), lambda b, i: (b, 0))
    kern = functools.partial(_fox_kernel, pos0=pos0, s_valid=S, fcol0=fcol0, n_heads=H)
    return _pcall(
        kern, name=f"fox_{grp['name']}", grid=(B, nq),
        in_specs=[qrow(W), qrow(LANES), krow(W), krow(W),
                  pl.BlockSpec((1, H, S_pad), lambda b, i: (b, 0, 0))],
        inputs=[q, Fq, k_all, v_all, Fk_t],
        out_specs=pl.BlockSpec((tq, W), lambda b, i: (row0 // tq + b * nq + i, 2)),
        out_shape=ybuf_shape, carries=[(0, ybuf)])


def _pack_w_in(w_in, W, H_C):
    sizes = (W, W, W, W, W, W, IDX_HEADS * IDX_DIM, IDX_DIM, IDX_HEADS, W, W, W, H_C, W, W)
    offs = [0]
    for s in sizes:
        offs.append(offs[-1] + s)
    seg = lambda k: w_in[:, offs[k]:offs[k + 1]]
    D = w_in.shape[0]
    small = jnp.concatenate([seg(7), seg(8), seg(12),
                             jnp.zeros((D, SMALL_W - IDX_DIM - IDX_HEADS - H_C), w_in.dtype)], axis=1)
    order = [0, 1, 2, 3, 4, 5, 6, 9, 10, 11, 13, 14]
    return jnp.concatenate([seg(k) for k in order] + [small], axis=1)


def _token_mix(xb, lw, groups, pasts, layer, depth, state_bufs):
    W = lw["W"]
    H = W // HEAD_DIM
    M = xb.shape[0]
    fcol0 = IDX_DIM + IDX_HEADS
    P = matmul(xb, lw["w_in"], F32, "in_proj")
    bf_row = jnp.zeros((1, LANES), F32).at[0, fcol0:fcol0 + H].set(lw["fox_b_f"])
    ybuf_shape = jax.ShapeDtypeStruct((M, 4 * W), BF16)
    ybuf = None
    small_states, new_bufs = [], []
    for gi, (grp, past) in enumerate(zip(groups, pasts)):
        B, T, S, S_pad = grp["B"], grp["T"], grp["S"], grp["S_pad"]
        pk_b, pv_b, pki_b, pk_c, pv_c, plf_c, st_a, st_d, h0 = past
        npast = S - T
        (q_b, k_b, v_x, qi_b, sm_x, kiki, cq_b, ck_b, cv_b, lf), bufs = prep_group(
            P, bf_row, grp, W, layer, depth, None if state_bufs is None else state_bufs[gi])
        new_bufs.append(bufs)

        ybuf, nst_a = conv_a_group(P, st_a, lw["conv_a_w"], grp, W, ybuf, ybuf_shape)
        ybuf, nst_d, h_last = lru_group(P, st_d, h0, lw, grp, W, ybuf, ybuf_shape)

        def with_past(new, past_arr, dtype, width):
            new = new.reshape(B, T, width)
            parts = [] if not npast else [past_arr.reshape(B, npast, -1).astype(dtype)]
            parts.append(new)
            if S_pad > S:
                parts.append(jnp.zeros((B, S_pad - S, width), dtype))
            return jnp.concatenate(parts, axis=1).reshape(B * S_pad, width)

        if grp["transposed"]:
            ybuf = dsa_cols_group(q_b, qi_b, sm_x, k_b, v_x, kiki, grp, W, ybuf, ybuf_shape)
            plf = None
        else:
            pkiki = jnp.concatenate([pki_b, pki_b], axis=-1)
            plf = jnp.pad(plf_c.astype(F32), ((0, 0), (0, 0), (fcol0, LANES - fcol0 - H)))
            ybuf = dsa_rows_group(q_b, qi_b, sm_x, with_past(k_b, pk_b, BF16, W),
                                  with_past(v_x, pv_b, BF16, W), with_past(kiki, pkiki, BF16, LANES),
                                  grp, W, ybuf, ybuf_shape)

        F_all = cumsum_rows(with_past(lf, plf, F32, LANES), B, S_pad, f"cumsum_{grp['name']}")
        F3 = F_all.reshape(B, S_pad, LANES)
        Fq = F3[:, npast:npast + T, :].reshape(B * T, LANES)
        Fk_t = jnp.swapaxes(F3[:, :, fcol0:fcol0 + H], 1, 2)
        ybuf = fox_group(cq_b, Fq, with_past(ck_b, pk_c, BF16, W), with_past(cv_b, pv_c, BF16, W),
                         Fk_t, grp, W, fcol0, ybuf, ybuf_shape)

        small_states.append((lf[:, :IDX_DIM].reshape(B, T, IDX_DIM),
                             lf[:, fcol0:fcol0 + H].reshape(B, T, H),
                             nst_a, nst_d, h_last.reshape(B, W)))
    y = matmul(ybuf, lw["w_out"], F32, "out_proj")
    return y, small_states, new_bufs


def kernel(x_prompt, x_sample, cache_dsa_k, cache_dsa_v, cache_dsa_kidx, cache_fox_k, cache_fox_v,
           cache_fox_logf, state_conv_a, state_conv_d, state_lru, ln_g, ln_b, ffn_w13, ffn_w2, w_in,
           fox_b_f, conv_a_w, conv_d_w, conv_d_b, lru_wa, lru_ba, lru_wx, lru_bx, lru_lambda, w_out):
    Bp, Tp, D = x_prompt.shape
    Bs, Ts, _ = x_sample.shape
    depth = ln_g.shape[0]
    past_len = cache_dsa_k.shape[2]
    W = D // 4
    H = W // HEAD_DIM
    assert IDX_HEADS * IDX_DIM == W and W % HEAD_DIM == 0
    alpha = (2.0 * depth) ** 0.25
    dt = x_prompt.dtype

    def group(name, row0, B, T, pos0):
        S = pos0 + T
        S_pad = -(-S // LANES) * LANES
        pos = pos0 + jnp.arange(T, dtype=jnp.int32)
        tabs = jnp.stack(_rope_tables(pos, HEAD_DIM, ROPE_DIM // 2)
                         + _rope_tables(pos, IDX_DIM, IDX_ROPE_DIM // 2)
                         + _rope_tables(pos, LANES, IDX_ROPE_DIM // 2))
        return dict(name=name, row0=row0, B=B, T=T, pos0=pos0, S=S, S_pad=S_pad, tabs=tabs,
                    transposed=(pos0 == 0 and T % LANES == 0))

    groups = [group("prompt", 0, Bp, Tp, 0), group("sample", Bp * Tp, Bs, Ts, past_len)]
    Rp, Rs = Bp * Tp, Bs * Ts

    x = jnp.concatenate([x_prompt.reshape(Rp, D), x_sample.reshape(Rs, D)], axis=0)
    xb = x.astype(BF16)

    small = [[], []]
    state_bufs = None
    for l in range(depth):
        lw = dict(W=W, w_in=_pack_w_in(w_in[l], W, H), fox_b_f=fox_b_f[l], conv_a_w=conv_a_w[l],
                  conv_d_w=conv_d_w[l], conv_d_b=conv_d_b[l], lru_wa=lru_wa[l], lru_ba=lru_ba[l],
                  lru_wx=lru_wx[l], lru_bx=lru_bx[l], lru_lambda=lru_lambda[l], w_out=w_out[l])
        empty = (None,) * 6 + (jnp.zeros((Bp, CONV_A_W - 1, W), dt),
                               jnp.zeros((Bp, CONV_D_W - 1, W), dt), jnp.zeros((Bp, W), dt))
        sample_past = (cache_dsa_k[l], cache_dsa_v[l], cache_dsa_kidx[l], cache_fox_k[l],
                       cache_fox_v[l], cache_fox_logf[l], state_conv_a[l], state_conv_d[l],
                       state_lru[l])

        h = swiglu_up(xb, ffn_w13[l, 0], "ffn_up")
        f = matmul(h, ffn_w2[l, 0], F32, "ffn_down")
        x, xb = residual_layer_norm(x, f, ln_g[l, 0], ln_b[l, 0], alpha, 0.5, "res_ln")

        y, small_states, state_bufs = _token_mix(xb, lw, groups, [empty, sample_past], l, depth,
                                                 state_bufs)
        x, xb = residual_layer_norm(x, y, ln_g[l, 1], ln_b[l, 1], alpha, 1.0, "res_ln")

        h = swiglu_up(xb, ffn_w13[l, 1], "ffn_up")
        f = matmul(h, ffn_w2[l, 1], F32, "ffn_down")
        if l + 1 < depth:
            x, xb = residual_layer_norm(x, f, ln_g[l, 2], ln_b[l, 2], alpha, 0.5, "res_ln")
        else:
            outs = [residual_layer_norm(x, f, ln_g[l, 2], ln_b[l, 2], alpha, 0.5, "res_ln_out",
                                        rows=r)[0] for r in ((0, Rp), (Rp, Rs))]
        for g in range(2):
            small[g].append(small_states[g])

    res = [outs[0].reshape(Bp, Tp, D), outs[1].reshape(Bs, Ts, D)]
    for g, (B, T) in enumerate(((Bp, Tp), (Bs, Ts))):
        ks, vs, cks, cvs = (a.reshape(depth, B, T, H, HEAD_DIM) for a in state_bufs[g])
        kidx, logf, st_a, st_d, h_last = (jnp.stack([s[i] for s in small[g]]) for i in range(5))
        res += [ks, vs, kidx, cks, cvs, logf, st_a, st_d, h_last]
    return tuple(res)
```

```python
import functools
import math

import jax
import jax.numpy as jnp
from jax import lax
from jax.experimental import pallas as pl
from jax.experimental.pallas import tpu as pltpu

F32 = jnp.float32
BF16 = jnp.bfloat16

CHUNK = 64
HEAD_DIM = 128
ROPE_DIM = HEAD_DIM // 4
ROPE_THETA = 500000.0
IDX_HEADS = 16
IDX_DIM = 64
IDX_ROPE_DIM = IDX_DIM // 4
TOPK_MAX = 256
CONV_A_W = 3
CONV_D_W = 4
LRU_C = 8.0
LN_EPS = 1e-5
LANES = 128
SMALL_W = 512
VMEM_LIMIT = 58 * 1024 * 1024
INT_MIN = -2 ** 31
NT = (((1,), (1,)), ((), ()))


def _pcall(body, *, name, grid, in_specs, inputs, out_specs, out_shape, carries=(), scratch=()):
    n_in, n_c = len(inputs), len(carries)

    def kern(*refs):
        body(*refs[:n_in], *refs[n_in + n_c:])

    return pl.pallas_call(
        kern, name=name, grid=grid,
        in_specs=list(in_specs) + [pl.BlockSpec(memory_space=pl.ANY)] * n_c,
        out_specs=out_specs, out_shape=out_shape,
        input_output_aliases={n_in + k: oi for k, (oi, _) in enumerate(carries)},
        scratch_shapes=list(scratch),
        compiler_params=pltpu.CompilerParams(dimension_semantics=("arbitrary",) * len(grid),
                                             vmem_limit_bytes=VMEM_LIMIT),
    )(*inputs, *[a for _, a in carries])


def _pick(n, cands):
    for c in cands:
        if n % c == 0:
            return c
    raise ValueError(f"no tile for {n} in {cands}")


def _mm_kernel(x_ref, w_ref, o_ref, wb_ref):
    @pl.when(pl.program_id(1) == 0)
    def _():
        wb_ref[...] = w_ref[0].astype(BF16)

    o_ref[...] = jnp.dot(x_ref[...], wb_ref[...], preferred_element_type=F32).astype(o_ref.dtype)


def matmul(x, w, li, n_cols, out_dtype, name, tms, tn, single_buffer_w=False):
    M, K = x.shape
    tm = _pick(M, tms)
    assert n_cols % tn == 0 and w.shape[1] == K
    mode = dict(pipeline_mode=pl.Buffered(1)) if single_buffer_w else {}
    return _pcall(
        _mm_kernel, name=name, grid=(n_cols // tn, M // tm),
        in_specs=[pl.BlockSpec((tm, K), lambda j, i: (i, 0)),
                  pl.BlockSpec((1, K, tn), lambda j, i: (li, 0, j), **mode)],
        inputs=[x, w],
        out_specs=pl.BlockSpec((tm, tn), lambda j, i: (i, j)),
        out_shape=jax.ShapeDtypeStruct((M, n_cols), out_dtype),
        scratch=[pltpu.VMEM((K, tn), BF16)])


def _swiglu_kernel(x_ref, wg_ref, wu_ref, o_ref, wgb_ref, wub_ref):
    @pl.when(pl.program_id(1) == 0)
    def _():
        wgb_ref[...] = wg_ref[0].astype(BF16)
        wub_ref[...] = wu_ref[0].astype(BF16)

    x = x_ref[...]
    g = jnp.dot(x, wgb_ref[...], preferred_element_type=F32)
    u = jnp.dot(x, wub_ref[...], preferred_element_type=F32)
    o_ref[...] = (g * jax.nn.sigmoid(g) * u).astype(o_ref.dtype)


def swiglu_up(x, w13, li, name):
    M, K = x.shape
    F = w13.shape[2] // 2
    tm = _pick(M, (1536, 768, 384, 128, 64))
    tn = _pick(F, (256, 128))
    nb = F // tn
    return _pcall(
        _swiglu_kernel, name=name, grid=(nb, M // tm),
        in_specs=[pl.BlockSpec((tm, K), lambda j, i: (i, 0)),
                  pl.BlockSpec((1, K, tn), lambda j, i: (li, 0, j)),
                  pl.BlockSpec((1, K, tn), lambda j, i: (li, 0, j + nb))],
        inputs=[x, w13, w13],
        out_specs=pl.BlockSpec((tm, tn), lambda j, i: (i, j)),
        out_shape=jax.ShapeDtypeStruct((M, F), BF16),
        scratch=[pltpu.VMEM((K, tn), BF16), pltpu.VMEM((K, tn), BF16)])


def _res_ln_kernel(x_ref, y_ref, g_ref, b_ref, *o_refs, alpha, yscale):
    z = alpha * x_ref[...] + yscale * y_ref[...]
    mu = jnp.mean(z, axis=-1, keepdims=True)
    zc = z - mu
    var = jnp.mean(zc * zc, axis=-1, keepdims=True)
    o = zc * lax.rsqrt(var + LN_EPS) * g_ref[...] + b_ref[...]
    o_refs[0][...] = o
    if len(o_refs) > 1:
        o_refs[1][...] = o.astype(BF16)


def residual_layer_norm(x, y, g, b, alpha, yscale, name, rows=None):
    M, D = x.shape
    row0, n = rows if rows is not None else (0, M)
    tm = _pick(n, (256, 128, 64))
    assert row0 % tm == 0
    rb0 = row0 // tm
    src = pl.BlockSpec((tm, D), lambda i: (rb0 + i, 0))
    dst = pl.BlockSpec((tm, D), lambda i: (i, 0))
    vec = pl.BlockSpec((1, D), lambda i: (0, 0))
    shapes = [jax.ShapeDtypeStruct((n, D), F32)]
    if rows is None:
        shapes.append(jax.ShapeDtypeStruct((n, D), BF16))
    return _pcall(
        functools.partial(_res_ln_kernel, alpha=alpha, yscale=yscale), name=name, grid=(n // tm,),
        in_specs=[src, src, vec, vec], inputs=[x, y, g.reshape(1, D), b.reshape(1, D)],
        out_specs=[dst] * len(shapes), out_shape=shapes)


def _rope_tables(pos, period, half):
    inv = ROPE_THETA ** (-jnp.arange(half, dtype=F32) / half)
    ang = pos.astype(F32)[:, None] * inv[None, :]
    cos, sin = jnp.cos(ang), jnp.sin(ang)
    T = pos.shape[0]
    reps = LANES // period
    zeros = jnp.zeros((T, period - 2 * half), F32)
    c = jnp.concatenate([cos, cos, jnp.ones((T, period - 2 * half), F32)], axis=1)
    s1 = jnp.concatenate([-sin, jnp.zeros((T, half), F32), zeros], axis=1)
    s2 = jnp.concatenate([jnp.zeros((T, half), F32), sin, zeros], axis=1)
    return [jnp.tile(t, (1, reps)) for t in (c, s1, s2)]


def _rope(x, c, s1, s2, half):
    outs = []
    for g in range(x.shape[-1] // LANES):
        xg = x[:, g * LANES:(g + 1) * LANES]
        outs.append(xg * c + pltpu.roll(xg, LANES - half, 1) * s1 + pltpu.roll(xg, half, 1) * s2)
    return outs[0] if len(outs) == 1 else jnp.concatenate(outs, axis=1)


def _prep_kernel(bq_ref, bk_ref, bv_ref, bqi_ref, cq_ref, ck_ref, cv_ref, sm_ref, tab_ref, bf_ref,
                 q_o, kb_o, v_o, qi_o, sm_o, kiki_o, cq_o, ck_o, cv_o, lf_o,
                 ks_o, vs_o, cks_o, cvs_o, *, transposed):
    t = tab_ref[...]
    q_o[...] = _rope(bq_ref[...], t[0], t[1], t[2], ROPE_DIM // 2).astype(BF16)
    k = _rope(bk_ref[...], t[0], t[1], t[2], ROPE_DIM // 2)
    ks_o[...] = k
    kb_o[...] = k.astype(BF16)
    v = bv_ref[...]
    vs_o[...] = v
    qi_o[...] = _rope(bqi_ref[...], t[3], t[4], t[5], IDX_ROPE_DIM // 2).astype(BF16)
    sm = _rope(sm_ref[...], t[6], t[7], t[8], IDX_ROPE_DIM // 2)
    if transposed:
        v_o[0] = v.T.astype(BF16)
        sm_o[0] = sm.T
    else:
        v_o[...] = v.astype(BF16)
        sm_o[...] = sm
    lane = lax.broadcasted_iota(jnp.int32, sm.shape, 1)
    ki_lo = jnp.where(lane < IDX_DIM, sm, 0.0)
    kiki_o[...] = (ki_lo + pltpu.roll(ki_lo, IDX_DIM, 1)).astype(BF16)
    cq_o[...] = cq_ref[...].astype(BF16)
    ck = ck_ref[...]
    cks_o[...] = ck
    ck_o[...] = ck.astype(BF16)
    cv = cv_ref[...]
    cvs_o[...] = cv
    cv_o[...] = cv.astype(BF16)
    z = -(sm + bf_ref[...])
    lf_o[...] = jnp.where(lane < IDX_DIM, sm, -(jnp.maximum(z, 0.0) + jnp.log1p(jnp.exp(-jnp.abs(z)))))


def prep_group(P1, P2, bf_row, grp, W, layer, depth, state_bufs):
    row0, B, T = grp["row0"], grp["B"], grp["T"]
    tr = grp["transposed"]
    tt = _pick(T, (256, 128, 64))
    nt = T // tt
    rb0 = row0 // tt
    R = B * T

    def col(cb, width):
        return pl.BlockSpec((tt, width), lambda b, i, cb=cb: (rb0 + b * nt + i, cb))

    rows = lambda w: pl.BlockSpec((tt, w), lambda b, i: (b * nt + i, 0))
    srows = pl.BlockSpec((tt, W), lambda b, i: (layer * (R // tt) + b * nt + i, 0))
    bf = lambda w: (rows(w), jax.ShapeDtypeStruct((R, w), BF16))
    if tr:
        v_out = (pl.BlockSpec((1, W, tt), lambda b, i: (b, 0, i)), jax.ShapeDtypeStruct((B, W, T), BF16))
        sm_out = (pl.BlockSpec((1, LANES, tt), lambda b, i: (b, 0, i)),
                  jax.ShapeDtypeStruct((B, LANES, T), F32))
    else:
        v_out = bf(W)
        sm_out = (rows(LANES), jax.ShapeDtypeStruct((R, LANES), F32))
    state = (srows, jax.ShapeDtypeStruct((depth * R, W), F32))
    outs = [bf(W), bf(W), v_out, bf(IDX_HEADS * IDX_DIM), sm_out, bf(LANES),
            bf(W), bf(W), bf(W), (rows(LANES), jax.ShapeDtypeStruct((R, LANES), F32)),
            state, state, state, state]
    carries = [] if state_bufs is None else [(10 + k, a) for k, a in enumerate(state_bufs)]
    small_cb = (5 * W) // LANES
    res = _pcall(
        functools.partial(_prep_kernel, transposed=tr), name=f"prep_{grp['name']}", grid=(B, nt),
        in_specs=[col(3, W), col(4, W), col(5, W), col(6, W), col(0, W), col(1, W), col(2, W),
                  col(small_cb, LANES),
                  pl.BlockSpec((9, tt, LANES), lambda b, i: (0, i, 0)),
                  pl.BlockSpec((1, LANES), lambda b, i: (0, 0))],
        inputs=[P1] * 4 + [P2] * 4 + [grp["tabs"], bf_row],
        out_specs=[s for s, _ in outs], out_shape=[s for _, s in outs], carries=carries)
    return res[:10], list(res[10:])


def _shift_rows(x, d, fill_rows):
    row = lax.broadcasted_iota(jnp.int32, x.shape, 0)
    y = pltpu.roll(x, d, 0)
    for t in range(d):
        y = jnp.where(row == t, fill_rows[t:t + 1, :], y)
    return y


def _conv_a_kernel(ah_ref, ab_ref, ac_ref, st_ref, w_ref, y_ref, nst_ref):
    z = ac_ref[...] * ah_ref[...]
    st = st_ref[0]
    w = w_ref[...]
    T = z.shape[0]
    z1 = _shift_rows(z, 1, st[1:2, :])
    z2 = _shift_rows(z, 2, st)
    conv = z2 * w[0:1, :] + z1 * w[1:2, :] + z * w[2:3, :]
    y_ref[...] = (ab_ref[...] * conv).astype(y_ref.dtype)
    nst_ref[0] = z[T - 2:T, :]


def conv_a_group(P, state_a, conv_w, grp, W, ybuf, ybuf_shape):
    row0, B, T = grp["row0"], grp["B"], grp["T"]
    assert T >= CONV_A_W - 1 and row0 % T == 0
    tc = _pick(W, (256, 128))
    nc = W // tc
    rb0 = row0 // T

    def col(k):
        return pl.BlockSpec((T, tc), lambda b, c, k=k: (rb0 + b, k * nc + c))

    return _pcall(
        _conv_a_kernel, name=f"conv_a_{grp['name']}", grid=(B, nc),
        in_specs=[col(0), col(1), col(2),
                  pl.BlockSpec((1, CONV_A_W - 1, tc), lambda b, c: (b, 0, c)),
                  pl.BlockSpec((CONV_A_W, tc), lambda b, c: (0, c))],
        inputs=[P, P, P, state_a, conv_w],
        out_specs=[pl.BlockSpec((T, tc), lambda b, c: (rb0 + b, c)),
                   pl.BlockSpec((1, CONV_A_W - 1, tc), lambda b, c: (b, 0, c))],
        out_shape=[ybuf_shape, jax.ShapeDtypeStruct((B, CONV_A_W - 1, W), F32)],
        carries=[] if ybuf is None else [(0, ybuf)])


def _lru_kernel(dx_ref, dg_ref, st_ref, h0_ref, cw_ref, cb_ref, wa_ref, ba_ref, wx_ref, bx_ref,
                lam_ref, y_ref, nst_ref, hl_ref):
    x = dx_ref[...]
    T = x.shape[0]
    st = st_ref[0]
    w = cw_ref[...]
    x1 = _shift_rows(x, 1, st[2:3, :])
    x2 = _shift_rows(x, 2, st[1:3, :])
    x3 = _shift_rows(x, 3, st)
    xc = x3 * w[0:1, :] + x2 * w[1:2, :] + x1 * w[2:3, :] + x * w[3:4, :] + cb_ref[...]
    xb = xc.astype(BF16)
    r = jax.nn.sigmoid(jnp.dot(xb, wa_ref[0].astype(BF16), preferred_element_type=F32) + ba_ref[...])
    i = jax.nn.sigmoid(jnp.dot(xb, wx_ref[0].astype(BF16), preferred_element_type=F32) + bx_ref[...])
    nl = -lam_ref[...]
    sp = jnp.maximum(nl, 0.0) + jnp.log1p(jnp.exp(-jnp.abs(nl)))
    log_a = -LRU_C * r * sp
    a = jnp.exp(log_a)
    u = jnp.sqrt(-jnp.tanh(log_a) * (a * a + 1.0)) * i * xc
    row = lax.broadcasted_iota(jnp.int32, x.shape, 0)
    u = jnp.where(row == 0, u + a * h0_ref[0], u)
    d = 1
    while d < T:
        keep = row >= d
        a_prev = jnp.where(keep, pltpu.roll(a, d, 0), 1.0)
        u_prev = jnp.where(keep, pltpu.roll(u, d, 0), 0.0)
        u = a * u_prev + u
        a = a * a_prev
        d *= 2
    y_ref[...] = (jax.nn.gelu(dg_ref[...]) * u).astype(y_ref.dtype)
    nst_ref[0] = x[T - 3:T, :]
    hl_ref[0] = u[T - 1:T, :]


def lru_group(P, state_d, h0, lw, grp, W, ybuf, ybuf_shape):
    row0, B, T = grp["row0"], grp["B"], grp["T"]
    wa, wx = lw["lru_wa"], lw["lru_wx"]
    nblk, bw = wa.shape[0], wa.shape[1]
    assert T >= CONV_D_W - 1 and row0 % T == 0 and bw % LANES == 0 and nblk * bw == W
    rb0 = row0 // T
    cb = (3 * W) // bw
    yb = (3 * W) // bw

    def col(k):
        return pl.BlockSpec((T, bw), lambda b, c, k=k: (rb0 + b, cb + k * nblk + c))

    vec = pl.BlockSpec((1, bw), lambda b, c: (0, c))
    mat = pl.BlockSpec((1, bw, bw), lambda b, c: (c, 0, 0))
    row = lambda a: a.reshape(1, W)
    return _pcall(
        _lru_kernel, name=f"lru_{grp['name']}", grid=(B, nblk),
        in_specs=[col(0), col(1),
                  pl.BlockSpec((1, CONV_D_W - 1, bw), lambda b, c: (b, 0, c)),
                  pl.BlockSpec((1, 1, bw), lambda b, c: (b, 0, c)),
                  pl.BlockSpec((CONV_D_W, bw), lambda b, c: (0, c)),
                  vec, mat, vec, mat, vec, vec],
        inputs=[P, P, state_d, h0.reshape(B, 1, W), lw["conv_d_w"], row(lw["conv_d_b"]), wa,
                row(lw["lru_ba"]), wx, row(lw["lru_bx"]), row(lw["lru_lambda"])],
        out_specs=[pl.BlockSpec((T, bw), lambda b, c: (rb0 + b, yb + c)),
                   pl.BlockSpec((1, CONV_D_W - 1, bw), lambda b, c: (b, 0, c)),
                   pl.BlockSpec((1, 1, bw), lambda b, c: (b, 0, c))],
        out_shape=[ybuf_shape, jax.ShapeDtypeStruct((B, CONV_D_W - 1, W), F32),
                   jax.ShapeDtypeStruct((B, 1, W), F32)],
        carries=[(0, ybuf)])


def _order_key(x):
    bits = pltpu.bitcast(x + 0.0, jnp.int32)
    return jnp.where(bits < 0, bits ^ jnp.int32(0x7FFFFFFF), bits)


def _count(mask, axis):
    ind = jnp.where(mask, 1.0, 0.0)
    if axis == 1:
        return jnp.sum(ind, axis=1, keepdims=True)
    S, tq = ind.shape
    g = 8 if S % 64 == 0 else 1
    part = ind.reshape(g, S // (8 * g), 8, tq).sum(axis=1)
    return part.sum(axis=0).sum(axis=0, keepdims=True)


def _select_topk(key, adm, tri, n_sel, axis):
    shape = tuple(1 if a == axis else s for a, s in enumerate(key.shape))

    def bit_step(b, t):
        cand = t + lax.shift_left(jnp.int32(1), 31 - b)
        return jnp.where(_count(key >= cand, axis) >= float(n_sel), cand, t)

    thr = lax.fori_loop(0, 32, bit_step, jnp.full(shape, INT_MIN, jnp.int32))
    eq = jnp.where(key == thr, 1.0, 0.0).astype(BF16)
    blk = tri.shape[0]
    total = jnp.zeros(shape, F32)
    ranks = []
    for b in range(key.shape[axis] // blk):
        if axis == 0:
            r = jnp.dot(tri, eq[b * blk:(b + 1) * blk, :], preferred_element_type=F32) + total
            total = r[blk - 1:blk, :]
        else:
            r = jnp.dot(eq[:, b * blk:(b + 1) * blk], tri, preferred_element_type=F32) + total
            total = r[:, blk - 1:blk]
        ranks.append(r)
    rank = jnp.concatenate(ranks, axis=axis)
    need = float(n_sel) - _count(key > thr, axis)
    ninf = -jnp.inf
    tie = jnp.where(key == thr, jnp.where(rank <= need, 0.0, ninf), ninf)
    return jnp.where(adm, jnp.where(key > thr, 0.0, tie), ninf)


IDX_SCALE = (IDX_DIM ** -0.5) * (IDX_HEADS ** -0.5)
EXP2_SCALE = (HEAD_DIM ** -0.5) * math.log2(math.e)


def _tri(n, lower):
    r = lax.broadcasted_iota(jnp.int32, (n, n), 0)
    c = lax.broadcasted_iota(jnp.int32, (n, n), 1)
    return jnp.where(r >= c if lower else r <= c, 1.0, 0.0).astype(BF16)


def _head_pair_rows(x):
    lane = lax.broadcasted_iota(jnp.int32, x.shape, 1)
    zero = jnp.zeros_like(x)
    half = x.shape[1] // 2
    return jnp.concatenate([jnp.where(lane < half, x, zero), jnp.where(lane >= half, x, zero)], axis=0)


def _pad_rows(x, n):
    if x.shape[0] == n:
        return x
    return jnp.concatenate([x, jnp.zeros((n - x.shape[0], x.shape[1]), x.dtype)], axis=0)


def _dsa_rows_kernel(q_ref, qi_ref, sm_ref, kp_ref, vp_ref, kikip_ref, kn_ref, vn_ref, kikin_ref,
                     tri_ref, o_ref, *, chunk0, n_new, s_valid, n_sel, n_heads):
    c = pl.program_id(1)
    tq = q_ref.shape[0]
    n_past = kp_ref.shape[0]
    S = n_past + n_new
    sm = sm_ref[...]
    kn = _pad_rows(kn_ref[...], n_new)
    vn = _pad_rows(vn_ref[...], n_new)
    kikip = kikip_ref[...].astype(BF16)
    kikin = _pad_rows(kikin_ref[...], n_new)
    score = jnp.zeros((tq, S), F32)
    for j in range(IDX_HEADS // 2):
        qq = _head_pair_rows(qi_ref[:, j * LANES:(j + 1) * LANES])
        s2 = jnp.concatenate([lax.dot_general(qq, kikip, NT, preferred_element_type=F32),
                              lax.dot_general(qq, kikin, NT, preferred_element_type=F32)], axis=1)
        for half in range(2):
            h = 2 * j + half
            w = sm[:, IDX_DIM + h:IDX_DIM + h + 1] * IDX_SCALE
            score = score + jnp.maximum(s2[half * tq:(half + 1) * tq, :], 0.0) * w
    kpos = lax.broadcasted_iota(jnp.int32, (tq, S), 1)
    adm = kpos < jnp.minimum((chunk0 + c + 1) * CHUNK, s_valid)
    key = jnp.where(adm, _order_key(score), jnp.int32(INT_MIN))
    bias = _select_topk(key, adm, tri_ref[...], n_sel, 1)

    for h in range(n_heads):
        sl = slice(h * HEAD_DIM, (h + 1) * HEAD_DIM)
        qh = q_ref[:, sl]
        lg = jnp.concatenate(
            [lax.dot_general(qh, kp_ref[:, sl].astype(BF16), NT, preferred_element_type=F32),
             lax.dot_general(qh, kn[:, sl], NT, preferred_element_type=F32)], axis=1) + bias
        p = jnp.exp2((lg - jnp.max(lg, axis=-1, keepdims=True)) * EXP2_SCALE)
        l = jnp.sum(p, axis=-1, keepdims=True)
        pb = p.astype(BF16)
        o = (jnp.dot(pb[:, :n_past], vp_ref[:, sl].astype(BF16), preferred_element_type=F32)
             + jnp.dot(pb[:, n_past:], vn[:, sl], preferred_element_type=F32))
        o_ref[:, sl] = (o / l).astype(o_ref.dtype)


def _dsa_cols_kernel(q_ref, qi_ref, wt_ref, k_ref, vt_ref, kiki_ref, tri_ref, o_ref, *,
                     tile0, n_sel, n_heads):
    i = pl.program_id(1) + tile0
    tq = q_ref.shape[0]
    S = k_ref.shape[1]
    wt = wt_ref[0]
    kiki = kiki_ref[0]
    score = jnp.zeros((S, tq), F32)
    for j in range(IDX_HEADS // 2):
        qq = _head_pair_rows(qi_ref[:, j * LANES:(j + 1) * LANES])
        s2 = lax.dot_general(kiki, qq, NT, preferred_element_type=F32)
        for half in range(2):
            h = 2 * j + half
            w = wt[IDX_DIM + h:IDX_DIM + h + 1, :] * IDX_SCALE
            score = score + jnp.maximum(s2[:, half * tq:(half + 1) * tq], 0.0) * w
    kpos = lax.broadcasted_iota(jnp.int32, (S, tq), 0)
    qpos = i * tq + lax.broadcasted_iota(jnp.int32, (1, tq), 1)
    adm = kpos < (lax.shift_right_logical(qpos, 6) + 1) * CHUNK
    key = jnp.where(adm, _order_key(score), jnp.int32(INT_MIN))
    bias = _select_topk(key, adm, tri_ref[...], n_sel, 0)

    for hp in range(n_heads // 2):
        sl2 = slice(2 * hp * HEAD_DIM, (2 * hp + 2) * HEAD_DIM)
        lg2 = lax.dot_general(k_ref[0, :, sl2], _head_pair_rows(q_ref[:, sl2]), NT,
                              preferred_element_type=F32)
        for half in range(2):
            sl = slice((2 * hp + half) * HEAD_DIM, (2 * hp + half + 1) * HEAD_DIM)
            lg = lg2[:, half * tq:(half + 1) * tq] + bias
            p = jnp.exp2((lg - jnp.max(lg, axis=0, keepdims=True)) * EXP2_SCALE)
            l = jnp.sum(p, axis=0, keepdims=True)
            ot = jnp.dot(vt_ref[0, sl, :], p.astype(BF16), preferred_element_type=F32)
            o_ref[:, sl] = (ot / l).T.astype(o_ref.dtype)


def _tri_block(n):
    return 256 if n % 256 == 0 else LANES


def dsa_rows_group(q, qi, smf, k_new, v_new, kiki_new, k_past, v_past, kiki_past, layer, grp, W,
                   ybuf, ybuf_shape):
    B, T, S, S_pad, pos0, row0 = (grp[k] for k in ("B", "T", "S", "S_pad", "pos0", "row0"))
    assert pos0 % LANES == 0 and T % CHUNK == 0 and pos0 > 0
    nq = T // CHUNK
    blk = _tri_block(S_pad)
    qrow = lambda w: pl.BlockSpec((CHUNK, w), lambda b, c: (b * nq + c, 0))
    new = lambda w: pl.BlockSpec((T, w), lambda b, c: (b, 0))
    kern = functools.partial(_dsa_rows_kernel, chunk0=pos0 // CHUNK, n_new=S_pad - pos0, s_valid=S,
                             n_sel=min(TOPK_MAX, S // 4), n_heads=W // HEAD_DIM)
    return _pcall(
        kern, name=f"dsa_{grp['name']}", grid=(B, nq),
        in_specs=[qrow(W), qrow(IDX_HEADS * IDX_DIM), qrow(LANES),
                  pl.BlockSpec((pos0, W), lambda b, c: (layer * B + b, 0)),
                  pl.BlockSpec((pos0, W), lambda b, c: (layer * B + b, 0)),
                  pl.BlockSpec((pos0, LANES), lambda b, c: (b, 0)),
                  new(W), new(W), new(LANES),
                  pl.BlockSpec((blk, blk), lambda b, c: (0, 0))],
        inputs=[q, qi, smf, k_past, v_past, kiki_past, k_new, v_new, kiki_new, _tri(blk, False)],
        out_specs=pl.BlockSpec((CHUNK, W), lambda b, c: (row0 // CHUNK + b * nq + c, 1)),
        out_shape=ybuf_shape, carries=[(0, ybuf)])


def dsa_cols_group(q, qi, smT, k_b, vT, kiki, grp, W, ybuf, ybuf_shape):
    B, T, row0 = grp["B"], grp["T"], grp["row0"]
    tq = LANES
    H = W // HEAD_DIM
    assert grp["pos0"] == 0 and T % tq == 0 and row0 % tq == 0 and H % 2 == 0
    nq = T // tq
    kb = max(tq, T // 4)
    tpc = kb // tq
    blk = _tri_block(kb)
    k3, kiki3 = k_b.reshape(B, T, W), kiki.reshape(B, T, LANES)
    for cls in range(nq // tpc):
        s_eff = (cls + 1) * kb
        t0 = cls * tpc
        qrow = lambda w: pl.BlockSpec((tq, w), lambda b, i: (b * nq + t0 + i, 0))
        keys = lambda w: pl.BlockSpec((1, s_eff, w), lambda b, i: (b, 0, 0))
        kern = functools.partial(_dsa_cols_kernel, tile0=t0, n_sel=min(TOPK_MAX, T // 4), n_heads=H)
        ybuf = _pcall(
            kern, name=f"dsa_{grp['name']}_{cls}", grid=(B, tpc),
            in_specs=[qrow(W), qrow(IDX_HEADS * IDX_DIM),
                      pl.BlockSpec((1, LANES, tq), lambda b, i: (b, 0, t0 + i)),
                      keys(W), pl.BlockSpec((1, W, s_eff), lambda b, i: (b, 0, 0)), keys(LANES),
                      pl.BlockSpec((blk, blk), lambda b, i: (0, 0))],
            inputs=[q, qi, smT, k3, vT, kiki3, _tri(blk, True)],
            out_specs=pl.BlockSpec((tq, W), lambda b, i: (row0 // tq + b * nq + t0 + i, 1)),
            out_shape=ybuf_shape, carries=[(0, ybuf)])
    return ybuf


def _cumsum_kernel(x_ref, o_ref):
    x = x_ref[...]
    n = x.shape[0]
    row = lax.broadcasted_iota(jnp.int32, x.shape, 0)
    d = 1
    while d < n:
        x = x + jnp.where(row >= d, pltpu.roll(x, d, 0), 0.0)
        d *= 2
    o_ref[...] = x


def cumsum_rows(x, B, n, name):
    spec = pl.BlockSpec((n, LANES), lambda b: (b, 0))
    return _pcall(_cumsum_kernel, name=name, grid=(B,), in_specs=[spec], inputs=[x], out_specs=spec,
                  out_shape=jax.ShapeDtypeStruct((B * n, LANES), F32))


def _fox_kernel(*refs, pos0, n_new, s_valid, fcol0, n_heads):
    if pos0:
        q_ref, fq_ref, fk_ref, kp_ref, vp_ref, kn_ref, vn_ref, o_ref = refs
    else:
        q_ref, fq_ref, fk_ref, kn_ref, vn_ref, o_ref = refs
    i = pl.program_id(1)
    tq = q_ref.shape[0]
    S = pos0 + n_new
    kn = _pad_rows(kn_ref[...], n_new)
    vn = _pad_rows(vn_ref[...], n_new)
    qpos = pos0 + i * tq + lax.broadcasted_iota(jnp.int32, (tq, S), 0)
    kpos = lax.broadcasted_iota(jnp.int32, (tq, S), 1)
    ok = (kpos <= qpos) & (kpos < s_valid)
    fq = fq_ref[...]
    fk = fk_ref[0]
    scale = HEAD_DIM ** -0.5
    for h in range(n_heads):
        sl = slice(h * HEAD_DIM, (h + 1) * HEAD_DIM)
        qh = q_ref[:, sl]
        lg = lax.dot_general(qh, kn[:, sl], NT, preferred_element_type=F32)
        if pos0:
            lg = jnp.concatenate(
                [lax.dot_general(qh, kp_ref[:, sl].astype(BF16), NT, preferred_element_type=F32), lg],
                axis=1)
        lg = lg * scale + fq[:, fcol0 + h:fcol0 + h + 1] - fk[h:h + 1, :]
        lg = jnp.where(ok, lg, -jnp.inf)
        m = jnp.max(lg, axis=-1, keepdims=True)
        p = jnp.exp(lg - m)
        l = jnp.sum(p, axis=-1, keepdims=True)
        pb = p.astype(BF16)
        o = jnp.dot(pb[:, pos0:], vn[:, sl], preferred_element_type=F32)
        if pos0:
            o = o + jnp.dot(pb[:, :pos0], vp_ref[:, sl].astype(BF16), preferred_element_type=F32)
        o_ref[:, sl] = (o / l).astype(o_ref.dtype)


def fox_group(q, Fq, Fk_t, k_new, v_new, k_past, v_past, layer, grp, W, fcol0, ybuf, ybuf_shape):
    B, T, S, S_pad, pos0, row0 = (grp[k] for k in ("B", "T", "S", "S_pad", "pos0", "row0"))
    H = W // HEAD_DIM
    assert pos0 % LANES == 0
    tq = _pick(T, (256, 128, 64))
    nq = T // tq
    qrow = lambda w: pl.BlockSpec((tq, w), lambda b, i: (b * nq + i, 0))
    new = pl.BlockSpec((T, W), lambda b, i: (b, 0))
    past = pl.BlockSpec((pos0, W), lambda b, i: (layer * B + b, 0))
    kern = functools.partial(_fox_kernel, pos0=pos0, n_new=S_pad - pos0, s_valid=S, fcol0=fcol0,
                             n_heads=H)
    return _pcall(
        kern, name=f"fox_{grp['name']}", grid=(B, nq),
        in_specs=[qrow(W), qrow(LANES), pl.BlockSpec((1, H, S_pad), lambda b, i: (b, 0, 0))]
        + ([past, past] if pos0 else []) + [new, new],
        inputs=[q, Fq, Fk_t] + ([k_past, v_past] if pos0 else []) + [k_new, v_new],
        out_specs=pl.BlockSpec((tq, W), lambda b, i: (row0 // tq + b * nq + i, 2)),
        out_shape=ybuf_shape, carries=[(0, ybuf)])


def _pack_w_tail(w_in, W, H_C):
    sizes = (W, W, W, W, W, W, IDX_HEADS * IDX_DIM, IDX_DIM, IDX_HEADS, W, W, W, H_C, W, W)
    offs = [0]
    for sz in sizes:
        offs.append(offs[-1] + sz)
    seg = lambda k: w_in[:, :, offs[k]:offs[k + 1]]
    pad = jnp.zeros(w_in.shape[:2] + (SMALL_W - IDX_DIM - IDX_HEADS - H_C,), w_in.dtype)
    return jnp.concatenate([seg(k) for k in (9, 10, 11, 13, 14, 7, 8, 12)] + [pad], axis=2)


def _token_mix(xb, lw, groups, pasts, layer, depth, state_bufs):
    W = lw["W"]
    H = W // HEAD_DIM
    M = xb.shape[0]
    fcol0 = IDX_DIM + IDX_HEADS
    tms = (768, 384, 128, 64)
    P1 = matmul(xb, lw["w_in"], layer, 7 * W, F32, "in_proj_a", tms, 512)
    P2 = matmul(xb, lw["w_tail"], layer, 5 * W + SMALL_W, F32, "in_proj_b", tms, 512)
    bf_row = jnp.zeros((1, LANES), F32).at[0, fcol0:fcol0 + H].set(lw["fox_b_f"])
    ybuf_shape = jax.ShapeDtypeStruct((M, 4 * W), BF16)
    ybuf = None
    small_states, new_bufs = [], []
    for gi, (grp, past) in enumerate(zip(groups, pasts)):
        B, T, S, S_pad = grp["B"], grp["T"], grp["S"], grp["S_pad"]
        pk_b, pv_b, pki_b, pk_c, pv_c, plf_c, st_a, st_d, h0 = past
        npast = S - T
        (q_b, k_b, v_x, qi_b, sm_x, kiki, cq_b, ck_b, cv_b, lf), bufs = prep_group(
            P1, P2, bf_row, grp, W, layer, depth, None if state_bufs is None else state_bufs[gi])
        new_bufs.append(bufs)

        ybuf, nst_a = conv_a_group(P1, st_a, lw["conv_a_w"], grp, W, ybuf, ybuf_shape)
        ybuf, nst_d, h_last = lru_group(P2, st_d, h0, lw, grp, W, ybuf, ybuf_shape)

        lf3 = lf.reshape(B, T, LANES)
        if grp["transposed"]:
            ybuf = dsa_cols_group(q_b, qi_b, sm_x, k_b, v_x, kiki, grp, W, ybuf, ybuf_shape)
            lf_all = lf3
        else:
            pki = pki_b[layer]
            kiki_past = jnp.concatenate([pki, pki], axis=-1).reshape(B * npast, LANES)
            ybuf = dsa_rows_group(q_b, qi_b, sm_x, k_b, v_x, kiki, pk_b, pv_b, kiki_past, layer, grp,
                                  W, ybuf, ybuf_shape)
            plf = jnp.pad(plf_c[layer].astype(F32), ((0, 0), (0, 0), (fcol0, LANES - fcol0 - H)))
            lf_all = jnp.concatenate([plf, lf3, jnp.zeros((B, S_pad - S, LANES), F32)], axis=1)

        F_all = cumsum_rows(lf_all.reshape(B * S_pad, LANES), B, S_pad, f"cumsum_{grp['name']}")
        F3 = F_all.reshape(B, S_pad, LANES)
        Fq = F3[:, npast:npast + T, :].reshape(B * T, LANES)
        Fk_t = jnp.swapaxes(F3[:, :, fcol0:fcol0 + H], 1, 2)
        ybuf = fox_group(cq_b, Fq, Fk_t, ck_b, cv_b, pk_c, pv_c, layer, grp, W, fcol0, ybuf, ybuf_shape)

        small_states.append((lf3[:, :, :IDX_DIM], lf3[:, :, fcol0:fcol0 + H],
                             nst_a, nst_d, h_last.reshape(B, W)))
    y = matmul(ybuf, lw["w_out"], layer, 4 * W, F32, "out_proj", tms, 512)
    return y, small_states, new_bufs


def kernel(x_prompt, x_sample, cache_dsa_k, cache_dsa_v, cache_dsa_kidx, cache_fox_k, cache_fox_v,
           cache_fox_logf, state_conv_a, state_conv_d, state_lru, ln_g, ln_b, ffn_w13, ffn_w2, w_in,
           fox_b_f, conv_a_w, conv_d_w, conv_d_b, lru_wa, lru_ba, lru_wx, lru_bx, lru_lambda, w_out):
    Bp, Tp, D = x_prompt.shape
    Bs, Ts, _ = x_sample.shape
    depth = ln_g.shape[0]
    past_len = cache_dsa_k.shape[2]
    W = D // 4
    H = W // HEAD_DIM
    assert IDX_HEADS * IDX_DIM == W and W % HEAD_DIM == 0
    alpha = (2.0 * depth) ** 0.25
    dt = x_prompt.dtype

    def group(name, row0, B, T, pos0):
        S = pos0 + T
        S_pad = -(-S // LANES) * LANES
        pos = pos0 + jnp.arange(T, dtype=jnp.int32)
        tabs = jnp.stack(_rope_tables(pos, HEAD_DIM, ROPE_DIM // 2)
                         + _rope_tables(pos, IDX_DIM, IDX_ROPE_DIM // 2)
                         + _rope_tables(pos, LANES, IDX_ROPE_DIM // 2))
        return dict(name=name, row0=row0, B=B, T=T, pos0=pos0, S=S, S_pad=S_pad, tabs=tabs,
                    transposed=(pos0 == 0 and T % LANES == 0))

    groups = [group("prompt", 0, Bp, Tp, 0), group("sample", Bp * Tp, Bs, Ts, past_len)]
    Rp, Rs = Bp * Tp, Bs * Ts

    x = jnp.concatenate([x_prompt.reshape(Rp, D), x_sample.reshape(Rs, D)], axis=0)
    xb = x.astype(BF16)

    w13 = ffn_w13.reshape((2 * depth,) + ffn_w13.shape[2:])
    w2 = ffn_w2.reshape((2 * depth,) + ffn_w2.shape[2:])
    w_tail = _pack_w_tail(w_in, W, H)
    rows2d = lambda c: c.reshape(-1, W)
    cache_rows = (rows2d(cache_dsa_k), rows2d(cache_dsa_v), cache_dsa_kidx, rows2d(cache_fox_k),
                  rows2d(cache_fox_v), cache_fox_logf)
    down_tms = (384, 128, 64)

    small = [[], []]
    state_bufs = None
    for l in range(depth):
        lw = dict(W=W, w_in=w_in, w_tail=w_tail, fox_b_f=fox_b_f[l], conv_a_w=conv_a_w[l],
                  conv_d_w=conv_d_w[l], conv_d_b=conv_d_b[l], lru_wa=lru_wa[l], lru_ba=lru_ba[l],
                  lru_wx=lru_wx[l], lru_bx=lru_bx[l], lru_lambda=lru_lambda[l], w_out=w_out)
        empty = (None,) * 6 + (jnp.zeros((Bp, CONV_A_W - 1, W), dt),
                               jnp.zeros((Bp, CONV_D_W - 1, W), dt), jnp.zeros((Bp, W), dt))
        sample_past = cache_rows + (state_conv_a[l], state_conv_d[l], state_lru[l])

        h = swiglu_up(xb, w13, 2 * l, "ffn_up")
        f = matmul(h, w2, 2 * l, D, F32, "ffn_down", down_tms, 512, single_buffer_w=True)
        x, xb = residual_layer_norm(x, f, ln_g[l, 0], ln_b[l, 0], alpha, 0.5, "res_ln")

        y, small_states, state_bufs = _token_mix(xb, lw, groups, [empty, sample_past], l, depth,
                                                 state_bufs)
        x, xb = residual_layer_norm(x, y, ln_g[l, 1], ln_b[l, 1], alpha, 1.0, "res_ln")

        h = swiglu_up(xb, w13, 2 * l + 1, "ffn_up")
        f = matmul(h, w2, 2 * l + 1, D, F32, "ffn_down", down_tms, 512, single_buffer_w=True)
        if l + 1 < depth:
            x, xb = residual_layer_norm(x, f, ln_g[l, 2], ln_b[l, 2], alpha, 0.5, "res_ln")
        else:
            outs = [residual_layer_norm(x, f, ln_g[l, 2], ln_b[l, 2], alpha, 0.5, "res_ln_out",
                                        rows=r)[0] for r in ((0, Rp), (Rp, Rs))]
        for g in range(2):
            small[g].append(small_states[g])

    res = [outs[0].reshape(Bp, Tp, D), outs[1].reshape(Bs, Ts, D)]
    for g, (B, T) in enumerate(((Bp, Tp), (Bs, Ts))):
        ks, vs, cks, cvs = (a.reshape(depth, B, T, H, HEAD_DIM) for a in state_bufs[g])
        kidx, logf, st_a, st_d, h_last = (jnp.stack([s[i] for s in small[g]]) for i in range(5))
        res += [ks, vs, kidx, cks, cvs, logf, st_a, st_d, h_last]
    return tuple(res)
```

```python
import functools
import math

import jax
import jax.numpy as jnp
from jax import lax
from jax.experimental import pallas as pl
from jax.experimental.pallas import tpu as pltpu

F32 = jnp.float32
BF16 = jnp.bfloat16

CHUNK = 64
HEAD_DIM = 128
ROPE_DIM = HEAD_DIM // 4
ROPE_THETA = 500000.0
IDX_HEADS = 16
IDX_DIM = 64
IDX_ROPE_DIM = IDX_DIM // 4
TOPK_MAX = 256
CONV_A_W = 3
CONV_D_W = 4
LRU_C = 8.0
LN_EPS = 1e-5
LANES = 128
SMALL_W = 512
VMEM_LIMIT = 58 * 1024 * 1024
INT_MIN = -2 ** 31
NT = (((1,), (1,)), ((), ()))


def _pcall(body, *, name, grid, in_specs, inputs, out_specs, out_shape, carries=(), scratch=()):
    n_in, n_c = len(inputs), len(carries)

    def kern(*refs):
        body(*refs[:n_in], *refs[n_in + n_c:])

    return pl.pallas_call(
        kern, name=name, grid=grid,
        in_specs=list(in_specs) + [pl.BlockSpec(memory_space=pl.ANY)] * n_c,
        out_specs=out_specs, out_shape=out_shape,
        input_output_aliases={n_in + k: oi for k, (oi, _) in enumerate(carries)},
        scratch_shapes=list(scratch),
        compiler_params=pltpu.CompilerParams(dimension_semantics=("arbitrary",) * len(grid),
                                             vmem_limit_bytes=VMEM_LIMIT),
    )(*inputs, *[a for _, a in carries])


def _pick(n, cands):
    for c in cands:
        if n % c == 0:
            return c
    raise ValueError(f"no tile for {n} in {cands}")


def _mm_kernel(x_ref, w_ref, o_ref, wb_ref, *, w_transposed):
    @pl.when(pl.program_id(1) == 0)
    def _():
        w = w_ref[0]
        wb_ref[...] = (w.T if w_transposed else w).astype(BF16)

    o_ref[...] = jnp.dot(x_ref[...], wb_ref[...], preferred_element_type=F32).astype(o_ref.dtype)


def matmul(x, w, li, n_cols, out_dtype, name, tms, tn, single_buffer_w=False, w_transposed=False):
    M, K = x.shape
    tm = _pick(M, tms)
    assert n_cols % tn == 0 and w.shape[2 if w_transposed else 1] == K
    mode = dict(pipeline_mode=pl.Buffered(1)) if single_buffer_w else {}
    if w_transposed:
        w_spec = pl.BlockSpec((1, tn, K), lambda j, i: (li, j, 0), **mode)
    else:
        w_spec = pl.BlockSpec((1, K, tn), lambda j, i: (li, 0, j), **mode)
    return _pcall(
        functools.partial(_mm_kernel, w_transposed=w_transposed), name=name,
        grid=(n_cols // tn, M // tm),
        in_specs=[pl.BlockSpec((tm, K), lambda j, i: (i, 0)), w_spec],
        inputs=[x, w],
        out_specs=pl.BlockSpec((tm, tn), lambda j, i: (i, j)),
        out_shape=jax.ShapeDtypeStruct((M, n_cols), out_dtype),
        scratch=[pltpu.VMEM((K, tn), BF16)])


def _swiglu_kernel(x_ref, wg_ref, wu_ref, o_ref, wgb_ref, wub_ref):
    @pl.when(pl.program_id(1) == 0)
    def _():
        wgb_ref[...] = wg_ref[0].astype(BF16)
        wub_ref[...] = wu_ref[0].astype(BF16)

    x = x_ref[...]
    g = jnp.dot(x, wgb_ref[...], preferred_element_type=F32)
    u = jnp.dot(x, wub_ref[...], preferred_element_type=F32)
    o_ref[...] = (g * jax.nn.sigmoid(g) * u).astype(o_ref.dtype)


def swiglu_up(x, w13, li, name):
    M, K = x.shape
    F = w13.shape[2] // 2
    tm = _pick(M, (1536, 768, 384, 128, 64))
    tn = _pick(F, (256, 128))
    nb = F // tn
    return _pcall(
        _swiglu_kernel, name=name, grid=(nb, M // tm),
        in_specs=[pl.BlockSpec((tm, K), lambda j, i: (i, 0)),
                  pl.BlockSpec((1, K, tn), lambda j, i: (li, 0, j)),
                  pl.BlockSpec((1, K, tn), lambda j, i: (li, 0, j + nb))],
        inputs=[x, w13, w13],
        out_specs=pl.BlockSpec((tm, tn), lambda j, i: (i, j)),
        out_shape=jax.ShapeDtypeStruct((M, F), BF16),
        scratch=[pltpu.VMEM((K, tn), BF16), pltpu.VMEM((K, tn), BF16)])


def _split_rows_specs(tm, D, n_first):
    return [pl.BlockSpec((tm, D), lambda i: (jnp.minimum(i, n_first - 1), 0)),
            pl.BlockSpec((tm, D), lambda i: (jnp.maximum(i - n_first, 0), 0))]


def _cast_split_kernel(a_ref, b_ref, o_ref, *, n_first):
    o_ref[...] = jnp.where(pl.program_id(0) < n_first, a_ref[...], b_ref[...]).astype(o_ref.dtype)


def cast_rows(a, b, dtype, name):
    D = a.shape[1]
    tm = _pick(b.shape[0], (256, 128, 64))
    assert a.shape[0] % tm == 0
    n = (a.shape[0] + b.shape[0]) // tm
    return _pcall(
        functools.partial(_cast_split_kernel, n_first=a.shape[0] // tm), name=name, grid=(n,),
        in_specs=_split_rows_specs(tm, D, a.shape[0] // tm), inputs=[a, b],
        out_specs=pl.BlockSpec((tm, D), lambda i: (i, 0)),
        out_shape=jax.ShapeDtypeStruct((n * tm, D), dtype))


def _res_ln_kernel(*refs, alpha, yscale, n_first):
    if n_first is None:
        x_ref, y_ref, g_ref, b_ref, *o_refs = refs
        x = x_ref[...]
    else:
        xa_ref, xb_ref, y_ref, g_ref, b_ref, *o_refs = refs
        x = jnp.where(pl.program_id(0) < n_first, xa_ref[...], xb_ref[...])
    z = alpha * x + yscale * y_ref[...].astype(F32)
    mu = jnp.mean(z, axis=-1, keepdims=True)
    zc = z - mu
    var = jnp.mean(zc * zc, axis=-1, keepdims=True)
    o = zc * lax.rsqrt(var + LN_EPS) * g_ref[...] + b_ref[...]
    o_refs[0][...] = o
    if len(o_refs) > 1:
        o_refs[1][...] = o.astype(BF16)


def residual_layer_norm(x, y, g, b, alpha, yscale, name, rows=None):
    M, D = y.shape
    row0, n = rows if rows is not None else (0, M)
    tm = _pick(n if not isinstance(x, tuple) else x[1].shape[0], (256, 128, 64))
    assert row0 % tm == 0
    rb0 = row0 // tm
    src = pl.BlockSpec((tm, D), lambda i: (rb0 + i, 0))
    dst = pl.BlockSpec((tm, D), lambda i: (i, 0))
    vec = pl.BlockSpec((1, D), lambda i: (0, 0))
    shapes = [jax.ShapeDtypeStruct((n, D), F32)]
    if rows is None:
        shapes.append(jax.ShapeDtypeStruct((n, D), BF16))
    if isinstance(x, tuple):
        assert rows is None and x[0].shape[0] % tm == 0
        n_first = x[0].shape[0] // tm
        x_specs, xs = _split_rows_specs(tm, D, n_first), list(x)
    else:
        n_first, x_specs, xs = None, [src], [x]
    return _pcall(
        functools.partial(_res_ln_kernel, alpha=alpha, yscale=yscale, n_first=n_first), name=name,
        grid=(n // tm,), in_specs=x_specs + [src, vec, vec],
        inputs=xs + [y, g.reshape(1, D), b.reshape(1, D)],
        out_specs=[dst] * len(shapes), out_shape=shapes)


def _rope_tables(pos, period, half):
    inv = ROPE_THETA ** (-jnp.arange(half, dtype=F32) / half)
    ang = pos.astype(F32)[:, None] * inv[None, :]
    cos, sin = jnp.cos(ang), jnp.sin(ang)
    T = pos.shape[0]
    reps = LANES // period
    zeros = jnp.zeros((T, period - 2 * half), F32)
    c = jnp.concatenate([cos, cos, jnp.ones((T, period - 2 * half), F32)], axis=1)
    s1 = jnp.concatenate([-sin, jnp.zeros((T, half), F32), zeros], axis=1)
    s2 = jnp.concatenate([jnp.zeros((T, half), F32), sin, zeros], axis=1)
    return [jnp.tile(t, (1, reps)) for t in (c, s1, s2)]


def _rope(x, c, s1, s2, half):
    outs = []
    for g in range(x.shape[-1] // LANES):
        xg = x[:, g * LANES:(g + 1) * LANES]
        outs.append(xg * c + pltpu.roll(xg, LANES - half, 1) * s1 + pltpu.roll(xg, half, 1) * s2)
    return outs[0] if len(outs) == 1 else jnp.concatenate(outs, axis=1)


def _prep_kernel(bq_ref, bk_ref, bv_ref, bqi_ref, cq_ref, ck_ref, cv_ref, sm_ref, tab_ref, bf_ref,
                 q_o, kb_o, v_o, qi_o, sm_o, kiki_o, cq_o, ck_o, cv_o, lf_o,
                 ks_o, vs_o, cks_o, cvs_o, *, transposed):
    t = tab_ref[...]
    q_o[...] = _rope(bq_ref[...], t[0], t[1], t[2], ROPE_DIM // 2).astype(BF16)
    k = _rope(bk_ref[...], t[0], t[1], t[2], ROPE_DIM // 2)
    ks_o[...] = k
    kb_o[...] = k.astype(BF16)
    v = bv_ref[...]
    vs_o[...] = v
    qi_o[...] = _rope(bqi_ref[...], t[3], t[4], t[5], IDX_ROPE_DIM // 2).astype(BF16)
    sm = _rope(sm_ref[...], t[6], t[7], t[8], IDX_ROPE_DIM // 2)
    if transposed:
        v_o[0] = v.T.astype(BF16)
        sm_o[0] = sm.T
    else:
        v_o[...] = v.astype(BF16)
        sm_o[...] = sm
    lane = lax.broadcasted_iota(jnp.int32, sm.shape, 1)
    ki_lo = jnp.where(lane < IDX_DIM, sm, 0.0)
    kiki_o[...] = (ki_lo + pltpu.roll(ki_lo, IDX_DIM, 1)).astype(BF16)
    cq_o[...] = cq_ref[...].astype(BF16)
    ck = ck_ref[...]
    cks_o[...] = ck
    ck_o[...] = ck.astype(BF16)
    cv = cv_ref[...]
    cvs_o[...] = cv
    cv_o[...] = cv.astype(BF16)
    z = -(sm + bf_ref[...])
    lf_o[...] = jnp.where(lane < IDX_DIM, sm, -(jnp.maximum(z, 0.0) + jnp.log1p(jnp.exp(-jnp.abs(z)))))


def prep_group(P1, P2, bf_row, grp, W, layer, depth, state_bufs):
    row0, B, T = grp["row0"], grp["B"], grp["T"]
    tr = grp["transposed"]
    tt = _pick(T, (256, 128, 64))
    nt = T // tt
    rb0 = row0 // tt
    R = B * T

    def col(cb, width):
        return pl.BlockSpec((tt, width), lambda b, i, cb=cb: (rb0 + b * nt + i, cb))

    rows = lambda w: pl.BlockSpec((tt, w), lambda b, i: (b * nt + i, 0))
    srows = pl.BlockSpec((tt, W), lambda b, i: (layer * (R // tt) + b * nt + i, 0))
    bf = lambda w: (rows(w), jax.ShapeDtypeStruct((R, w), BF16))
    if tr:
        v_out = (pl.BlockSpec((1, W, tt), lambda b, i: (b, 0, i)), jax.ShapeDtypeStruct((B, W, T), BF16))
        sm_out = (pl.BlockSpec((1, LANES, tt), lambda b, i: (b, 0, i)),
                  jax.ShapeDtypeStruct((B, LANES, T), F32))
    else:
        v_out = bf(W)
        sm_out = (rows(LANES), jax.ShapeDtypeStruct((R, LANES), F32))
    state = (srows, jax.ShapeDtypeStruct((depth * R, W), F32))
    outs = [bf(W), bf(W), v_out, bf(IDX_HEADS * IDX_DIM), sm_out, bf(LANES),
            bf(W), bf(W), bf(W), (rows(LANES), jax.ShapeDtypeStruct((R, LANES), F32)),
            state, state, state, state]
    carries = [] if state_bufs is None else [(10 + k, a) for k, a in enumerate(state_bufs)]
    small_cb = (5 * W) // LANES
    res = _pcall(
        functools.partial(_prep_kernel, transposed=tr), name=f"prep_{grp['name']}", grid=(B, nt),
        in_specs=[col(3, W), col(4, W), col(5, W), col(6, W), col(0, W), col(1, W), col(2, W),
                  col(small_cb, LANES),
                  pl.BlockSpec((9, tt, LANES), lambda b, i: (0, i, 0)),
                  pl.BlockSpec((1, LANES), lambda b, i: (0, 0))],
        inputs=[P1] * 4 + [P2] * 4 + [grp["tabs"], bf_row],
        out_specs=[s for s, _ in outs], out_shape=[s for _, s in outs], carries=carries)
    return res[:10], list(res[10:])


def _shift_rows(x, d, fill_rows):
    row = lax.broadcasted_iota(jnp.int32, x.shape, 0)
    y = pltpu.roll(x, d, 0)
    for t in range(d):
        y = jnp.where(row == t, fill_rows[t:t + 1, :], y)
    return y


def _conv_a_kernel(ah_ref, ab_ref, ac_ref, st_ref, w_ref, y_ref, nst_ref):
    z = ac_ref[...] * ah_ref[...]
    st = st_ref[0]
    w = w_ref[...]
    T = z.shape[0]
    z1 = _shift_rows(z, 1, st[1:2, :])
    z2 = _shift_rows(z, 2, st)
    conv = z2 * w[0:1, :] + z1 * w[1:2, :] + z * w[2:3, :]
    y_ref[...] = (ab_ref[...] * conv).astype(y_ref.dtype)
    nst_ref[0] = z[T - 2:T, :]


def conv_a_group(P, state_a, conv_w, grp, W, ybuf, ybuf_shape):
    row0, B, T = grp["row0"], grp["B"], grp["T"]
    assert T >= CONV_A_W - 1 and row0 % T == 0
    tc = _pick(W, (256, 128))
    nc = W // tc
    rb0 = row0 // T

    def col(k):
        return pl.BlockSpec((T, tc), lambda b, c, k=k: (rb0 + b, k * nc + c))

    return _pcall(
        _conv_a_kernel, name=f"conv_a_{grp['name']}", grid=(B, nc),
        in_specs=[col(0), col(1), col(2),
                  pl.BlockSpec((1, CONV_A_W - 1, tc), lambda b, c: (b, 0, c)),
                  pl.BlockSpec((CONV_A_W, tc), lambda b, c: (0, c))],
        inputs=[P, P, P, state_a, conv_w],
        out_specs=[pl.BlockSpec((T, tc), lambda b, c: (rb0 + b, c)),
                   pl.BlockSpec((1, CONV_A_W - 1, tc), lambda b, c: (b, 0, c))],
        out_shape=[ybuf_shape, jax.ShapeDtypeStruct((B, CONV_A_W - 1, W), F32)],
        carries=[] if ybuf is None else [(0, ybuf)])


def _lru_kernel(dx_ref, dg_ref, st_ref, h0_ref, cw_ref, cb_ref, wa_ref, ba_ref, wx_ref, bx_ref,
                lam_ref, y_ref, nst_ref, hl_ref):
    x = dx_ref[...]
    T = x.shape[0]
    st = st_ref[0]
    w = cw_ref[...]
    x1 = _shift_rows(x, 1, st[2:3, :])
    x2 = _shift_rows(x, 2, st[1:3, :])
    x3 = _shift_rows(x, 3, st)
    xc = x3 * w[0:1, :] + x2 * w[1:2, :] + x1 * w[2:3, :] + x * w[3:4, :] + cb_ref[...]
    xb = xc.astype(BF16)
    r = jax.nn.sigmoid(jnp.dot(xb, wa_ref[0].astype(BF16), preferred_element_type=F32) + ba_ref[...])
    i = jax.nn.sigmoid(jnp.dot(xb, wx_ref[0].astype(BF16), preferred_element_type=F32) + bx_ref[...])
    nl = -lam_ref[...]
    sp = jnp.maximum(nl, 0.0) + jnp.log1p(jnp.exp(-jnp.abs(nl)))
    log_a = -LRU_C * r * sp
    a = jnp.exp(log_a)
    u = jnp.sqrt(-jnp.tanh(log_a) * (a * a + 1.0)) * i * xc
    row = lax.broadcasted_iota(jnp.int32, x.shape, 0)
    u = jnp.where(row == 0, u + a * h0_ref[0], u)
    d = 1
    while d < T:
        keep = row >= d
        a_prev = jnp.where(keep, pltpu.roll(a, d, 0), 1.0)
        u_prev = jnp.where(keep, pltpu.roll(u, d, 0), 0.0)
        u = a * u_prev + u
        a = a * a_prev
        d *= 2
    y_ref[...] = (jax.nn.gelu(dg_ref[...]) * u).astype(y_ref.dtype)
    nst_ref[0] = x[T - 3:T, :]
    hl_ref[0] = u[T - 1:T, :]


def lru_group(P, state_d, h0, lw, grp, W, ybuf, ybuf_shape):
    row0, B, T = grp["row0"], grp["B"], grp["T"]
    wa, wx = lw["lru_wa"], lw["lru_wx"]
    nblk, bw = wa.shape[0], wa.shape[1]
    assert T >= CONV_D_W - 1 and row0 % T == 0 and bw % LANES == 0 and nblk * bw == W
    rb0 = row0 // T
    cb = (3 * W) // bw
    yb = (3 * W) // bw

    def col(k):
        return pl.BlockSpec((T, bw), lambda b, c, k=k: (rb0 + b, cb + k * nblk + c))

    vec = pl.BlockSpec((1, bw), lambda b, c: (0, c))
    mat = pl.BlockSpec((1, bw, bw), lambda b, c: (c, 0, 0))
    row = lambda a: a.reshape(1, W)
    return _pcall(
        _lru_kernel, name=f"lru_{grp['name']}", grid=(B, nblk),
        in_specs=[col(0), col(1),
                  pl.BlockSpec((1, CONV_D_W - 1, bw), lambda b, c: (b, 0, c)),
                  pl.BlockSpec((1, 1, bw), lambda b, c: (b, 0, c)),
                  pl.BlockSpec((CONV_D_W, bw), lambda b, c: (0, c)),
                  vec, mat, vec, mat, vec, vec],
        inputs=[P, P, state_d, h0.reshape(B, 1, W), lw["conv_d_w"], row(lw["conv_d_b"]), wa,
                row(lw["lru_ba"]), wx, row(lw["lru_bx"]), row(lw["lru_lambda"])],
        out_specs=[pl.BlockSpec((T, bw), lambda b, c: (rb0 + b, yb + c)),
                   pl.BlockSpec((1, CONV_D_W - 1, bw), lambda b, c: (b, 0, c)),
                   pl.BlockSpec((1, 1, bw), lambda b, c: (b, 0, c))],
        out_shape=[ybuf_shape, jax.ShapeDtypeStruct((B, CONV_D_W - 1, W), F32),
                   jax.ShapeDtypeStruct((B, 1, W), F32)],
        carries=[(0, ybuf)])


def _order_key(x):
    bits = pltpu.bitcast(x + 0.0, jnp.int32)
    return jnp.where(bits < 0, bits ^ jnp.int32(0x7FFFFFFF), bits)


def _count(mask, axis):
    ind = jnp.where(mask, 1.0, 0.0)
    if axis == 1:
        return jnp.sum(ind, axis=1, keepdims=True)
    S, tq = ind.shape
    g = 8 if S % 64 == 0 else 1
    part = ind.reshape(g, S // (8 * g), 8, tq).sum(axis=1)
    return part.sum(axis=0).sum(axis=0, keepdims=True)


def _select_topk(key, adm, tri, n_sel, axis):
    shape = tuple(1 if a == axis else s for a, s in enumerate(key.shape))

    def bit_step(b, t):
        cand = t + lax.shift_left(jnp.int32(1), 31 - b)
        return jnp.where(_count(key >= cand, axis) >= float(n_sel), cand, t)

    thr = lax.fori_loop(0, 32, bit_step, jnp.full(shape, INT_MIN, jnp.int32))
    eq = jnp.where(key == thr, 1.0, 0.0).astype(BF16)
    blk = tri.shape[0]
    total = jnp.zeros(shape, F32)
    ranks = []
    for b in range(key.shape[axis] // blk):
        if axis == 0:
            r = jnp.dot(tri, eq[b * blk:(b + 1) * blk, :], preferred_element_type=F32) + total
            total = r[blk - 1:blk, :]
        else:
            r = jnp.dot(eq[:, b * blk:(b + 1) * blk], tri, preferred_element_type=F32) + total
            total = r[:, blk - 1:blk]
        ranks.append(r)
    rank = jnp.concatenate(ranks, axis=axis)
    need = float(n_sel) - _count(key > thr, axis)
    ninf = -jnp.inf
    tie = jnp.where(key == thr, jnp.where(rank <= need, 0.0, ninf), ninf)
    return jnp.where(adm, jnp.where(key > thr, 0.0, tie), ninf)


IDX_SCALE = (IDX_DIM ** -0.5) * (IDX_HEADS ** -0.5)
EXP2_SCALE = (HEAD_DIM ** -0.5) * math.log2(math.e)


def _tri(n, lower):
    r = lax.broadcasted_iota(jnp.int32, (n, n), 0)
    c = lax.broadcasted_iota(jnp.int32, (n, n), 1)
    return jnp.where(r >= c if lower else r <= c, 1.0, 0.0).astype(BF16)


def _head_pair_rows(x):
    lane = lax.broadcasted_iota(jnp.int32, x.shape, 1)
    zero = jnp.zeros_like(x)
    half = x.shape[1] // 2
    return jnp.concatenate([jnp.where(lane < half, x, zero), jnp.where(lane >= half, x, zero)], axis=0)


def _pad_rows(x, n):
    if x.shape[0] == n:
        return x
    return jnp.concatenate([x, jnp.zeros((n - x.shape[0], x.shape[1]), x.dtype)], axis=0)


def _dsa_rows_kernel(q_ref, qi_ref, sm_ref, kp_ref, vp_ref, kikip_ref, kn_ref, vn_ref, kikin_ref,
                     tri_ref, o_ref, *, chunk0, n_new, s_valid, n_sel, n_heads):
    c = pl.program_id(1)
    tq = q_ref.shape[0]
    n_past = kp_ref.shape[0]
    S = n_past + n_new
    sm = sm_ref[...]
    kn = _pad_rows(kn_ref[...], n_new)
    vn = _pad_rows(vn_ref[...], n_new)
    kikip = kikip_ref[...].astype(BF16)
    kikin = _pad_rows(kikin_ref[...], n_new)
    score = jnp.zeros((tq, S), F32)
    for j in range(IDX_HEADS // 2):
        qq = _head_pair_rows(qi_ref[:, j * LANES:(j + 1) * LANES])
        s2 = jnp.concatenate([lax.dot_general(qq, kikip, NT, preferred_element_type=F32),
                              lax.dot_general(qq, kikin, NT, preferred_element_type=F32)], axis=1)
        for half in range(2):
            h = 2 * j + half
            w = sm[:, IDX_DIM + h:IDX_DIM + h + 1] * IDX_SCALE
            score = score + jnp.maximum(s2[half * tq:(half + 1) * tq, :], 0.0) * w
    kpos = lax.broadcasted_iota(jnp.int32, (tq, S), 1)
    adm = kpos < jnp.minimum((chunk0 + c + 1) * CHUNK, s_valid)
    key = jnp.where(adm, _order_key(score), jnp.int32(INT_MIN))
    bias = _select_topk(key, adm, tri_ref[...], n_sel, 1)

    for h in range(n_heads):
        sl = slice(h * HEAD_DIM, (h + 1) * HEAD_DIM)
        qh = q_ref[:, sl]
        lg = jnp.concatenate(
            [lax.dot_general(qh, kp_ref[:, sl].astype(BF16), NT, preferred_element_type=F32),
             lax.dot_general(qh, kn[:, sl], NT, preferred_element_type=F32)], axis=1) + bias
        p = jnp.exp2((lg - jnp.max(lg, axis=-1, keepdims=True)) * EXP2_SCALE)
        l = jnp.sum(p, axis=-1, keepdims=True)
        pb = p.astype(BF16)
        o = (jnp.dot(pb[:, :n_past], vp_ref[:, sl].astype(BF16), preferred_element_type=F32)
             + jnp.dot(pb[:, n_past:], vn[:, sl], preferred_element_type=F32))
        o_ref[:, sl] = (o / l).astype(o_ref.dtype)


def _dsa_cols_kernel(q_ref, qi_ref, wt_ref, k_ref, vt_ref, kiki_ref, tri_ref, o_ref, *,
                     tile0, n_sel, n_heads):
    i = pl.program_id(1) + tile0
    tq = q_ref.shape[0]
    S = k_ref.shape[1]
    wt = wt_ref[0]
    kiki = kiki_ref[0]
    score = jnp.zeros((S, tq), F32)
    for j in range(IDX_HEADS // 2):
        qq = _head_pair_rows(qi_ref[:, j * LANES:(j + 1) * LANES])
        s2 = lax.dot_general(kiki, qq, NT, preferred_element_type=F32)
        for half in range(2):
            h = 2 * j + half
            w = wt[IDX_DIM + h:IDX_DIM + h + 1, :] * IDX_SCALE
            score = score + jnp.maximum(s2[:, half * tq:(half + 1) * tq], 0.0) * w
    kpos = lax.broadcasted_iota(jnp.int32, (S, tq), 0)
    qpos = i * tq + lax.broadcasted_iota(jnp.int32, (1, tq), 1)
    adm = kpos < (lax.shift_right_logical(qpos, 6) + 1) * CHUNK
    key = jnp.where(adm, _order_key(score), jnp.int32(INT_MIN))
    bias = _select_topk(key, adm, tri_ref[...], n_sel, 0)

    for hp in range(n_heads // 2):
        sl2 = slice(2 * hp * HEAD_DIM, (2 * hp + 2) * HEAD_DIM)
        lg2 = lax.dot_general(k_ref[0, :, sl2], _head_pair_rows(q_ref[:, sl2]), NT,
                              preferred_element_type=F32)
        for half in range(2):
            sl = slice((2 * hp + half) * HEAD_DIM, (2 * hp + half + 1) * HEAD_DIM)
            lg = lg2[:, half * tq:(half + 1) * tq] + bias
            p = jnp.exp2((lg - jnp.max(lg, axis=0, keepdims=True)) * EXP2_SCALE)
            l = jnp.sum(p, axis=0, keepdims=True)
            ot = jnp.dot(vt_ref[0, sl, :], p.astype(BF16), preferred_element_type=F32)
            o_ref[:, sl] = (ot / l).T.astype(o_ref.dtype)


def _tri_block(n):
    return 256 if n % 256 == 0 else LANES


def dsa_rows_group(q, qi, smf, k_new, v_new, kiki_new, k_past, v_past, kiki_past, layer, grp, W,
                   ybuf, ybuf_shape):
    B, T, S, S_pad, pos0, row0 = (grp[k] for k in ("B", "T", "S", "S_pad", "pos0", "row0"))
    assert pos0 % LANES == 0 and T % CHUNK == 0 and pos0 > 0
    nq = T // CHUNK
    blk = _tri_block(S_pad)
    qrow = lambda w: pl.BlockSpec((CHUNK, w), lambda b, c: (b * nq + c, 0))
    new = lambda w: pl.BlockSpec((T, w), lambda b, c: (b, 0))
    kern = functools.partial(_dsa_rows_kernel, chunk0=pos0 // CHUNK, n_new=S_pad - pos0, s_valid=S,
                             n_sel=min(TOPK_MAX, S // 4), n_heads=W // HEAD_DIM)
    return _pcall(
        kern, name=f"dsa_{grp['name']}", grid=(B, nq),
        in_specs=[qrow(W), qrow(IDX_HEADS * IDX_DIM), qrow(LANES),
                  pl.BlockSpec((pos0, W), lambda b, c: (layer * B + b, 0)),
                  pl.BlockSpec((pos0, W), lambda b, c: (layer * B + b, 0)),
                  pl.BlockSpec((pos0, LANES), lambda b, c: (b, 0)),
                  new(W), new(W), new(LANES),
                  pl.BlockSpec((blk, blk), lambda b, c: (0, 0))],
        inputs=[q, qi, smf, k_past, v_past, kiki_past, k_new, v_new, kiki_new, _tri(blk, False)],
        out_specs=pl.BlockSpec((CHUNK, W), lambda b, c: (row0 // CHUNK + b * nq + c, 1)),
        out_shape=ybuf_shape, carries=[(0, ybuf)])


def dsa_cols_group(q, qi, smT, k_b, vT, kiki, grp, W, ybuf, ybuf_shape):
    B, T, row0 = grp["B"], grp["T"], grp["row0"]
    tq = LANES
    H = W // HEAD_DIM
    assert grp["pos0"] == 0 and T % tq == 0 and row0 % tq == 0 and H % 2 == 0
    nq = T // tq
    kb = max(tq, T // 4)
    tpc = kb // tq
    blk = _tri_block(kb)
    k3, kiki3 = k_b.reshape(B, T, W), kiki.reshape(B, T, LANES)
    for cls in range(nq // tpc):
        s_eff = (cls + 1) * kb
        t0 = cls * tpc
        qrow = lambda w: pl.BlockSpec((tq, w), lambda b, i: (b * nq + t0 + i, 0))
        keys = lambda w: pl.BlockSpec((1, s_eff, w), lambda b, i: (b, 0, 0))
        kern = functools.partial(_dsa_cols_kernel, tile0=t0, n_sel=min(TOPK_MAX, T // 4), n_heads=H)
        ybuf = _pcall(
            kern, name=f"dsa_{grp['name']}_{cls}", grid=(B, tpc),
            in_specs=[qrow(W), qrow(IDX_HEADS * IDX_DIM),
                      pl.BlockSpec((1, LANES, tq), lambda b, i: (b, 0, t0 + i)),
                      keys(W), pl.BlockSpec((1, W, s_eff), lambda b, i: (b, 0, 0)), keys(LANES),
                      pl.BlockSpec((blk, blk), lambda b, i: (0, 0))],
            inputs=[q, qi, smT, k3, vT, kiki3, _tri(blk, True)],
            out_specs=pl.BlockSpec((tq, W), lambda b, i: (row0 // tq + b * nq + t0 + i, 1)),
            out_shape=ybuf_shape, carries=[(0, ybuf)])
    return ybuf


def _cumsum_kernel(x_ref, o_ref):
    x = x_ref[...]
    n = x.shape[0]
    row = lax.broadcasted_iota(jnp.int32, x.shape, 0)
    d = 1
    while d < n:
        x = x + jnp.where(row >= d, pltpu.roll(x, d, 0), 0.0)
        d *= 2
    o_ref[...] = x


def cumsum_rows(x, B, n, name):
    spec = pl.BlockSpec((n, LANES), lambda b: (b, 0))
    return _pcall(_cumsum_kernel, name=name, grid=(B,), in_specs=[spec], inputs=[x], out_specs=spec,
                  out_shape=jax.ShapeDtypeStruct((B * n, LANES), F32))


def _fox_kernel(*refs, pos0, tile0, n_new, s_valid, fcol0, n_heads):
    if pos0:
        q_ref, fq_ref, fk_ref, kp_ref, vp_ref, kn_ref, vn_ref, o_ref = refs
    else:
        q_ref, fq_ref, fk_ref, kn_ref, vn_ref, o_ref = refs
    i = pl.program_id(1) + tile0
    tq = q_ref.shape[0]
    S = pos0 + n_new
    kn = _pad_rows(kn_ref[0], n_new)
    vn = _pad_rows(vn_ref[0], n_new)
    qpos = pos0 + i * tq + lax.broadcasted_iota(jnp.int32, (tq, S), 0)
    kpos = lax.broadcasted_iota(jnp.int32, (tq, S), 1)
    ok = (kpos <= qpos) & (kpos < s_valid)
    fq = fq_ref[...]
    fk = fk_ref[0]
    scale = HEAD_DIM ** -0.5
    for h in range(n_heads):
        sl = slice(h * HEAD_DIM, (h + 1) * HEAD_DIM)
        qh = q_ref[:, sl]
        lg = lax.dot_general(qh, kn[:, sl], NT, preferred_element_type=F32)
        if pos0:
            lg = jnp.concatenate(
                [lax.dot_general(qh, kp_ref[:, sl].astype(BF16), NT, preferred_element_type=F32), lg],
                axis=1)
        lg = lg * scale + fq[:, fcol0 + h:fcol0 + h + 1] - fk[h:h + 1, :]
        lg = jnp.where(ok, lg, -jnp.inf)
        m = jnp.max(lg, axis=-1, keepdims=True)
        p = jnp.exp(lg - m)
        l = jnp.sum(p, axis=-1, keepdims=True)
        pb = p.astype(BF16)
        o = jnp.dot(pb[:, pos0:], vn[:, sl], preferred_element_type=F32)
        if pos0:
            o = o + jnp.dot(pb[:, :pos0], vp_ref[:, sl].astype(BF16), preferred_element_type=F32)
        o_ref[:, sl] = (o / l).astype(o_ref.dtype)


def fox_group(q, Fq, Fk_t, k_new, v_new, k_past, v_past, layer, grp, W, fcol0, ybuf, ybuf_shape):
    B, T, S, S_pad, pos0, row0 = (grp[k] for k in ("B", "T", "S", "S_pad", "pos0", "row0"))
    H = W // HEAD_DIM
    assert pos0 % LANES == 0
    tq = _pick(T, (256, 128, 64))
    nq = T // tq
    tpc = max(1, nq // 4) if pos0 == 0 else nq
    k3, v3 = k_new.reshape(B, T, W), v_new.reshape(B, T, W)
    past = pl.BlockSpec((pos0, W), lambda b, i: (layer * B + b, 0))
    for cls in range(nq // tpc):
        t0 = cls * tpc
        n_keys = min(T, (t0 + tpc) * tq)
        n_new = -(-n_keys // LANES) * LANES
        qrow = lambda w: pl.BlockSpec((tq, w), lambda b, i: (b * nq + t0 + i, 0))
        new = pl.BlockSpec((1, n_keys, W), lambda b, i: (b, 0, 0))
        kern = functools.partial(_fox_kernel, pos0=pos0, tile0=t0, n_new=n_new, s_valid=S,
                                 fcol0=fcol0, n_heads=H)
        ybuf = _pcall(
            kern, name=f"fox_{grp['name']}_{cls}", grid=(B, tpc),
            in_specs=[qrow(W), qrow(LANES), pl.BlockSpec((1, H, pos0 + n_new), lambda b, i: (b, 0, 0))]
            + ([past, past] if pos0 else []) + [new, new],
            inputs=[q, Fq, Fk_t] + ([k_past, v_past] if pos0 else []) + [k3, v3],
            out_specs=pl.BlockSpec((tq, W), lambda b, i: (row0 // tq + b * nq + t0 + i, 2)),
            out_shape=ybuf_shape, carries=[(0, ybuf)])
    return ybuf


def _pack_w_tail(w_in_t, W, H_C):
    sizes = (W, W, W, W, W, W, IDX_HEADS * IDX_DIM, IDX_DIM, IDX_HEADS, W, W, W, H_C, W, W)
    offs = [0]
    for sz in sizes:
        offs.append(offs[-1] + sz)
    seg = lambda k: w_in_t[:, offs[k]:offs[k + 1], :]
    L, _, D = w_in_t.shape
    pad = jnp.zeros((L, SMALL_W - IDX_DIM - IDX_HEADS - H_C, D), w_in_t.dtype)
    return jnp.concatenate([seg(k) for k in (9, 10, 11, 13, 14, 7, 8, 12)] + [pad], axis=1)


def _token_mix(xb, lw, groups, pasts, layer, depth, state_bufs):
    W = lw["W"]
    H = W // HEAD_DIM
    M = xb.shape[0]
    fcol0 = IDX_DIM + IDX_HEADS
    tms = (768, 384, 128, 64)
    P1 = matmul(xb, lw["w_in_t"], layer, 7 * W, F32, "in_proj_a", tms, 512, w_transposed=True)
    P2 = matmul(xb, lw["w_tail_t"], layer, 5 * W + SMALL_W, F32, "in_proj_b", tms, 512,
                w_transposed=True)
    bf_row = jnp.zeros((1, LANES), F32).at[0, fcol0:fcol0 + H].set(lw["fox_b_f"])
    ybuf_shape = jax.ShapeDtypeStruct((M, 4 * W), BF16)
    ybuf = None
    small_states, new_bufs = [], []
    for gi, (grp, past) in enumerate(zip(groups, pasts)):
        B, T, S, S_pad = grp["B"], grp["T"], grp["S"], grp["S_pad"]
        pk_b, pv_b, pki_b, pk_c, pv_c, plf_c, st_a, st_d, h0 = past
        npast = S - T
        (q_b, k_b, v_x, qi_b, sm_x, kiki, cq_b, ck_b, cv_b, lf), bufs = prep_group(
            P1, P2, bf_row, grp, W, layer, depth, None if state_bufs is None else state_bufs[gi])
        new_bufs.append(bufs)

        ybuf, nst_a = conv_a_group(P1, st_a, lw["conv_a_w"], grp, W, ybuf, ybuf_shape)
        ybuf, nst_d, h_last = lru_group(P2, st_d, h0, lw, grp, W, ybuf, ybuf_shape)

        lf3 = lf.reshape(B, T, LANES)
        if grp["transposed"]:
            ybuf = dsa_cols_group(q_b, qi_b, sm_x, k_b, v_x, kiki, grp, W, ybuf, ybuf_shape)
            lf_all = lf3
        else:
            pki = pki_b[layer]
            kiki_past = jnp.concatenate([pki, pki], axis=-1).reshape(B * npast, LANES)
            ybuf = dsa_rows_group(q_b, qi_b, sm_x, k_b, v_x, kiki, pk_b, pv_b, kiki_past, layer, grp,
                                  W, ybuf, ybuf_shape)
            plf = jnp.pad(plf_c[layer].astype(F32), ((0, 0), (0, 0), (fcol0, LANES - fcol0 - H)))
            lf_all = jnp.concatenate([plf, lf3, jnp.zeros((B, S_pad - S, LANES), F32)], axis=1)

        F_all = cumsum_rows(lf_all.reshape(B * S_pad, LANES), B, S_pad, f"cumsum_{grp['name']}")
        F3 = F_all.reshape(B, S_pad, LANES)
        Fq = F3[:, npast:npast + T, :].reshape(B * T, LANES)
        Fk_t = jnp.swapaxes(F3[:, :, fcol0:fcol0 + H], 1, 2)
        ybuf = fox_group(cq_b, Fq, Fk_t, ck_b, cv_b, pk_c, pv_c, layer, grp, W, fcol0, ybuf, ybuf_shape)

        small_states.append((lf3[:, :, :IDX_DIM], lf3[:, :, fcol0:fcol0 + H],
                             nst_a, nst_d, h_last.reshape(B, W)))
    y = matmul(ybuf, lw["w_out"], layer, 4 * W, BF16, "out_proj", tms, 512)
    return y, small_states, new_bufs


def kernel(x_prompt, x_sample, cache_dsa_k, cache_dsa_v, cache_dsa_kidx, cache_fox_k, cache_fox_v,
           cache_fox_logf, state_conv_a, state_conv_d, state_lru, ln_g, ln_b, ffn_w13, ffn_w2, w_in,
           fox_b_f, conv_a_w, conv_d_w, conv_d_b, lru_wa, lru_ba, lru_wx, lru_bx, lru_lambda, w_out):
    Bp, Tp, D = x_prompt.shape
    Bs, Ts, _ = x_sample.shape
    depth = ln_g.shape[0]
    past_len = cache_dsa_k.shape[2]
    W = D // 4
    H = W // HEAD_DIM
    assert IDX_HEADS * IDX_DIM == W and W % HEAD_DIM == 0
    alpha = (2.0 * depth) ** 0.25
    dt = x_prompt.dtype

    def group(name, row0, B, T, pos0):
        S = pos0 + T
        S_pad = -(-S // LANES) * LANES
        pos = pos0 + jnp.arange(T, dtype=jnp.int32)
        tabs = jnp.stack(_rope_tables(pos, HEAD_DIM, ROPE_DIM // 2)
                         + _rope_tables(pos, IDX_DIM, IDX_ROPE_DIM // 2)
                         + _rope_tables(pos, LANES, IDX_ROPE_DIM // 2))
        return dict(name=name, row0=row0, B=B, T=T, pos0=pos0, S=S, S_pad=S_pad, tabs=tabs,
                    transposed=(pos0 == 0 and T % LANES == 0))

    groups = [group("prompt", 0, Bp, Tp, 0), group("sample", Bp * Tp, Bs, Ts, past_len)]
    Rp, Rs = Bp * Tp, Bs * Ts

    x = (x_prompt.reshape(Rp, D), x_sample.reshape(Rs, D))
    xb = cast_rows(x[0], x[1], BF16, "cast_x")

    w13 = ffn_w13.reshape((2 * depth,) + ffn_w13.shape[2:])
    w2 = ffn_w2.reshape((2 * depth,) + ffn_w2.shape[2:])
    w_in_t = jnp.swapaxes(w_in, 1, 2)
    w_tail_t = _pack_w_tail(w_in_t, W, H)
    rows2d = lambda c: c.reshape(-1, W)
    cache_rows = (rows2d(cache_dsa_k), rows2d(cache_dsa_v), cache_dsa_kidx, rows2d(cache_fox_k),
                  rows2d(cache_fox_v), cache_fox_logf)
    down_tms = (384, 128, 64)

    small = [[], []]
    state_bufs = None
    for l in range(depth):
        lw = dict(W=W, w_in_t=w_in_t, w_tail_t=w_tail_t, fox_b_f=fox_b_f[l], conv_a_w=conv_a_w[l],
                  conv_d_w=conv_d_w[l], conv_d_b=conv_d_b[l], lru_wa=lru_wa[l], lru_ba=lru_ba[l],
                  lru_wx=lru_wx[l], lru_bx=lru_bx[l], lru_lambda=lru_lambda[l], w_out=w_out)
        empty = (None,) * 6 + (jnp.zeros((Bp, CONV_A_W - 1, W), dt),
                               jnp.zeros((Bp, CONV_D_W - 1, W), dt), jnp.zeros((Bp, W), dt))
        sample_past = cache_rows + (state_conv_a[l], state_conv_d[l], state_lru[l])

        h = swiglu_up(xb, w13, 2 * l, "ffn_up")
        f = matmul(h, w2, 2 * l, D, BF16, "ffn_down", down_tms, 512, single_buffer_w=True)
        x, xb = residual_layer_norm(x, f, ln_g[l, 0], ln_b[l, 0], alpha, 0.5, "res_ln")

        y, small_states, state_bufs = _token_mix(xb, lw, groups, [empty, sample_past], l, depth,
                                                 state_bufs)
        x, xb = residual_layer_norm(x, y, ln_g[l, 1], ln_b[l, 1], alpha, 1.0, "res_ln")

        h = swiglu_up(xb, w13, 2 * l + 1, "ffn_up")
        f = matmul(h, w2, 2 * l + 1, D, BF16, "ffn_down", down_tms, 512, single_buffer_w=True)
        if l + 1 < depth:
            x, xb = residual_layer_norm(x, f, ln_g[l, 2], ln_b[l, 2], alpha, 0.5, "res_ln")
        else:
            outs = [residual_layer_norm(x, f, ln_g[l, 2], ln_b[l, 2], alpha, 0.5, "res_ln_out",
                                        rows=r)[0] for r in ((0, Rp), (Rp, Rs))]
        for g in range(2):
            small[g].append(small_states[g])

    res = [outs[0].reshape(Bp, Tp, D), outs[1].reshape(Bs, Ts, D)]
    for g, (B, T) in enumerate(((Bp, Tp), (Bs, Ts))):
        ks, vs, cks, cvs = (a.reshape(depth, B, T, H, HEAD_DIM) for a in state_bufs[g])
        kidx, logf, st_a, st_d, h_last = (jnp.stack([s[i] for s in small[g]]) for i in range(5))
        res += [ks, vs, kidx, cks, cvs, logf, st_a, st_d, h_last]
    return tuple(res)
```

```python
import functools
import math

import jax
import jax.numpy as jnp
from jax import lax
from jax.experimental import pallas as pl
from jax.experimental.pallas import tpu as pltpu

F32 = jnp.float32
BF16 = jnp.bfloat16

CHUNK = 64
HEAD_DIM = 128
ROPE_DIM = HEAD_DIM // 4
ROPE_THETA = 500000.0
IDX_HEADS = 16
IDX_DIM = 64
IDX_ROPE_DIM = IDX_DIM // 4
TOPK_MAX = 256
CONV_A_W = 3
CONV_D_W = 4
LRU_C = 8.0
LN_EPS = 1e-5
LANES = 128
SMALL_W = 512
VMEM_LIMIT = 58 * 1024 * 1024
INT_MIN = -2 ** 31
NT = (((1,), (1,)), ((), ()))


def _pcall(body, *, name, grid, in_specs, inputs, out_specs, out_shape, carries=(), scratch=()):
    n_in, n_c = len(inputs), len(carries)

    def kern(*refs):
        body(*refs[:n_in], *refs[n_in + n_c:])

    return pl.pallas_call(
        kern, name=name, grid=grid,
        in_specs=list(in_specs) + [pl.BlockSpec(memory_space=pl.ANY)] * n_c,
        out_specs=out_specs, out_shape=out_shape,
        input_output_aliases={n_in + k: oi for k, (oi, _) in enumerate(carries)},
        scratch_shapes=list(scratch),
        compiler_params=pltpu.CompilerParams(dimension_semantics=("arbitrary",) * len(grid),
                                             vmem_limit_bytes=VMEM_LIMIT),
    )(*inputs, *[a for _, a in carries])


def _pick(n, cands):
    for c in cands:
        if n % c == 0:
            return c
    raise ValueError(f"no tile for {n} in {cands}")


def _mm_kernel(x_ref, w_ref, o_ref, wb_ref, *, w_transposed):
    @pl.when(pl.program_id(1) == 0)
    def _():
        w = w_ref[0]
        wb_ref[...] = (w.T if w_transposed else w).astype(BF16)

    o_ref[...] = jnp.dot(x_ref[...], wb_ref[...], preferred_element_type=F32).astype(o_ref.dtype)


def matmul(x, w, li, n_cols, out_dtype, name, tms, tn, single_buffer_w=False, w_transposed=False):
    M, K = x.shape
    tm = _pick(M, tms)
    assert n_cols % tn == 0 and w.shape[2 if w_transposed else 1] == K
    mode = dict(pipeline_mode=pl.Buffered(1)) if single_buffer_w else {}
    if w_transposed:
        w_spec = pl.BlockSpec((1, tn, K), lambda j, i: (li, j, 0), **mode)
    else:
        w_spec = pl.BlockSpec((1, K, tn), lambda j, i: (li, 0, j), **mode)
    return _pcall(
        functools.partial(_mm_kernel, w_transposed=w_transposed), name=name,
        grid=(n_cols // tn, M // tm),
        in_specs=[pl.BlockSpec((tm, K), lambda j, i: (i, 0)), w_spec],
        inputs=[x, w],
        out_specs=pl.BlockSpec((tm, tn), lambda j, i: (i, j)),
        out_shape=jax.ShapeDtypeStruct((M, n_cols), out_dtype),
        scratch=[pltpu.VMEM((K, tn), BF16)])


def _swiglu_kernel(x_ref, wg_ref, wu_ref, o_ref, wgb_ref, wub_ref):
    @pl.when(pl.program_id(1) == 0)
    def _():
        wgb_ref[...] = wg_ref[0].astype(BF16)
        wub_ref[...] = wu_ref[0].astype(BF16)

    x = x_ref[...]
    g = jnp.dot(x, wgb_ref[...], preferred_element_type=F32)
    u = jnp.dot(x, wub_ref[...], preferred_element_type=F32)
    o_ref[...] = (g * jax.nn.sigmoid(g) * u).astype(o_ref.dtype)


def swiglu_up(x, w13, li, name):
    M, K = x.shape
    F = w13.shape[2] // 2
    tm = _pick(M, (1536, 768, 384, 128, 64))
    tn = _pick(F, (256, 128))
    nb = F // tn
    return _pcall(
        _swiglu_kernel, name=name, grid=(nb, M // tm),
        in_specs=[pl.BlockSpec((tm, K), lambda j, i: (i, 0)),
                  pl.BlockSpec((1, K, tn), lambda j, i: (li, 0, j)),
                  pl.BlockSpec((1, K, tn), lambda j, i: (li, 0, j + nb))],
        inputs=[x, w13, w13],
        out_specs=pl.BlockSpec((tm, tn), lambda j, i: (i, j)),
        out_shape=jax.ShapeDtypeStruct((M, F), BF16),
        scratch=[pltpu.VMEM((K, tn), BF16), pltpu.VMEM((K, tn), BF16)])


def _split_rows_specs(tm, D, n_first):
    return [pl.BlockSpec((tm, D), lambda i: (jnp.minimum(i, n_first - 1), 0)),
            pl.BlockSpec((tm, D), lambda i: (jnp.maximum(i - n_first, 0), 0))]


def _cast_split_kernel(a_ref, b_ref, o_ref, *, n_first):
    o_ref[...] = jnp.where(pl.program_id(0) < n_first, a_ref[...], b_ref[...]).astype(o_ref.dtype)


def cast_rows(a, b, dtype, name):
    D = a.shape[1]
    tm = _pick(b.shape[0], (256, 128, 64))
    assert a.shape[0] % tm == 0
    n = (a.shape[0] + b.shape[0]) // tm
    return _pcall(
        functools.partial(_cast_split_kernel, n_first=a.shape[0] // tm), name=name, grid=(n,),
        in_specs=_split_rows_specs(tm, D, a.shape[0] // tm), inputs=[a, b],
        out_specs=pl.BlockSpec((tm, D), lambda i: (i, 0)),
        out_shape=jax.ShapeDtypeStruct((n * tm, D), dtype))


def _res_ln_kernel(*refs, alpha, yscale, n_first):
    if n_first is None:
        x_ref, y_ref, g_ref, b_ref, *o_refs = refs
        x = x_ref[...]
    else:
        xa_ref, xb_ref, y_ref, g_ref, b_ref, *o_refs = refs
        x = jnp.where(pl.program_id(0) < n_first, xa_ref[...], xb_ref[...])
    z = alpha * x + yscale * y_ref[...].astype(F32)
    mu = jnp.mean(z, axis=-1, keepdims=True)
    zc = z - mu
    var = jnp.mean(zc * zc, axis=-1, keepdims=True)
    o = zc * lax.rsqrt(var + LN_EPS) * g_ref[...] + b_ref[...]
    o_refs[0][...] = o
    if len(o_refs) > 1:
        o_refs[1][...] = o.astype(BF16)


def residual_layer_norm(x, y, g, b, alpha, yscale, name, rows=None):
    M, D = y.shape
    row0, n = rows if rows is not None else (0, M)
    tm = _pick(n if not isinstance(x, tuple) else x[1].shape[0], (256, 128, 64))
    assert row0 % tm == 0
    rb0 = row0 // tm
    src = pl.BlockSpec((tm, D), lambda i: (rb0 + i, 0))
    dst = pl.BlockSpec((tm, D), lambda i: (i, 0))
    vec = pl.BlockSpec((1, D), lambda i: (0, 0))
    shapes = [jax.ShapeDtypeStruct((n, D), F32)]
    if rows is None:
        shapes.append(jax.ShapeDtypeStruct((n, D), BF16))
    if isinstance(x, tuple):
        assert rows is None and x[0].shape[0] % tm == 0
        n_first = x[0].shape[0] // tm
        x_specs, xs = _split_rows_specs(tm, D, n_first), list(x)
    else:
        n_first, x_specs, xs = None, [src], [x]
    return _pcall(
        functools.partial(_res_ln_kernel, alpha=alpha, yscale=yscale, n_first=n_first), name=name,
        grid=(n // tm,), in_specs=x_specs + [src, vec, vec],
        inputs=xs + [y, g.reshape(1, D), b.reshape(1, D)],
        out_specs=[dst] * len(shapes), out_shape=shapes)


def _rope_tables(pos, period, half):
    inv = ROPE_THETA ** (-jnp.arange(half, dtype=F32) / half)
    ang = pos.astype(F32)[:, None] * inv[None, :]
    cos, sin = jnp.cos(ang), jnp.sin(ang)
    T = pos.shape[0]
    reps = LANES // period
    zeros = jnp.zeros((T, period - 2 * half), F32)
    c = jnp.concatenate([cos, cos, jnp.ones((T, period - 2 * half), F32)], axis=1)
    s1 = jnp.concatenate([-sin, jnp.zeros((T, half), F32), zeros], axis=1)
    s2 = jnp.concatenate([jnp.zeros((T, half), F32), sin, zeros], axis=1)
    return [jnp.tile(t, (1, reps)) for t in (c, s1, s2)]


def _rope(x, c, s1, s2, half):
    outs = []
    for g in range(x.shape[-1] // LANES):
        xg = x[:, g * LANES:(g + 1) * LANES]
        outs.append(xg * c + pltpu.roll(xg, LANES - half, 1) * s1 + pltpu.roll(xg, half, 1) * s2)
    return outs[0] if len(outs) == 1 else jnp.concatenate(outs, axis=1)


def _prep_kernel(bq_ref, bk_ref, bv_ref, bqi_ref, cq_ref, ck_ref, cv_ref, sm_ref, tab_ref, bf_ref,
                 q_o, kb_o, v_o, qi_o, sm_o, kiki_o, cq_o, ck_o, cv_o, lf_o,
                 ks_o, vs_o, cks_o, cvs_o, *, transposed):
    t = tab_ref[...]
    q_o[...] = _rope(bq_ref[...], t[0], t[1], t[2], ROPE_DIM // 2).astype(BF16)
    k = _rope(bk_ref[...], t[0], t[1], t[2], ROPE_DIM // 2)
    ks_o[...] = k
    kb_o[...] = k.astype(BF16)
    v = bv_ref[...]
    vs_o[...] = v
    qi_o[...] = _rope(bqi_ref[...], t[3], t[4], t[5], IDX_ROPE_DIM // 2).astype(BF16)
    sm = _rope(sm_ref[...], t[6], t[7], t[8], IDX_ROPE_DIM // 2)
    if transposed:
        v_o[0] = v.T.astype(BF16)
        sm_o[0] = sm.T
    else:
        v_o[...] = v.astype(BF16)
        sm_o[...] = sm
    lane = lax.broadcasted_iota(jnp.int32, sm.shape, 1)
    ki_lo = jnp.where(lane < IDX_DIM, sm, 0.0)
    kiki_o[...] = (ki_lo + pltpu.roll(ki_lo, IDX_DIM, 1)).astype(BF16)
    cq_o[...] = cq_ref[...].astype(BF16)
    ck = ck_ref[...]
    cks_o[...] = ck
    ck_o[...] = ck.astype(BF16)
    cv = cv_ref[...]
    cvs_o[...] = cv
    cv_o[...] = cv.astype(BF16)
    z = -(sm + bf_ref[...])
    lf_o[...] = jnp.where(lane < IDX_DIM, sm, -(jnp.maximum(z, 0.0) + jnp.log1p(jnp.exp(-jnp.abs(z)))))


def prep_group(P1, P2, bf_row, grp, W, layer, depth, state_bufs):
    row0, B, T = grp["row0"], grp["B"], grp["T"]
    tr = grp["transposed"]
    tt = _pick(T, (256, 128, 64))
    nt = T // tt
    rb0 = row0 // tt
    R = B * T

    def col(cb, width):
        return pl.BlockSpec((tt, width), lambda b, i, cb=cb: (rb0 + b * nt + i, cb))

    rows = lambda w: pl.BlockSpec((tt, w), lambda b, i: (b * nt + i, 0))
    srows = pl.BlockSpec((tt, W), lambda b, i: (layer * (R // tt) + b * nt + i, 0))
    bf = lambda w: (rows(w), jax.ShapeDtypeStruct((R, w), BF16))
    if tr:
        v_out = (pl.BlockSpec((1, W, tt), lambda b, i: (b, 0, i)), jax.ShapeDtypeStruct((B, W, T), BF16))
        sm_out = (pl.BlockSpec((1, LANES, tt), lambda b, i: (b, 0, i)),
                  jax.ShapeDtypeStruct((B, LANES, T), F32))
    else:
        v_out = bf(W)
        sm_out = (rows(LANES), jax.ShapeDtypeStruct((R, LANES), F32))
    state = (srows, jax.ShapeDtypeStruct((depth * R, W), F32))
    outs = [bf(W), bf(W), v_out, bf(IDX_HEADS * IDX_DIM), sm_out, bf(LANES),
            bf(W), bf(W), bf(W), (rows(LANES), jax.ShapeDtypeStruct((R, LANES), F32)),
            state, state, state, state]
    carries = [] if state_bufs is None else [(10 + k, a) for k, a in enumerate(state_bufs)]
    small_cb = (5 * W) // LANES
    res = _pcall(
        functools.partial(_prep_kernel, transposed=tr), name=f"prep_{grp['name']}", grid=(B, nt),
        in_specs=[col(3, W), col(4, W), col(5, W), col(6, W), col(0, W), col(1, W), col(2, W),
                  col(small_cb, LANES),
                  pl.BlockSpec((9, tt, LANES), lambda b, i: (0, i, 0)),
                  pl.BlockSpec((1, LANES), lambda b, i: (0, 0))],
        inputs=[P1] * 4 + [P2] * 4 + [grp["tabs"], bf_row],
        out_specs=[s for s, _ in outs], out_shape=[s for _, s in outs], carries=carries)
    return res[:10], list(res[10:])


def _shift_rows(x, d, fill_rows):
    row = lax.broadcasted_iota(jnp.int32, x.shape, 0)
    y = pltpu.roll(x, d, 0)
    for t in range(d):
        y = jnp.where(row == t, fill_rows[t:t + 1, :], y)
    return y


def _conv_a_kernel(ah_ref, ab_ref, ac_ref, st_ref, w_ref, y_ref, nst_ref):
    z = ac_ref[...] * ah_ref[...]
    st = st_ref[0]
    w = w_ref[...]
    T = z.shape[0]
    z1 = _shift_rows(z, 1, st[1:2, :])
    z2 = _shift_rows(z, 2, st)
    conv = z2 * w[0:1, :] + z1 * w[1:2, :] + z * w[2:3, :]
    y_ref[...] = (ab_ref[...] * conv).astype(y_ref.dtype)
    nst_ref[0] = z[T - 2:T, :]


def conv_a_group(P, state_a, conv_w, grp, W, ybuf, ybuf_shape):
    row0, B, T = grp["row0"], grp["B"], grp["T"]
    assert T >= CONV_A_W - 1 and row0 % T == 0
    tc = _pick(W, (256, 128))
    nc = W // tc
    rb0 = row0 // T

    def col(k):
        return pl.BlockSpec((T, tc), lambda b, c, k=k: (rb0 + b, k * nc + c))

    return _pcall(
        _conv_a_kernel, name=f"conv_a_{grp['name']}", grid=(B, nc),
        in_specs=[col(0), col(1), col(2),
                  pl.BlockSpec((1, CONV_A_W - 1, tc), lambda b, c: (b, 0, c)),
                  pl.BlockSpec((CONV_A_W, tc), lambda b, c: (0, c))],
        inputs=[P, P, P, state_a, conv_w],
        out_specs=[pl.BlockSpec((T, tc), lambda b, c: (rb0 + b, c)),
                   pl.BlockSpec((1, CONV_A_W - 1, tc), lambda b, c: (b, 0, c))],
        out_shape=[ybuf_shape, jax.ShapeDtypeStruct((B, CONV_A_W - 1, W), F32)],
        carries=[] if ybuf is None else [(0, ybuf)])


def _lru_kernel(dx_ref, dg_ref, st_ref, h0_ref, cw_ref, cb_ref, wa_ref, ba_ref, wx_ref, bx_ref,
                lam_ref, y_ref, nst_ref, hl_ref):
    x = dx_ref[...]
    T = x.shape[0]
    st = st_ref[0]
    w = cw_ref[...]
    x1 = _shift_rows(x, 1, st[2:3, :])
    x2 = _shift_rows(x, 2, st[1:3, :])
    x3 = _shift_rows(x, 3, st)
    xc = x3 * w[0:1, :] + x2 * w[1:2, :] + x1 * w[2:3, :] + x * w[3:4, :] + cb_ref[...]
    xb = xc.astype(BF16)
    r = jax.nn.sigmoid(jnp.dot(xb, wa_ref[0].astype(BF16), preferred_element_type=F32) + ba_ref[...])
    i = jax.nn.sigmoid(jnp.dot(xb, wx_ref[0].astype(BF16), preferred_element_type=F32) + bx_ref[...])
    nl = -lam_ref[...]
    sp = jnp.maximum(nl, 0.0) + jnp.log1p(jnp.exp(-jnp.abs(nl)))
    log_a = -LRU_C * r * sp
    a = jnp.exp(log_a)
    u = jnp.sqrt(-jnp.tanh(log_a) * (a * a + 1.0)) * i * xc
    row = lax.broadcasted_iota(jnp.int32, x.shape, 0)
    u = jnp.where(row == 0, u + a * h0_ref[0], u)
    d = 1
    while d < T:
        keep = row >= d
        a_prev = jnp.where(keep, pltpu.roll(a, d, 0), 1.0)
        u_prev = jnp.where(keep, pltpu.roll(u, d, 0), 0.0)
        u = a * u_prev + u
        a = a * a_prev
        d *= 2
    y_ref[...] = (jax.nn.gelu(dg_ref[...]) * u).astype(y_ref.dtype)
    nst_ref[0] = x[T - 3:T, :]
    hl_ref[0] = u[T - 1:T, :]


def lru_group(P, state_d, h0, lw, grp, W, ybuf, ybuf_shape):
    row0, B, T = grp["row0"], grp["B"], grp["T"]
    wa, wx = lw["lru_wa"], lw["lru_wx"]
    nblk, bw = wa.shape[0], wa.shape[1]
    assert T >= CONV_D_W - 1 and row0 % T == 0 and bw % LANES == 0 and nblk * bw == W
    rb0 = row0 // T
    cb = (3 * W) // bw
    yb = (3 * W) // bw

    def col(k):
        return pl.BlockSpec((T, bw), lambda b, c, k=k: (rb0 + b, cb + k * nblk + c))

    vec = pl.BlockSpec((1, bw), lambda b, c: (0, c))
    mat = pl.BlockSpec((1, bw, bw), lambda b, c: (c, 0, 0))
    row = lambda a: a.reshape(1, W)
    return _pcall(
        _lru_kernel, name=f"lru_{grp['name']}", grid=(B, nblk),
        in_specs=[col(0), col(1),
                  pl.BlockSpec((1, CONV_D_W - 1, bw), lambda b, c: (b, 0, c)),
                  pl.BlockSpec((1, 1, bw), lambda b, c: (b, 0, c)),
                  pl.BlockSpec((CONV_D_W, bw), lambda b, c: (0, c)),
                  vec, mat, vec, mat, vec, vec],
        inputs=[P, P, state_d, h0.reshape(B, 1, W), lw["conv_d_w"], row(lw["conv_d_b"]), wa,
                row(lw["lru_ba"]), wx, row(lw["lru_bx"]), row(lw["lru_lambda"])],
        out_specs=[pl.BlockSpec((T, bw), lambda b, c: (rb0 + b, yb + c)),
                   pl.BlockSpec((1, CONV_D_W - 1, bw), lambda b, c: (b, 0, c)),
                   pl.BlockSpec((1, 1, bw), lambda b, c: (b, 0, c))],
        out_shape=[ybuf_shape, jax.ShapeDtypeStruct((B, CONV_D_W - 1, W), F32),
                   jax.ShapeDtypeStruct((B, 1, W), F32)],
        carries=[(0, ybuf)])


def _order_key(x):
    bits = pltpu.bitcast(x + 0.0, jnp.int32)
    return jnp.where(bits < 0, bits ^ jnp.int32(0x7FFFFFFF), bits)


def _count(mask, axis):
    ind = jnp.where(mask, 1.0, 0.0)
    if axis == 1:
        return jnp.sum(ind, axis=1, keepdims=True)
    S, tq = ind.shape
    g = 8 if S % 64 == 0 else 1
    part = ind.reshape(g, S // (8 * g), 8, tq).sum(axis=1)
    return part.sum(axis=0).sum(axis=0, keepdims=True)


def _select_topk(key, adm, tri, n_sel, axis):
    shape = tuple(1 if a == axis else s for a, s in enumerate(key.shape))

    def bit_step(b, t):
        cand = t + lax.shift_left(jnp.int32(1), 31 - b)
        return jnp.where(_count(key >= cand, axis) >= float(n_sel), cand, t)

    thr = lax.fori_loop(0, 32, bit_step, jnp.full(shape, INT_MIN, jnp.int32))
    eq = jnp.where(key == thr, 1.0, 0.0).astype(BF16)
    blk = tri.shape[0]
    total = jnp.zeros(shape, F32)
    ranks = []
    for b in range(key.shape[axis] // blk):
        if axis == 0:
            r = jnp.dot(tri, eq[b * blk:(b + 1) * blk, :], preferred_element_type=F32) + total
            total = r[blk - 1:blk, :]
        else:
            r = jnp.dot(eq[:, b * blk:(b + 1) * blk], tri, preferred_element_type=F32) + total
            total = r[:, blk - 1:blk]
        ranks.append(r)
    rank = jnp.concatenate(ranks, axis=axis)
    need = float(n_sel) - _count(key > thr, axis)
    ninf = -jnp.inf
    tie = jnp.where(key == thr, jnp.where(rank <= need, 0.0, ninf), ninf)
    return jnp.where(adm, jnp.where(key > thr, 0.0, tie), ninf)


IDX_SCALE = (IDX_DIM ** -0.5) * (IDX_HEADS ** -0.5)
EXP2_SCALE = (HEAD_DIM ** -0.5) * math.log2(math.e)


def _tri(n, lower):
    r = lax.broadcasted_iota(jnp.int32, (n, n), 0)
    c = lax.broadcasted_iota(jnp.int32, (n, n), 1)
    return jnp.where(r >= c if lower else r <= c, 1.0, 0.0).astype(BF16)


def _head_pair_rows(x):
    lane = lax.broadcasted_iota(jnp.int32, x.shape, 1)
    zero = jnp.zeros_like(x)
    half = x.shape[1] // 2
    return jnp.concatenate([jnp.where(lane < half, x, zero), jnp.where(lane >= half, x, zero)], axis=0)


def _pad_rows(x, n):
    if x.shape[0] == n:
        return x
    return jnp.concatenate([x, jnp.zeros((n - x.shape[0], x.shape[1]), x.dtype)], axis=0)


def _past_head(ref, h, n_heads):
    return ref[pl.ds(h, ref.shape[0] // n_heads, stride=n_heads), :].astype(BF16)


def _dsa_rows_kernel(q_ref, qi_ref, sm_ref, kp_ref, vp_ref, kikip_ref, kn_ref, vn_ref, kikin_ref,
                     tri_ref, o_ref, *, chunk0, n_new, s_valid, n_sel, n_heads):
    c = pl.program_id(1)
    tq = q_ref.shape[0]
    n_past = kp_ref.shape[0] // n_heads
    S = n_past + n_new
    sm = sm_ref[...]
    kn = _pad_rows(kn_ref[...], n_new)
    vn = _pad_rows(vn_ref[...], n_new)
    kikip = kikip_ref[...].astype(BF16)
    kikin = _pad_rows(kikin_ref[...], n_new)
    score = jnp.zeros((tq, S), F32)
    for j in range(IDX_HEADS // 2):
        qq = _head_pair_rows(qi_ref[:, j * LANES:(j + 1) * LANES])
        s2 = jnp.concatenate([lax.dot_general(qq, kikip, NT, preferred_element_type=F32),
                              lax.dot_general(qq, kikin, NT, preferred_element_type=F32)], axis=1)
        for half in range(2):
            h = 2 * j + half
            w = sm[:, IDX_DIM + h:IDX_DIM + h + 1] * IDX_SCALE
            score = score + jnp.maximum(s2[half * tq:(half + 1) * tq, :], 0.0) * w
    kpos = lax.broadcasted_iota(jnp.int32, (tq, S), 1)
    adm = kpos < jnp.minimum((chunk0 + c + 1) * CHUNK, s_valid)
    key = jnp.where(adm, _order_key(score), jnp.int32(INT_MIN))
    bias = _select_topk(key, adm, tri_ref[...], n_sel, 1)

    for h in range(n_heads):
        sl = slice(h * HEAD_DIM, (h + 1) * HEAD_DIM)
        qh = q_ref[:, sl]
        lg = jnp.concatenate(
            [lax.dot_general(qh, _past_head(kp_ref, h, n_heads), NT, preferred_element_type=F32),
             lax.dot_general(qh, kn[:, sl], NT, preferred_element_type=F32)], axis=1) + bias
        p = jnp.exp2((lg - jnp.max(lg, axis=-1, keepdims=True)) * EXP2_SCALE)
        l = jnp.sum(p, axis=-1, keepdims=True)
        pb = p.astype(BF16)
        o = (jnp.dot(pb[:, :n_past], _past_head(vp_ref, h, n_heads), preferred_element_type=F32)
             + jnp.dot(pb[:, n_past:], vn[:, sl], preferred_element_type=F32))
        o_ref[:, sl] = (o / l).astype(o_ref.dtype)


def _dsa_cols_kernel(q_ref, qi_ref, wt_ref, k_ref, vt_ref, kiki_ref, tri_ref, o_ref, *,
                     tile0, n_sel, n_heads):
    i = pl.program_id(1) + tile0
    tq = q_ref.shape[0]
    S = k_ref.shape[1]
    wt = wt_ref[0]
    kiki = kiki_ref[0]
    score = jnp.zeros((S, tq), F32)
    for j in range(IDX_HEADS // 2):
        qq = _head_pair_rows(qi_ref[:, j * LANES:(j + 1) * LANES])
        s2 = lax.dot_general(kiki, qq, NT, preferred_element_type=F32)
        for half in range(2):
            h = 2 * j + half
            w = wt[IDX_DIM + h:IDX_DIM + h + 1, :] * IDX_SCALE
            score = score + jnp.maximum(s2[:, half * tq:(half + 1) * tq], 0.0) * w
    kpos = lax.broadcasted_iota(jnp.int32, (S, tq), 0)
    qpos = i * tq + lax.broadcasted_iota(jnp.int32, (1, tq), 1)
    adm = kpos < (lax.shift_right_logical(qpos, 6) + 1) * CHUNK
    key = jnp.where(adm, _order_key(score), jnp.int32(INT_MIN))
    bias = _select_topk(key, adm, tri_ref[...], n_sel, 0)

    for hp in range(n_heads // 2):
        sl2 = slice(2 * hp * HEAD_DIM, (2 * hp + 2) * HEAD_DIM)
        lg2 = lax.dot_general(k_ref[0, :, sl2], _head_pair_rows(q_ref[:, sl2]), NT,
                              preferred_element_type=F32)
        for half in range(2):
            sl = slice((2 * hp + half) * HEAD_DIM, (2 * hp + half + 1) * HEAD_DIM)
            lg = lg2[:, half * tq:(half + 1) * tq] + bias
            p = jnp.exp2((lg - jnp.max(lg, axis=0, keepdims=True)) * EXP2_SCALE)
            l = jnp.sum(p, axis=0, keepdims=True)
            ot = jnp.dot(vt_ref[0, sl, :], p.astype(BF16), preferred_element_type=F32)
            o_ref[:, sl] = (ot / l).T.astype(o_ref.dtype)


def _tri_block(n):
    return 256 if n % 256 == 0 else LANES


def dsa_rows_group(q, qi, smf, k_new, v_new, kiki_new, k_past, v_past, kiki_past, layer, grp, W,
                   ybuf, ybuf_shape):
    B, T, S, S_pad, pos0, row0 = (grp[k] for k in ("B", "T", "S", "S_pad", "pos0", "row0"))
    assert pos0 % LANES == 0 and T % CHUNK == 0 and pos0 > 0
    nq = T // CHUNK
    blk = _tri_block(S_pad)
    qrow = lambda w: pl.BlockSpec((CHUNK, w), lambda b, c: (b * nq + c, 0))
    new = lambda w: pl.BlockSpec((T, w), lambda b, c: (b, 0))
    kern = functools.partial(_dsa_rows_kernel, chunk0=pos0 // CHUNK, n_new=S_pad - pos0, s_valid=S,
                             n_sel=min(TOPK_MAX, S // 4), n_heads=W // HEAD_DIM)
    return _pcall(
        kern, name=f"dsa_{grp['name']}", grid=(B, nq),
        in_specs=[qrow(W), qrow(IDX_HEADS * IDX_DIM), qrow(LANES),
                  pl.BlockSpec((pos0 * (W // HEAD_DIM), HEAD_DIM), lambda b, c: (layer * B + b, 0)),
                  pl.BlockSpec((pos0 * (W // HEAD_DIM), HEAD_DIM), lambda b, c: (layer * B + b, 0)),
                  pl.BlockSpec((pos0, LANES), lambda b, c: (b, 0)),
                  new(W), new(W), new(LANES),
                  pl.BlockSpec((blk, blk), lambda b, c: (0, 0))],
        inputs=[q, qi, smf, k_past, v_past, kiki_past, k_new, v_new, kiki_new, _tri(blk, False)],
        out_specs=pl.BlockSpec((CHUNK, W), lambda b, c: (row0 // CHUNK + b * nq + c, 1)),
        out_shape=ybuf_shape, carries=[(0, ybuf)])


def dsa_cols_group(q, qi, smT, k_b, vT, kiki, grp, W, ybuf, ybuf_shape):
    B, T, row0 = grp["B"], grp["T"], grp["row0"]
    tq = LANES
    H = W // HEAD_DIM
    assert grp["pos0"] == 0 and T % tq == 0 and row0 % tq == 0 and H % 2 == 0
    nq = T // tq
    kb = max(tq, T // 4)
    tpc = kb // tq
    blk = _tri_block(kb)
    k3, kiki3 = k_b.reshape(B, T, W), kiki.reshape(B, T, LANES)
    for cls in range(nq // tpc):
        s_eff = (cls + 1) * kb
        t0 = cls * tpc
        qrow = lambda w: pl.BlockSpec((tq, w), lambda b, i: (b * nq + t0 + i, 0))
        keys = lambda w: pl.BlockSpec((1, s_eff, w), lambda b, i: (b, 0, 0))
        kern = functools.partial(_dsa_cols_kernel, tile0=t0, n_sel=min(TOPK_MAX, T // 4), n_heads=H)
        ybuf = _pcall(
            kern, name=f"dsa_{grp['name']}_{cls}", grid=(B, tpc),
            in_specs=[qrow(W), qrow(IDX_HEADS * IDX_DIM),
                      pl.BlockSpec((1, LANES, tq), lambda b, i: (b, 0, t0 + i)),
                      keys(W), pl.BlockSpec((1, W, s_eff), lambda b, i: (b, 0, 0)), keys(LANES),
                      pl.BlockSpec((blk, blk), lambda b, i: (0, 0))],
            inputs=[q, qi, smT, k3, vT, kiki3, _tri(blk, True)],
            out_specs=pl.BlockSpec((tq, W), lambda b, i: (row0 // tq + b * nq + t0 + i, 1)),
            out_shape=ybuf_shape, carries=[(0, ybuf)])
    return ybuf


def _cumsum_kernel(x_ref, o_ref):
    x = x_ref[...]
    n = x.shape[0]
    row = lax.broadcasted_iota(jnp.int32, x.shape, 0)
    d = 1
    while d < n:
        x = x + jnp.where(row >= d, pltpu.roll(x, d, 0), 0.0)
        d *= 2
    o_ref[...] = x


def cumsum_rows(x, B, n, name):
    spec = pl.BlockSpec((n, LANES), lambda b: (b, 0))
    return _pcall(_cumsum_kernel, name=name, grid=(B,), in_specs=[spec], inputs=[x], out_specs=spec,
                  out_shape=jax.ShapeDtypeStruct((B * n, LANES), F32))


def _fox_kernel(*refs, pos0, tile0, n_new, s_valid, fcol0, n_heads):
    if pos0:
        q_ref, fq_ref, fk_ref, kp_ref, vp_ref, kn_ref, vn_ref, o_ref = refs
    else:
        q_ref, fq_ref, fk_ref, kn_ref, vn_ref, o_ref = refs
    i = pl.program_id(1) + tile0
    tq = q_ref.shape[0]
    S = pos0 + n_new
    kn = _pad_rows(kn_ref[0], n_new)
    vn = _pad_rows(vn_ref[0], n_new)
    qpos = pos0 + i * tq + lax.broadcasted_iota(jnp.int32, (tq, S), 0)
    kpos = lax.broadcasted_iota(jnp.int32, (tq, S), 1)
    ok = (kpos <= qpos) & (kpos < s_valid)
    fq = fq_ref[...] * (HEAD_DIM ** 0.5)
    fk = fk_ref[0] * (HEAD_DIM ** 0.5)
    for h in range(n_heads):
        sl = slice(h * HEAD_DIM, (h + 1) * HEAD_DIM)
        qh = q_ref[:, sl]
        lg = lax.dot_general(qh, kn[:, sl], NT, preferred_element_type=F32)
        if pos0:
            lg = jnp.concatenate(
                [lax.dot_general(qh, _past_head(kp_ref, h, n_heads), NT, preferred_element_type=F32),
                 lg], axis=1)
        lg = lg + fq[:, fcol0 + h:fcol0 + h + 1] - fk[h:h + 1, :]
        lg = jnp.where(ok, lg, -jnp.inf)
        p = jnp.exp2((lg - jnp.max(lg, axis=-1, keepdims=True)) * EXP2_SCALE)
        l = jnp.sum(p, axis=-1, keepdims=True)
        pb = p.astype(BF16)
        o = jnp.dot(pb[:, pos0:], vn[:, sl], preferred_element_type=F32)
        if pos0:
            o = o + jnp.dot(pb[:, :pos0], _past_head(vp_ref, h, n_heads), preferred_element_type=F32)
        o_ref[:, sl] = (o / l).astype(o_ref.dtype)


def fox_group(q, Fq, Fk_t, k_new, v_new, k_past, v_past, layer, grp, W, fcol0, ybuf, ybuf_shape):
    B, T, S, S_pad, pos0, row0 = (grp[k] for k in ("B", "T", "S", "S_pad", "pos0", "row0"))
    H = W // HEAD_DIM
    assert pos0 % LANES == 0
    tq = _pick(T, (256, 128, 64))
    nq = T // tq
    tpc = max(1, nq // 4) if pos0 == 0 else nq
    k3, v3 = k_new.reshape(B, T, W), v_new.reshape(B, T, W)
    past = pl.BlockSpec((pos0 * H, HEAD_DIM), lambda b, i: (layer * B + b, 0))
    for cls in range(nq // tpc):
        t0 = cls * tpc
        n_keys = min(T, (t0 + tpc) * tq)
        n_new = -(-n_keys // LANES) * LANES
        qrow = lambda w: pl.BlockSpec((tq, w), lambda b, i: (b * nq + t0 + i, 0))
        new = pl.BlockSpec((1, n_keys, W), lambda b, i: (b, 0, 0))
        kern = functools.partial(_fox_kernel, pos0=pos0, tile0=t0, n_new=n_new, s_valid=S,
                                 fcol0=fcol0, n_heads=H)
        ybuf = _pcall(
            kern, name=f"fox_{grp['name']}_{cls}", grid=(B, tpc),
            in_specs=[qrow(W), qrow(LANES), pl.BlockSpec((1, H, pos0 + n_new), lambda b, i: (b, 0, 0))]
            + ([past, past] if pos0 else []) + [new, new],
            inputs=[q, Fq, Fk_t] + ([k_past, v_past] if pos0 else []) + [k3, v3],
            out_specs=pl.BlockSpec((tq, W), lambda b, i: (row0 // tq + b * nq + t0 + i, 2)),
            out_shape=ybuf_shape, carries=[(0, ybuf)])
    return ybuf


def _pack_w_tail(w_in_t, W, H_C):
    sizes = (W, W, W, W, W, W, IDX_HEADS * IDX_DIM, IDX_DIM, IDX_HEADS, W, W, W, H_C, W, W)
    offs = [0]
    for sz in sizes:
        offs.append(offs[-1] + sz)
    seg = lambda k: w_in_t[:, offs[k]:offs[k + 1], :]
    L, _, D = w_in_t.shape
    pad = jnp.zeros((L, SMALL_W - IDX_DIM - IDX_HEADS - H_C, D), w_in_t.dtype)
    return jnp.concatenate([seg(k) for k in (9, 10, 11, 13, 14, 7, 8, 12)] + [pad], axis=1)


def _token_mix(xb, lw, groups, pasts, layer, depth, state_bufs):
    W = lw["W"]
    H = W // HEAD_DIM
    M = xb.shape[0]
    fcol0 = IDX_DIM + IDX_HEADS
    tms = (1536, 768, 384, 128, 64)
    P1 = matmul(xb, lw["w_in_t"], layer, 7 * W, F32, "in_proj_a", tms, 512, w_transposed=True)
    P2 = matmul(xb, lw["w_tail_t"], layer, 5 * W + SMALL_W, F32, "in_proj_b", tms, 512,
                w_transposed=True)
    bf_row = jnp.zeros((1, LANES), F32).at[0, fcol0:fcol0 + H].set(lw["fox_b_f"])
    ybuf_shape = jax.ShapeDtypeStruct((M, 4 * W), BF16)
    ybuf = None
    small_states, new_bufs = [], []
    for gi, (grp, past) in enumerate(zip(groups, pasts)):
        B, T, S, S_pad = grp["B"], grp["T"], grp["S"], grp["S_pad"]
        pk_b, pv_b, pki_b, pk_c, pv_c, plf_c, st_a, st_d, h0 = past
        npast = S - T
        (q_b, k_b, v_x, qi_b, sm_x, kiki, cq_b, ck_b, cv_b, lf), bufs = prep_group(
            P1, P2, bf_row, grp, W, layer, depth, None if state_bufs is None else state_bufs[gi])
        new_bufs.append(bufs)

        ybuf, nst_a = conv_a_group(P1, st_a, lw["conv_a_w"], grp, W, ybuf, ybuf_shape)
        ybuf, nst_d, h_last = lru_group(P2, st_d, h0, lw, grp, W, ybuf, ybuf_shape)

        lf3 = lf.reshape(B, T, LANES)
        if grp["transposed"]:
            ybuf = dsa_cols_group(q_b, qi_b, sm_x, k_b, v_x, kiki, grp, W, ybuf, ybuf_shape)
            lf_all = lf3
        else:
            pki = pki_b[layer]
            kiki_past = jnp.concatenate([pki, pki], axis=-1).reshape(B * npast, LANES)
            ybuf = dsa_rows_group(q_b, qi_b, sm_x, k_b, v_x, kiki, pk_b, pv_b, kiki_past, layer, grp,
                                  W, ybuf, ybuf_shape)
            plf = jnp.pad(plf_c[layer].astype(F32), ((0, 0), (0, 0), (fcol0, LANES - fcol0 - H)))
            lf_all = jnp.concatenate([plf, lf3, jnp.zeros((B, S_pad - S, LANES), F32)], axis=1)

        F_all = cumsum_rows(lf_all.reshape(B * S_pad, LANES), B, S_pad, f"cumsum_{grp['name']}")
        F3 = F_all.reshape(B, S_pad, LANES)
        Fq = F3[:, npast:npast + T, :].reshape(B * T, LANES)
        Fk_t = jnp.swapaxes(F3[:, :, fcol0:fcol0 + H], 1, 2)
        ybuf = fox_group(cq_b, Fq, Fk_t, ck_b, cv_b, pk_c, pv_c, layer, grp, W, fcol0, ybuf, ybuf_shape)

        small_states.append((lf3[:, :, :IDX_DIM], lf3[:, :, fcol0:fcol0 + H],
                             nst_a, nst_d, h_last.reshape(B, W)))
    y = matmul(ybuf, lw["w_out"], layer, 4 * W, BF16, "out_proj", tms, 512)
    return y, small_states, new_bufs


def kernel(x_prompt, x_sample, cache_dsa_k, cache_dsa_v, cache_dsa_kidx, cache_fox_k, cache_fox_v,
           cache_fox_logf, state_conv_a, state_conv_d, state_lru, ln_g, ln_b, ffn_w13, ffn_w2, w_in,
           fox_b_f, conv_a_w, conv_d_w, conv_d_b, lru_wa, lru_ba, lru_wx, lru_bx, lru_lambda, w_out):
    Bp, Tp, D = x_prompt.shape
    Bs, Ts, _ = x_sample.shape
    depth = ln_g.shape[0]
    past_len = cache_dsa_k.shape[2]
    W = D // 4
    H = W // HEAD_DIM
    assert IDX_HEADS * IDX_DIM == W and W % HEAD_DIM == 0
    alpha = (2.0 * depth) ** 0.25
    dt = x_prompt.dtype

    def group(name, row0, B, T, pos0):
        S = pos0 + T
        S_pad = -(-S // LANES) * LANES
        pos = pos0 + jnp.arange(T, dtype=jnp.int32)
        tabs = jnp.stack(_rope_tables(pos, HEAD_DIM, ROPE_DIM // 2)
                         + _rope_tables(pos, IDX_DIM, IDX_ROPE_DIM // 2)
                         + _rope_tables(pos, LANES, IDX_ROPE_DIM // 2))
        return dict(name=name, row0=row0, B=B, T=T, pos0=pos0, S=S, S_pad=S_pad, tabs=tabs,
                    transposed=(pos0 == 0 and T % LANES == 0))

    groups = [group("prompt", 0, Bp, Tp, 0), group("sample", Bp * Tp, Bs, Ts, past_len)]
    Rp, Rs = Bp * Tp, Bs * Ts

    x = (x_prompt.reshape(Rp, D), x_sample.reshape(Rs, D))
    xb = cast_rows(x[0], x[1], BF16, "cast_x")

    w13 = ffn_w13.reshape((2 * depth,) + ffn_w13.shape[2:])
    w2 = ffn_w2.reshape((2 * depth,) + ffn_w2.shape[2:])
    w_in_t = jnp.swapaxes(w_in, 1, 2)
    w_tail_t = _pack_w_tail(w_in_t, W, H)
    rows2d = lambda c: c.reshape(-1, HEAD_DIM)
    cache_rows = (rows2d(cache_dsa_k), rows2d(cache_dsa_v), cache_dsa_kidx, rows2d(cache_fox_k),
                  rows2d(cache_fox_v), cache_fox_logf)
    down_tms = (768, 384, 128, 64)

    small = [[], []]
    state_bufs = None
    for l in range(depth):
        lw = dict(W=W, w_in_t=w_in_t, w_tail_t=w_tail_t, fox_b_f=fox_b_f[l], conv_a_w=conv_a_w[l],
                  conv_d_w=conv_d_w[l], conv_d_b=conv_d_b[l], lru_wa=lru_wa[l], lru_ba=lru_ba[l],
                  lru_wx=lru_wx[l], lru_bx=lru_bx[l], lru_lambda=lru_lambda[l], w_out=w_out)
        empty = (None,) * 6 + (jnp.zeros((Bp, CONV_A_W - 1, W), dt),
                               jnp.zeros((Bp, CONV_D_W - 1, W), dt), jnp.zeros((Bp, W), dt))
        sample_past = cache_rows + (state_conv_a[l], state_conv_d[l], state_lru[l])

        h = swiglu_up(xb, w13, 2 * l, "ffn_up")
        f = matmul(h, w2, 2 * l, D, BF16, "ffn_down", down_tms, 512, single_buffer_w=True)
        x, xb = residual_layer_norm(x, f, ln_g[l, 0], ln_b[l, 0], alpha, 0.5, "res_ln")

        y, small_states, state_bufs = _token_mix(xb, lw, groups, [empty, sample_past], l, depth,
                                                 state_bufs)
        x, xb = residual_layer_norm(x, y, ln_g[l, 1], ln_b[l, 1], alpha, 1.0, "res_ln")

        h = swiglu_up(xb, w13, 2 * l + 1, "ffn_up")
        f = matmul(h, w2, 2 * l + 1, D, BF16, "ffn_down", down_tms, 512, single_buffer_w=True)
        if l + 1 < depth:
            x, xb = residual_layer_norm(x, f, ln_g[l, 2], ln_b[l, 2], alpha, 0.5, "res_ln")
        else:
            outs = [residual_layer_norm(x, f, ln_g[l, 2], ln_b[l, 2], alpha, 0.5, "res_ln_out",
                                        rows=r)[0] for r in ((0, Rp), (Rp, Rs))]
        for g in range(2):
            small[g].append(small_states[g])

    res = [outs[0].reshape(Bp, Tp, D), outs[1].reshape(Bs, Ts, D)]
    for g, (B, T) in enumerate(((Bp, Tp), (Bs, Ts))):
        ks, vs, cks, cvs = (a.reshape(depth, B, T, H, HEAD_DIM) for a in state_bufs[g])
        kidx, logf, st_a, st_d, h_last = (jnp.stack([s[i] for s in small[g]]) for i in range(5))
        res += [ks, vs, kidx, cks, cvs, logf, st_a, st_d, h_last]
    return tuple(res)
```

```python
import functools
import math

import jax
import jax.numpy as jnp
from jax import lax
from jax.experimental import pallas as pl
from jax.experimental.pallas import tpu as pltpu

F32 = jnp.float32
BF16 = jnp.bfloat16

CHUNK = 64
HEAD_DIM = 128
ROPE_DIM = HEAD_DIM // 4
ROPE_THETA = 500000.0
IDX_HEADS = 16
IDX_DIM = 64
IDX_ROPE_DIM = IDX_DIM // 4
TOPK_MAX = 256
CONV_A_W = 3
CONV_D_W = 4
LRU_C = 8.0
LN_EPS = 1e-5
LANES = 128
SMALL_W = 512
VMEM_LIMIT = 58 * 1024 * 1024
INT_MIN = -2 ** 31
NT = (((1,), (1,)), ((), ()))


def _pcall(body, *, name, grid, in_specs, inputs, out_specs, out_shape, carries=(), scratch=()):
    n_in, n_c = len(inputs), len(carries)

    def kern(*refs):
        body(*refs[:n_in], *refs[n_in + n_c:])

    return pl.pallas_call(
        kern, name=name, grid=grid,
        in_specs=list(in_specs) + [pl.BlockSpec(memory_space=pl.ANY)] * n_c,
        out_specs=out_specs, out_shape=out_shape,
        input_output_aliases={n_in + k: oi for k, (oi, _) in enumerate(carries)},
        scratch_shapes=list(scratch),
        compiler_params=pltpu.CompilerParams(dimension_semantics=("arbitrary",) * len(grid),
                                             vmem_limit_bytes=VMEM_LIMIT),
    )(*inputs, *[a for _, a in carries])


def _pick(n, cands):
    for c in cands:
        if n % c == 0:
            return c
    raise ValueError(f"no tile for {n} in {cands}")


def _mm_kernel(x_ref, w_ref, o_ref, wb_ref, *, w_transposed):
    @pl.when(pl.program_id(1) == 0)
    def _():
        w = w_ref[0]
        wb_ref[...] = (w.T if w_transposed else w).astype(BF16)

    o_ref[...] = jnp.dot(x_ref[...], wb_ref[...], preferred_element_type=F32).astype(o_ref.dtype)


def matmul(x, w, li, n_cols, out_dtype, name, tms, tn, single_buffer_w=False, w_transposed=False):
    M, K = x.shape
    tm = _pick(M, tms)
    assert n_cols % tn == 0 and w.shape[2 if w_transposed else 1] == K
    mode = dict(pipeline_mode=pl.Buffered(1)) if single_buffer_w else {}
    if w_transposed:
        w_spec = pl.BlockSpec((1, tn, K), lambda j, i: (li, j, 0), **mode)
    else:
        w_spec = pl.BlockSpec((1, K, tn), lambda j, i: (li, 0, j), **mode)
    return _pcall(
        functools.partial(_mm_kernel, w_transposed=w_transposed), name=name,
        grid=(n_cols // tn, M // tm),
        in_specs=[pl.BlockSpec((tm, K), lambda j, i: (i, 0)), w_spec],
        inputs=[x, w],
        out_specs=pl.BlockSpec((tm, tn), lambda j, i: (i, j)),
        out_shape=jax.ShapeDtypeStruct((M, n_cols), out_dtype),
        scratch=[pltpu.VMEM((K, tn), BF16)])


def _swiglu_kernel(x_ref, wg_ref, wu_ref, o_ref, wgb_ref, wub_ref):
    @pl.when(pl.program_id(1) == 0)
    def _():
        wgb_ref[...] = wg_ref[0].astype(BF16)
        wub_ref[...] = wu_ref[0].astype(BF16)

    x = x_ref[...]
    g = jnp.dot(x, wgb_ref[...], preferred_element_type=F32)
    u = jnp.dot(x, wub_ref[...], preferred_element_type=F32)
    o_ref[...] = (g * jax.nn.sigmoid(g) * u).astype(o_ref.dtype)


def swiglu_up(x, w13, li, name):
    M, K = x.shape
    F = w13.shape[2] // 2
    tm = _pick(M, (1536, 768, 384, 128, 64))
    tn = _pick(F, (256, 128))
    nb = F // tn
    return _pcall(
        _swiglu_kernel, name=name, grid=(nb, M // tm),
        in_specs=[pl.BlockSpec((tm, K), lambda j, i: (i, 0)),
                  pl.BlockSpec((1, K, tn), lambda j, i: (li, 0, j)),
                  pl.BlockSpec((1, K, tn), lambda j, i: (li, 0, j + nb))],
        inputs=[x, w13, w13],
        out_specs=pl.BlockSpec((tm, tn), lambda j, i: (i, j)),
        out_shape=jax.ShapeDtypeStruct((M, F), BF16),
        scratch=[pltpu.VMEM((K, tn), BF16), pltpu.VMEM((K, tn), BF16)])


def _split_rows_specs(tm, D, n_first):
    return [pl.BlockSpec((tm, D), lambda i: (jnp.minimum(i, n_first - 1), 0)),
            pl.BlockSpec((tm, D), lambda i: (jnp.maximum(i - n_first, 0), 0))]


def _cast_split_kernel(a_ref, b_ref, o_ref, *, n_first):
    o_ref[...] = jnp.where(pl.program_id(0) < n_first, a_ref[...], b_ref[...]).astype(o_ref.dtype)


def cast_rows(a, b, dtype, name):
    D = a.shape[1]
    tm = _pick(b.shape[0], (256, 128, 64))
    assert a.shape[0] % tm == 0
    n = (a.shape[0] + b.shape[0]) // tm
    return _pcall(
        functools.partial(_cast_split_kernel, n_first=a.shape[0] // tm), name=name, grid=(n,),
        in_specs=_split_rows_specs(tm, D, a.shape[0] // tm), inputs=[a, b],
        out_specs=pl.BlockSpec((tm, D), lambda i: (i, 0)),
        out_shape=jax.ShapeDtypeStruct((n * tm, D), dtype))


def _res_ln_kernel(*refs, alpha, yscale, n_first):
    if n_first is None:
        x_ref, y_ref, g_ref, b_ref, *o_refs = refs
        x = x_ref[...]
    else:
        xa_ref, xb_ref, y_ref, g_ref, b_ref, *o_refs = refs
        x = jnp.where(pl.program_id(0) < n_first, xa_ref[...], xb_ref[...])
    z = alpha * x + yscale * y_ref[...].astype(F32)
    mu = jnp.mean(z, axis=-1, keepdims=True)
    zc = z - mu
    var = jnp.mean(zc * zc, axis=-1, keepdims=True)
    o = zc * lax.rsqrt(var + LN_EPS) * g_ref[...] + b_ref[...]
    o_refs[0][...] = o
    if len(o_refs) > 1:
        o_refs[1][...] = o.astype(BF16)


def residual_layer_norm(x, y, g, b, alpha, yscale, name, rows=None):
    M, D = y.shape
    row0, n = rows if rows is not None else (0, M)
    tm = _pick(n if not isinstance(x, tuple) else x[1].shape[0], (256, 128, 64))
    assert row0 % tm == 0
    rb0 = row0 // tm
    src = pl.BlockSpec((tm, D), lambda i: (rb0 + i, 0))
    dst = pl.BlockSpec((tm, D), lambda i: (i, 0))
    vec = pl.BlockSpec((1, D), lambda i: (0, 0))
    shapes = [jax.ShapeDtypeStruct((n, D), F32)]
    if rows is None:
        shapes.append(jax.ShapeDtypeStruct((n, D), BF16))
    if isinstance(x, tuple):
        assert rows is None and x[0].shape[0] % tm == 0
        n_first = x[0].shape[0] // tm
        x_specs, xs = _split_rows_specs(tm, D, n_first), list(x)
    else:
        n_first, x_specs, xs = None, [src], [x]
    return _pcall(
        functools.partial(_res_ln_kernel, alpha=alpha, yscale=yscale, n_first=n_first), name=name,
        grid=(n // tm,), in_specs=x_specs + [src, vec, vec],
        inputs=xs + [y, g.reshape(1, D), b.reshape(1, D)],
        out_specs=[dst] * len(shapes), out_shape=shapes)


def _rope_tables(pos, period, half):
    inv = ROPE_THETA ** (-jnp.arange(half, dtype=F32) / half)
    ang = pos.astype(F32)[:, None] * inv[None, :]
    cos, sin = jnp.cos(ang), jnp.sin(ang)
    T = pos.shape[0]
    reps = LANES // period
    zeros = jnp.zeros((T, period - 2 * half), F32)
    c = jnp.concatenate([cos, cos, jnp.ones((T, period - 2 * half), F32)], axis=1)
    s1 = jnp.concatenate([-sin, jnp.zeros((T, half), F32), zeros], axis=1)
    s2 = jnp.concatenate([jnp.zeros((T, half), F32), sin, zeros], axis=1)
    return [jnp.tile(t, (1, reps)) for t in (c, s1, s2)]


def _rope(x, c, s1, s2, half):
    outs = []
    for g in range(x.shape[-1] // LANES):
        xg = x[:, g * LANES:(g + 1) * LANES]
        outs.append(xg * c + pltpu.roll(xg, LANES - half, 1) * s1 + pltpu.roll(xg, half, 1) * s2)
    return outs[0] if len(outs) == 1 else jnp.concatenate(outs, axis=1)


def _store_heads(o_ref, x):
    n_heads = x.shape[1] // HEAD_DIM
    for h in range(n_heads):
        o_ref[pl.ds(h, x.shape[0], stride=n_heads), :] = x[:, h * HEAD_DIM:(h + 1) * HEAD_DIM]


def _prep_kernel(bq_ref, bk_ref, bv_ref, bqi_ref, cq_ref, ck_ref, cv_ref, sm_ref, tab_ref, bf_ref,
                 q_o, kb_o, v_o, qi_o, sm_o, kiki_o, cq_o, ck_o, cv_o, lf_o,
                 ks_o, vs_o, cks_o, cvs_o, *, transposed):
    t = tab_ref[...]
    q_o[...] = _rope(bq_ref[...], t[0], t[1], t[2], ROPE_DIM // 2).astype(BF16)
    k = _rope(bk_ref[...], t[0], t[1], t[2], ROPE_DIM // 2)
    _store_heads(ks_o, k)
    kb_o[...] = k.astype(BF16)
    v = bv_ref[...]
    _store_heads(vs_o, v)
    qi_o[...] = _rope(bqi_ref[...], t[3], t[4], t[5], IDX_ROPE_DIM // 2).astype(BF16)
    sm = _rope(sm_ref[...], t[6], t[7], t[8], IDX_ROPE_DIM // 2)
    if transposed:
        v_o[0] = v.T.astype(BF16)
        sm_o[0] = sm.T
    else:
        v_o[...] = v.astype(BF16)
        sm_o[...] = sm
    lane = lax.broadcasted_iota(jnp.int32, sm.shape, 1)
    ki_lo = jnp.where(lane < IDX_DIM, sm, 0.0)
    kiki_o[...] = (ki_lo + pltpu.roll(ki_lo, IDX_DIM, 1)).astype(BF16)
    cq_o[...] = cq_ref[...].astype(BF16)
    ck = ck_ref[...]
    _store_heads(cks_o, ck)
    ck_o[...] = ck.astype(BF16)
    cv = cv_ref[...]
    _store_heads(cvs_o, cv)
    cv_o[...] = cv.astype(BF16)
    z = -(sm + bf_ref[...])
    lf_o[...] = jnp.where(lane < IDX_DIM, sm, -(jnp.maximum(z, 0.0) + jnp.log1p(jnp.exp(-jnp.abs(z)))))


def prep_group(P1, P2, bf_row, grp, W, layer, depth, state_bufs):
    row0, B, T = grp["row0"], grp["B"], grp["T"]
    tr = grp["transposed"]
    tt = _pick(T, (256, 128, 64))
    nt = T // tt
    rb0 = row0 // tt
    R = B * T

    def col(cb, width):
        return pl.BlockSpec((tt, width), lambda b, i, cb=cb: (rb0 + b * nt + i, cb))

    rows = lambda w: pl.BlockSpec((tt, w), lambda b, i: (b * nt + i, 0))
    n_heads = W // HEAD_DIM
    srows = pl.BlockSpec((tt * n_heads, HEAD_DIM), lambda b, i: (layer * (R // tt) + b * nt + i, 0))
    bf = lambda w: (rows(w), jax.ShapeDtypeStruct((R, w), BF16))
    if tr:
        v_out = (pl.BlockSpec((1, W, tt), lambda b, i: (b, 0, i)), jax.ShapeDtypeStruct((B, W, T), BF16))
        sm_out = (pl.BlockSpec((1, LANES, tt), lambda b, i: (b, 0, i)),
                  jax.ShapeDtypeStruct((B, LANES, T), F32))
    else:
        v_out = bf(W)
        sm_out = (rows(LANES), jax.ShapeDtypeStruct((R, LANES), F32))
    state = (srows, jax.ShapeDtypeStruct((depth * R * n_heads, HEAD_DIM), F32))
    outs = [bf(W), bf(W), v_out, bf(IDX_HEADS * IDX_DIM), sm_out, bf(LANES),
            bf(W), bf(W), bf(W), (rows(LANES), jax.ShapeDtypeStruct((R, LANES), F32)),
            state, state, state, state]
    carries = [] if state_bufs is None else [(10 + k, a) for k, a in enumerate(state_bufs)]
    small_cb = (5 * W) // LANES
    res = _pcall(
        functools.partial(_prep_kernel, transposed=tr), name=f"prep_{grp['name']}", grid=(B, nt),
        in_specs=[col(3, W), col(4, W), col(5, W), col(6, W), col(0, W), col(1, W), col(2, W),
                  col(small_cb, LANES),
                  pl.BlockSpec((9, tt, LANES), lambda b, i: (0, i, 0)),
                  pl.BlockSpec((1, LANES), lambda b, i: (0, 0))],
        inputs=[P1] * 4 + [P2] * 4 + [grp["tabs"], bf_row],
        out_specs=[s for s, _ in outs], out_shape=[s for _, s in outs], carries=carries)
    return res[:10], list(res[10:])


def _shift_rows(x, d, fill_rows):
    row = lax.broadcasted_iota(jnp.int32, x.shape, 0)
    y = pltpu.roll(x, d, 0)
    for t in range(d):
        y = jnp.where(row == t, fill_rows[t:t + 1, :], y)
    return y


def _conv_a_kernel(ah_ref, ab_ref, ac_ref, st_ref, w_ref, y_ref, nst_ref):
    z = ac_ref[...] * ah_ref[...]
    st = st_ref[0]
    w = w_ref[...]
    T = z.shape[0]
    z1 = _shift_rows(z, 1, st[1:2, :])
    z2 = _shift_rows(z, 2, st)
    conv = z2 * w[0:1, :] + z1 * w[1:2, :] + z * w[2:3, :]
    y_ref[...] = (ab_ref[...] * conv).astype(y_ref.dtype)
    nst_ref[0] = z[T - 2:T, :]


def conv_a_group(P, state_a, conv_w, grp, W, ybuf, ybuf_shape):
    row0, B, T = grp["row0"], grp["B"], grp["T"]
    assert T >= CONV_A_W - 1 and row0 % T == 0
    tc = _pick(W, (256, 128))
    nc = W // tc
    rb0 = row0 // T

    def col(k):
        return pl.BlockSpec((T, tc), lambda b, c, k=k: (rb0 + b, k * nc + c))

    return _pcall(
        _conv_a_kernel, name=f"conv_a_{grp['name']}", grid=(B, nc),
        in_specs=[col(0), col(1), col(2),
                  pl.BlockSpec((1, CONV_A_W - 1, tc), lambda b, c: (b, 0, c)),
                  pl.BlockSpec((CONV_A_W, tc), lambda b, c: (0, c))],
        inputs=[P, P, P, state_a, conv_w],
        out_specs=[pl.BlockSpec((T, tc), lambda b, c: (rb0 + b, c)),
                   pl.BlockSpec((1, CONV_A_W - 1, tc), lambda b, c: (b, 0, c))],
        out_shape=[ybuf_shape, jax.ShapeDtypeStruct((B, CONV_A_W - 1, W), F32)],
        carries=[] if ybuf is None else [(0, ybuf)])


def _lru_kernel(dx_ref, dg_ref, st_ref, h0_ref, cw_ref, cb_ref, wa_ref, ba_ref, wx_ref, bx_ref,
                lam_ref, y_ref, nst_ref, hl_ref):
    x = dx_ref[...]
    T = x.shape[0]
    st = st_ref[0]
    w = cw_ref[...]
    x1 = _shift_rows(x, 1, st[2:3, :])
    x2 = _shift_rows(x, 2, st[1:3, :])
    x3 = _shift_rows(x, 3, st)
    xc = x3 * w[0:1, :] + x2 * w[1:2, :] + x1 * w[2:3, :] + x * w[3:4, :] + cb_ref[...]
    xb = xc.astype(BF16)
    r = jax.nn.sigmoid(jnp.dot(xb, wa_ref[0].astype(BF16), preferred_element_type=F32) + ba_ref[...])
    i = jax.nn.sigmoid(jnp.dot(xb, wx_ref[0].astype(BF16), preferred_element_type=F32) + bx_ref[...])
    nl = -lam_ref[...]
    sp = jnp.maximum(nl, 0.0) + jnp.log1p(jnp.exp(-jnp.abs(nl)))
    log_a = -LRU_C * r * sp
    a = jnp.exp(log_a)
    u = jnp.sqrt(-jnp.tanh(log_a) * (a * a + 1.0)) * i * xc
    row = lax.broadcasted_iota(jnp.int32, x.shape, 0)
    u = jnp.where(row == 0, u + a * h0_ref[0], u)
    d = 1
    while d < T:
        keep = row >= d
        a_prev = jnp.where(keep, pltpu.roll(a, d, 0), 1.0)
        u_prev = jnp.where(keep, pltpu.roll(u, d, 0), 0.0)
        u = a * u_prev + u
        a = a * a_prev
        d *= 2
    y_ref[...] = (jax.nn.gelu(dg_ref[...]) * u).astype(y_ref.dtype)
    nst_ref[0] = x[T - 3:T, :]
    hl_ref[0] = u[T - 1:T, :]


def lru_group(P, state_d, h0, lw, grp, W, ybuf, ybuf_shape):
    row0, B, T = grp["row0"], grp["B"], grp["T"]
    wa, wx = lw["lru_wa"], lw["lru_wx"]
    nblk, bw = wa.shape[0], wa.shape[1]
    assert T >= CONV_D_W - 1 and row0 % T == 0 and bw % LANES == 0 and nblk * bw == W
    rb0 = row0 // T
    cb = (3 * W) // bw
    yb = (3 * W) // bw

    def col(k):
        return pl.BlockSpec((T, bw), lambda b, c, k=k: (rb0 + b, cb + k * nblk + c))

    vec = pl.BlockSpec((1, bw), lambda b, c: (0, c))
    mat = pl.BlockSpec((1, bw, bw), lambda b, c: (c, 0, 0))
    row = lambda a: a.reshape(1, W)
    return _pcall(
        _lru_kernel, name=f"lru_{grp['name']}", grid=(B, nblk),
        in_specs=[col(0), col(1),
                  pl.BlockSpec((1, CONV_D_W - 1, bw), lambda b, c: (b, 0, c)),
                  pl.BlockSpec((1, 1, bw), lambda b, c: (b, 0, c)),
                  pl.BlockSpec((CONV_D_W, bw), lambda b, c: (0, c)),
                  vec, mat, vec, mat, vec, vec],
        inputs=[P, P, state_d, h0.reshape(B, 1, W), lw["conv_d_w"], row(lw["conv_d_b"]), wa,
                row(lw["lru_ba"]), wx, row(lw["lru_bx"]), row(lw["lru_lambda"])],
        out_specs=[pl.BlockSpec((T, bw), lambda b, c: (rb0 + b, yb + c)),
                   pl.BlockSpec((1, CONV_D_W - 1, bw), lambda b, c: (b, 0, c)),
                   pl.BlockSpec((1, 1, bw), lambda b, c: (b, 0, c))],
        out_shape=[ybuf_shape, jax.ShapeDtypeStruct((B, CONV_D_W - 1, W), F32),
                   jax.ShapeDtypeStruct((B, 1, W), F32)],
        carries=[(0, ybuf)])


def _order_key(x):
    bits = pltpu.bitcast(x + 0.0, jnp.int32)
    return jnp.where(bits < 0, bits ^ jnp.int32(0x7FFFFFFF), bits)


def _count(mask, axis):
    ind = jnp.where(mask, 1.0, 0.0)
    if axis == 1:
        return jnp.sum(ind, axis=1, keepdims=True)
    S, tq = ind.shape
    g = 8 if S % 64 == 0 else 1
    part = ind.reshape(g, S // (8 * g), 8, tq).sum(axis=1)
    return part.sum(axis=0).sum(axis=0, keepdims=True)


def _select_topk(key, adm, tri, n_sel, axis):
    shape = tuple(1 if a == axis else s for a, s in enumerate(key.shape))

    def bit_step(b, t):
        cand = t + lax.shift_left(jnp.int32(1), 31 - b)
        return jnp.where(_count(key >= cand, axis) >= float(n_sel), cand, t)

    thr = lax.fori_loop(0, 32, bit_step, jnp.full(shape, INT_MIN, jnp.int32))
    eq = jnp.where(key == thr, 1.0, 0.0).astype(BF16)
    blk = tri.shape[0]
    total = jnp.zeros(shape, F32)
    ranks = []
    for b in range(key.shape[axis] // blk):
        if axis == 0:
            r = jnp.dot(tri, eq[b * blk:(b + 1) * blk, :], preferred_element_type=F32) + total
            total = r[blk - 1:blk, :]
        else:
            r = jnp.dot(eq[:, b * blk:(b + 1) * blk], tri, preferred_element_type=F32) + total
            total = r[:, blk - 1:blk]
        ranks.append(r)
    rank = jnp.concatenate(ranks, axis=axis)
    need = float(n_sel) - _count(key > thr, axis)
    ninf = -jnp.inf
    tie = jnp.where(key == thr, jnp.where(rank <= need, 0.0, ninf), ninf)
    return jnp.where(adm, jnp.where(key > thr, 0.0, tie), ninf)


IDX_SCALE = (IDX_DIM ** -0.5) * (IDX_HEADS ** -0.5)
EXP2_SCALE = (HEAD_DIM ** -0.5) * math.log2(math.e)


def _tri(n, lower):
    r = lax.broadcasted_iota(jnp.int32, (n, n), 0)
    c = lax.broadcasted_iota(jnp.int32, (n, n), 1)
    return jnp.where(r >= c if lower else r <= c, 1.0, 0.0).astype(BF16)


def _head_pair_rows(x):
    lane = lax.broadcasted_iota(jnp.int32, x.shape, 1)
    zero = jnp.zeros_like(x)
    half = x.shape[1] // 2
    return jnp.concatenate([jnp.where(lane < half, x, zero), jnp.where(lane >= half, x, zero)], axis=0)


def _pad_rows(x, n):
    if x.shape[0] == n:
        return x
    return jnp.concatenate([x, jnp.zeros((n - x.shape[0], x.shape[1]), x.dtype)], axis=0)


def _past_head(ref, h, n_heads):
    return ref[pl.ds(h, ref.shape[0] // n_heads, stride=n_heads), :].astype(BF16)


def _dsa_rows_kernel(q_ref, qi_ref, sm_ref, kp_ref, vp_ref, kikip_ref, kn_ref, vn_ref, kikin_ref,
                     tri_ref, o_ref, *, chunk0, n_new, s_valid, n_sel, n_heads):
    c = pl.program_id(1)
    tq = q_ref.shape[0]
    n_past = kp_ref.shape[0] // n_heads
    S = n_past + n_new
    sm = sm_ref[...]
    kn = _pad_rows(kn_ref[...], n_new)
    vn = _pad_rows(vn_ref[...], n_new)
    kikip = kikip_ref[...].astype(BF16)
    kikin = _pad_rows(kikin_ref[...], n_new)
    score = jnp.zeros((tq, S), F32)
    for j in range(IDX_HEADS // 2):
        qq = _head_pair_rows(qi_ref[:, j * LANES:(j + 1) * LANES])
        s2 = jnp.concatenate([lax.dot_general(qq, kikip, NT, preferred_element_type=F32),
                              lax.dot_general(qq, kikin, NT, preferred_element_type=F32)], axis=1)
        for half in range(2):
            h = 2 * j + half
            w = sm[:, IDX_DIM + h:IDX_DIM + h + 1] * IDX_SCALE
            score = score + jnp.maximum(s2[half * tq:(half + 1) * tq, :], 0.0) * w
    kpos = lax.broadcasted_iota(jnp.int32, (tq, S), 1)
    adm = kpos < jnp.minimum((chunk0 + c + 1) * CHUNK, s_valid)
    key = jnp.where(adm, _order_key(score), jnp.int32(INT_MIN))
    bias = _select_topk(key, adm, tri_ref[...], n_sel, 1)

    for h in range(n_heads):
        sl = slice(h * HEAD_DIM, (h + 1) * HEAD_DIM)
        qh = q_ref[:, sl]
        lg = jnp.concatenate(
            [lax.dot_general(qh, _past_head(kp_ref, h, n_heads), NT, preferred_element_type=F32),
             lax.dot_general(qh, kn[:, sl], NT, preferred_element_type=F32)], axis=1) + bias
        p = jnp.exp2((lg - jnp.max(lg, axis=-1, keepdims=True)) * EXP2_SCALE)
        l = jnp.sum(p, axis=-1, keepdims=True)
        pb = p.astype(BF16)
        o = (jnp.dot(pb[:, :n_past], _past_head(vp_ref, h, n_heads), preferred_element_type=F32)
             + jnp.dot(pb[:, n_past:], vn[:, sl], preferred_element_type=F32))
        o_ref[:, sl] = (o / l).astype(o_ref.dtype)


def _dsa_cols_kernel(q_ref, qi_ref, wt_ref, k_ref, vt_ref, kiki_ref, tri_ref, o_ref, *,
                     tile0, n_sel, n_heads):
    i = pl.program_id(1) + tile0
    tq = q_ref.shape[0]
    S = k_ref.shape[1]
    wt = wt_ref[0]
    kiki = kiki_ref[0]
    score = jnp.zeros((S, tq), F32)
    for j in range(IDX_HEADS // 2):
        qq = _head_pair_rows(qi_ref[:, j * LANES:(j + 1) * LANES])
        s2 = lax.dot_general(kiki, qq, NT, preferred_element_type=F32)
        for half in range(2):
            h = 2 * j + half
            w = wt[IDX_DIM + h:IDX_DIM + h + 1, :] * IDX_SCALE
            score = score + jnp.maximum(s2[:, half * tq:(half + 1) * tq], 0.0) * w
    kpos = lax.broadcasted_iota(jnp.int32, (S, tq), 0)
    qpos = i * tq + lax.broadcasted_iota(jnp.int32, (1, tq), 1)
    adm = kpos < (lax.shift_right_logical(qpos, 6) + 1) * CHUNK
    key = jnp.where(adm, _order_key(score), jnp.int32(INT_MIN))
    bias = _select_topk(key, adm, tri_ref[...], n_sel, 0)

    for hp in range(n_heads // 2):
        sl2 = slice(2 * hp * HEAD_DIM, (2 * hp + 2) * HEAD_DIM)
        lg2 = lax.dot_general(k_ref[0, :, sl2], _head_pair_rows(q_ref[:, sl2]), NT,
                              preferred_element_type=F32)
        for half in range(2):
            sl = slice((2 * hp + half) * HEAD_DIM, (2 * hp + half + 1) * HEAD_DIM)
            lg = lg2[:, half * tq:(half + 1) * tq] + bias
            p = jnp.exp2((lg - jnp.max(lg, axis=0, keepdims=True)) * EXP2_SCALE)
            l = jnp.sum(p, axis=0, keepdims=True)
            ot = jnp.dot(vt_ref[0, sl, :], p.astype(BF16), preferred_element_type=F32)
            o_ref[:, sl] = (ot / l).T.astype(o_ref.dtype)


def _tri_block(n):
    return 256 if n % 256 == 0 else LANES


def dsa_rows_group(q, qi, smf, k_new, v_new, kiki_new, k_past, v_past, kiki_past, layer, grp, W,
                   ybuf, ybuf_shape):
    B, T, S, S_pad, pos0, row0 = (grp[k] for k in ("B", "T", "S", "S_pad", "pos0", "row0"))
    assert pos0 % LANES == 0 and T % CHUNK == 0 and pos0 > 0
    nq = T // CHUNK
    blk = _tri_block(S_pad)
    qrow = lambda w: pl.BlockSpec((CHUNK, w), lambda b, c: (b * nq + c, 0))
    new = lambda w: pl.BlockSpec((T, w), lambda b, c: (b, 0))
    kern = functools.partial(_dsa_rows_kernel, chunk0=pos0 // CHUNK, n_new=S_pad - pos0, s_valid=S,
                             n_sel=min(TOPK_MAX, S // 4), n_heads=W // HEAD_DIM)
    return _pcall(
        kern, name=f"dsa_{grp['name']}", grid=(B, nq),
        in_specs=[qrow(W), qrow(IDX_HEADS * IDX_DIM), qrow(LANES),
                  pl.BlockSpec((pos0 * (W // HEAD_DIM), HEAD_DIM), lambda b, c: (layer * B + b, 0)),
                  pl.BlockSpec((pos0 * (W // HEAD_DIM), HEAD_DIM), lambda b, c: (layer * B + b, 0)),
                  pl.BlockSpec((pos0, LANES), lambda b, c: (b, 0)),
                  new(W), new(W), new(LANES),
                  pl.BlockSpec((blk, blk), lambda b, c: (0, 0))],
        inputs=[q, qi, smf, k_past, v_past, kiki_past, k_new, v_new, kiki_new, _tri(blk, False)],
        out_specs=pl.BlockSpec((CHUNK, W), lambda b, c: (row0 // CHUNK + b * nq + c, 1)),
        out_shape=ybuf_shape, carries=[(0, ybuf)])


def dsa_cols_group(q, qi, smT, k_b, vT, kiki, grp, W, ybuf, ybuf_shape):
    B, T, row0 = grp["B"], grp["T"], grp["row0"]
    tq = LANES
    H = W // HEAD_DIM
    assert grp["pos0"] == 0 and T % tq == 0 and row0 % tq == 0 and H % 2 == 0
    nq = T // tq
    kb = max(tq, T // 8)
    tpc = kb // tq
    blk = _tri_block(kb)
    k3, kiki3 = k_b.reshape(B, T, W), kiki.reshape(B, T, LANES)
    for cls in range(nq // tpc):
        s_eff = (cls + 1) * kb
        t0 = cls * tpc
        qrow = lambda w: pl.BlockSpec((tq, w), lambda b, i: (b * nq + t0 + i, 0))
        keys = lambda w: pl.BlockSpec((1, s_eff, w), lambda b, i: (b, 0, 0))
        kern = functools.partial(_dsa_cols_kernel, tile0=t0, n_sel=min(TOPK_MAX, T // 4), n_heads=H)
        ybuf = _pcall(
            kern, name=f"dsa_{grp['name']}_{cls}", grid=(B, tpc),
            in_specs=[qrow(W), qrow(IDX_HEADS * IDX_DIM),
                      pl.BlockSpec((1, LANES, tq), lambda b, i: (b, 0, t0 + i)),
                      keys(W), pl.BlockSpec((1, W, s_eff), lambda b, i: (b, 0, 0)), keys(LANES),
                      pl.BlockSpec((blk, blk), lambda b, i: (0, 0))],
            inputs=[q, qi, smT, k3, vT, kiki3, _tri(blk, True)],
            out_specs=pl.BlockSpec((tq, W), lambda b, i: (row0 // tq + b * nq + t0 + i, 1)),
            out_shape=ybuf_shape, carries=[(0, ybuf)])
    return ybuf


def _cumsum_kernel(x_ref, o_ref):
    x = x_ref[...]
    n = x.shape[0]
    row = lax.broadcasted_iota(jnp.int32, x.shape, 0)
    d = 1
    while d < n:
        x = x + jnp.where(row >= d, pltpu.roll(x, d, 0), 0.0)
        d *= 2
    o_ref[...] = x


def cumsum_rows(x, B, n, name):
    spec = pl.BlockSpec((n, LANES), lambda b: (b, 0))
    return _pcall(_cumsum_kernel, name=name, grid=(B,), in_specs=[spec], inputs=[x], out_specs=spec,
                  out_shape=jax.ShapeDtypeStruct((B * n, LANES), F32))


def _fox_kernel(*refs, pos0, tile0, n_new, s_valid, fcol0, n_heads):
    if pos0:
        q_ref, fq_ref, fk_ref, kp_ref, vp_ref, kn_ref, vn_ref, o_ref = refs
    else:
        q_ref, fq_ref, fk_ref, kn_ref, vn_ref, o_ref = refs
    i = pl.program_id(1) + tile0
    tq = q_ref.shape[0]
    S = pos0 + n_new
    kn = _pad_rows(kn_ref[0], n_new)
    vn = _pad_rows(vn_ref[0], n_new)
    qpos = pos0 + i * tq + lax.broadcasted_iota(jnp.int32, (tq, S), 0)
    kpos = lax.broadcasted_iota(jnp.int32, (tq, S), 1)
    ok = (kpos <= qpos) & (kpos < s_valid)
    fq = fq_ref[...] * (HEAD_DIM ** 0.5)
    fk = fk_ref[0] * (HEAD_DIM ** 0.5)
    for h in range(n_heads):
        sl = slice(h * HEAD_DIM, (h + 1) * HEAD_DIM)
        qh = q_ref[:, sl]
        lg = lax.dot_general(qh, kn[:, sl], NT, preferred_element_type=F32)
        if pos0:
            lg = jnp.concatenate(
                [lax.dot_general(qh, _past_head(kp_ref, h, n_heads), NT, preferred_element_type=F32),
                 lg], axis=1)
        lg = lg + fq[:, fcol0 + h:fcol0 + h + 1] - fk[h:h + 1, :]
        lg = jnp.where(ok, lg, -jnp.inf)
        p = jnp.exp2((lg - jnp.max(lg, axis=-1, keepdims=True)) * EXP2_SCALE)
        l = jnp.sum(p, axis=-1, keepdims=True)
        pb = p.astype(BF16)
        o = jnp.dot(pb[:, pos0:], vn[:, sl], preferred_element_type=F32)
        if pos0:
            o = o + jnp.dot(pb[:, :pos0], _past_head(vp_ref, h, n_heads), preferred_element_type=F32)
        o_ref[:, sl] = (o / l).astype(o_ref.dtype)


def fox_group(q, Fq, Fk_t, k_new, v_new, k_past, v_past, layer, grp, W, fcol0, ybuf, ybuf_shape):
    B, T, S, S_pad, pos0, row0 = (grp[k] for k in ("B", "T", "S", "S_pad", "pos0", "row0"))
    H = W // HEAD_DIM
    assert pos0 % LANES == 0
    tq = _pick(T, (256, 128, 64))
    nq = T // tq
    tpc = max(1, nq // 4) if pos0 == 0 else nq
    k3, v3 = k_new.reshape(B, T, W), v_new.reshape(B, T, W)
    past = pl.BlockSpec((pos0 * H, HEAD_DIM), lambda b, i: (layer * B + b, 0))
    for cls in range(nq // tpc):
        t0 = cls * tpc
        n_keys = min(T, (t0 + tpc) * tq)
        n_new = -(-n_keys // LANES) * LANES
        qrow = lambda w: pl.BlockSpec((tq, w), lambda b, i: (b * nq + t0 + i, 0))
        new = pl.BlockSpec((1, n_keys, W), lambda b, i: (b, 0, 0))
        kern = functools.partial(_fox_kernel, pos0=pos0, tile0=t0, n_new=n_new, s_valid=S,
                                 fcol0=fcol0, n_heads=H)
        ybuf = _pcall(
            kern, name=f"fox_{grp['name']}_{cls}", grid=(B, tpc),
            in_specs=[qrow(W), qrow(LANES), pl.BlockSpec((1, H, pos0 + n_new), lambda b, i: (b, 0, 0))]
            + ([past, past] if pos0 else []) + [new, new],
            inputs=[q, Fq, Fk_t] + ([k_past, v_past] if pos0 else []) + [k3, v3],
            out_specs=pl.BlockSpec((tq, W), lambda b, i: (row0 // tq + b * nq + t0 + i, 2)),
            out_shape=ybuf_shape, carries=[(0, ybuf)])
    return ybuf


def _pack_w_tail(w_in_t, W, H_C):
    sizes = (W, W, W, W, W, W, IDX_HEADS * IDX_DIM, IDX_DIM, IDX_HEADS, W, W, W, H_C, W, W)
    offs = [0]
    for sz in sizes:
        offs.append(offs[-1] + sz)
    seg = lambda k: w_in_t[:, offs[k]:offs[k + 1], :]
    L, _, D = w_in_t.shape
    pad = jnp.zeros((L, SMALL_W - IDX_DIM - IDX_HEADS - H_C, D), w_in_t.dtype)
    return jnp.concatenate([seg(k) for k in (9, 10, 11, 13, 14, 7, 8, 12)] + [pad], axis=1)


def _token_mix(xb, lw, groups, pasts, layer, depth, state_bufs):
    W = lw["W"]
    H = W // HEAD_DIM
    M = xb.shape[0]
    fcol0 = IDX_DIM + IDX_HEADS
    tms = (1536, 768, 384, 128, 64)
    P1 = matmul(xb, lw["w_in_t"], layer, 7 * W, F32, "in_proj_a", tms, 512, w_transposed=True)
    P2 = matmul(xb, lw["w_tail_t"], layer, 5 * W + SMALL_W, F32, "in_proj_b", tms, 512,
                w_transposed=True)
    bf_row = jnp.zeros((1, LANES), F32).at[0, fcol0:fcol0 + H].set(lw["fox_b_f"])
    ybuf_shape = jax.ShapeDtypeStruct((M, 4 * W), BF16)
    ybuf = None
    small_states, new_bufs = [], []
    for gi, (grp, past) in enumerate(zip(groups, pasts)):
        B, T, S, S_pad = grp["B"], grp["T"], grp["S"], grp["S_pad"]
        pk_b, pv_b, pki_b, pk_c, pv_c, plf_c, st_a, st_d, h0 = past
        npast = S - T
        (q_b, k_b, v_x, qi_b, sm_x, kiki, cq_b, ck_b, cv_b, lf), bufs = prep_group(
            P1, P2, bf_row, grp, W, layer, depth, None if state_bufs is None else state_bufs[gi])
        new_bufs.append(bufs)

        ybuf, nst_a = conv_a_group(P1, st_a, lw["conv_a_w"], grp, W, ybuf, ybuf_shape)
        ybuf, nst_d, h_last = lru_group(P2, st_d, h0, lw, grp, W, ybuf, ybuf_shape)

        lf3 = lf.reshape(B, T, LANES)
        if grp["transposed"]:
            ybuf = dsa_cols_group(q_b, qi_b, sm_x, k_b, v_x, kiki, grp, W, ybuf, ybuf_shape)
            lf_all = lf3
        else:
            pki = pki_b[layer]
            kiki_past = jnp.concatenate([pki, pki], axis=-1).reshape(B * npast, LANES)
            ybuf = dsa_rows_group(q_b, qi_b, sm_x, k_b, v_x, kiki, pk_b, pv_b, kiki_past, layer, grp,
                                  W, ybuf, ybuf_shape)
            plf = jnp.pad(plf_c[layer].astype(F32), ((0, 0), (0, 0), (fcol0, LANES - fcol0 - H)))
            lf_all = jnp.concatenate([plf, lf3, jnp.zeros((B, S_pad - S, LANES), F32)], axis=1)

        F_all = cumsum_rows(lf_all.reshape(B * S_pad, LANES), B, S_pad, f"cumsum_{grp['name']}")
        F3 = F_all.reshape(B, S_pad, LANES)
        Fq = F3[:, npast:npast + T, :].reshape(B * T, LANES)
        Fk_t = jnp.swapaxes(F3[:, :, fcol0:fcol0 + H], 1, 2)
        ybuf = fox_group(cq_b, Fq, Fk_t, ck_b, cv_b, pk_c, pv_c, layer, grp, W, fcol0, ybuf, ybuf_shape)

        small_states.append((lf3[:, :, :IDX_DIM], lf3[:, :, fcol0:fcol0 + H],
                             nst_a, nst_d, h_last.reshape(B, W)))
    y = matmul(ybuf, lw["w_out"], layer, 4 * W, BF16, "out_proj", tms, 512)
    return y, small_states, new_bufs


def kernel(x_prompt, x_sample, cache_dsa_k, cache_dsa_v, cache_dsa_kidx, cache_fox_k, cache_fox_v,
           cache_fox_logf, state_conv_a, state_conv_d, state_lru, ln_g, ln_b, ffn_w13, ffn_w2, w_in,
           fox_b_f, conv_a_w, conv_d_w, conv_d_b, lru_wa, lru_ba, lru_wx, lru_bx, lru_lambda, w_out):
    Bp, Tp, D = x_prompt.shape
    Bs, Ts, _ = x_sample.shape
    depth = ln_g.shape[0]
    past_len = cache_dsa_k.shape[2]
    W = D // 4
    H = W // HEAD_DIM
    assert IDX_HEADS * IDX_DIM == W and W % HEAD_DIM == 0
    alpha = (2.0 * depth) ** 0.25
    dt = x_prompt.dtype

    def group(name, row0, B, T, pos0):
        S = pos0 + T
        S_pad = -(-S // LANES) * LANES
        pos = pos0 + jnp.arange(T, dtype=jnp.int32)
        tabs = jnp.stack(_rope_tables(pos, HEAD_DIM, ROPE_DIM // 2)
                         + _rope_tables(pos, IDX_DIM, IDX_ROPE_DIM // 2)
                         + _rope_tables(pos, LANES, IDX_ROPE_DIM // 2))
        return dict(name=name, row0=row0, B=B, T=T, pos0=pos0, S=S, S_pad=S_pad, tabs=tabs,
                    transposed=(pos0 == 0 and T % LANES == 0))

    groups = [group("prompt", 0, Bp, Tp, 0), group("sample", Bp * Tp, Bs, Ts, past_len)]
    Rp, Rs = Bp * Tp, Bs * Ts

    x = (x_prompt.reshape(Rp, D), x_sample.reshape(Rs, D))
    xb = cast_rows(x[0], x[1], BF16, "cast_x")

    w13 = ffn_w13.reshape((2 * depth,) + ffn_w13.shape[2:])
    w2 = ffn_w2.reshape((2 * depth,) + ffn_w2.shape[2:])
    w_in_t = jnp.swapaxes(w_in, 1, 2)
    w_tail_t = _pack_w_tail(w_in_t, W, H)
    rows2d = lambda c: c.reshape(-1, HEAD_DIM)
    cache_rows = (rows2d(cache_dsa_k), rows2d(cache_dsa_v), cache_dsa_kidx, rows2d(cache_fox_k),
                  rows2d(cache_fox_v), cache_fox_logf)
    down_tms = (768, 384, 128, 64)

    small = [[], []]
    state_bufs = None
    for l in range(depth):
        lw = dict(W=W, w_in_t=w_in_t, w_tail_t=w_tail_t, fox_b_f=fox_b_f[l], conv_a_w=conv_a_w[l],
                  conv_d_w=conv_d_w[l], conv_d_b=conv_d_b[l], lru_wa=lru_wa[l], lru_ba=lru_ba[l],
                  lru_wx=lru_wx[l], lru_bx=lru_bx[l], lru_lambda=lru_lambda[l], w_out=w_out)
        empty = (None,) * 6 + (jnp.zeros((Bp, CONV_A_W - 1, W), dt),
                               jnp.zeros((Bp, CONV_D_W - 1, W), dt), jnp.zeros((Bp, W), dt))
        sample_past = cache_rows + (state_conv_a[l], state_conv_d[l], state_lru[l])

        h = swiglu_up(xb, w13, 2 * l, "ffn_up")
        f = matmul(h, w2, 2 * l, D, BF16, "ffn_down", down_tms, 512, single_buffer_w=True)
        x, xb = residual_layer_norm(x, f, ln_g[l, 0], ln_b[l, 0], alpha, 0.5, "res_ln")

        y, small_states, state_bufs = _token_mix(xb, lw, groups, [empty, sample_past], l, depth,
                                                 state_bufs)
        x, xb = residual_layer_norm(x, y, ln_g[l, 1], ln_b[l, 1], alpha, 1.0, "res_ln")

        h = swiglu_up(xb, w13, 2 * l + 1, "ffn_up")
        f = matmul(h, w2, 2 * l + 1, D, BF16, "ffn_down", down_tms, 512, single_buffer_w=True)
        if l + 1 < depth:
            x, xb = residual_layer_norm(x, f, ln_g[l, 2], ln_b[l, 2], alpha, 0.5, "res_ln")
        else:
            outs = [residual_layer_norm(x, f, ln_g[l, 2], ln_b[l, 2], alpha, 0.5, "res_ln_out",
                                        rows=r)[0] for r in ((0, Rp), (Rp, Rs))]
        for g in range(2):
            small[g].append(small_states[g])

    res = [outs[0].reshape(Bp, Tp, D), outs[1].reshape(Bs, Ts, D)]
    for g, (B, T) in enumerate(((Bp, Tp), (Bs, Ts))):
        ks, vs, cks, cvs = (a.reshape(depth, B, T, H, HEAD_DIM) for a in state_bufs[g])
        kidx, logf, st_a, st_d, h_last = (jnp.stack([s[i] for s in small[g]]) for i in range(5))
        res += [ks, vs, kidx, cks, cvs, logf, st_a, st_d, h_last]
    return tuple(res)
```

```python
import functools
import math

import jax
import jax.numpy as jnp
from jax import lax
from jax.experimental import pallas as pl
from jax.experimental.pallas import tpu as pltpu

F32 = jnp.float32
BF16 = jnp.bfloat16

CHUNK = 64
HEAD_DIM = 128
ROPE_DIM = HEAD_DIM // 4
ROPE_THETA = 500000.0
IDX_HEADS = 16
IDX_DIM = 64
IDX_ROPE_DIM = IDX_DIM // 4
TOPK_MAX = 256
CONV_A_W = 3
CONV_D_W = 4
LRU_C = 8.0
LN_EPS = 1e-5
LANES = 128
VMEM_LIMIT = 58 * 1024 * 1024
INT_MIN = -2 ** 31
NT = (((1,), (1,)), ((), ()))


def _pcall(body, *, name, grid, in_specs, inputs, out_specs, out_shape, carries=(), scratch=()):
    n_in, n_c = len(inputs), len(carries)

    def kern(*refs):
        body(*refs[:n_in], *refs[n_in + n_c:])

    return pl.pallas_call(
        kern, name=name, grid=grid,
        in_specs=list(in_specs) + [pl.BlockSpec(memory_space=pl.ANY)] * n_c,
        out_specs=out_specs, out_shape=out_shape,
        input_output_aliases={n_in + k: oi for k, (oi, _) in enumerate(carries)},
        scratch_shapes=list(scratch),
        compiler_params=pltpu.CompilerParams(dimension_semantics=("arbitrary",) * len(grid),
                                             vmem_limit_bytes=VMEM_LIMIT),
    )(*inputs, *[a for _, a in carries])


def _pick(n, cands):
    for c in cands:
        if n % c == 0:
            return c
    raise ValueError(f"no tile for {n} in {cands}")


def _mm_kernel(x_ref, w_ref, o_ref, wb_ref, *, w_transposed):
    @pl.when(pl.program_id(1) == 0)
    def _():
        w = w_ref[0]
        wb_ref[...] = (w.T if w_transposed else w).astype(BF16)

    o_ref[...] = jnp.dot(x_ref[...], wb_ref[...], preferred_element_type=F32).astype(o_ref.dtype)


def matmul(x, w, li, n_cols, out_dtype, name, tms, tn, single_buffer_w=False, w_transposed=False):
    M, K = x.shape
    tm = _pick(M, tms)
    assert n_cols % tn == 0 and w.shape[2 if w_transposed else 1] == K
    mode = dict(pipeline_mode=pl.Buffered(1)) if single_buffer_w else {}
    if w_transposed:
        w_spec = pl.BlockSpec((1, tn, K), lambda j, i: (li, j, 0), **mode)
    else:
        w_spec = pl.BlockSpec((1, K, tn), lambda j, i: (li, 0, j), **mode)
    return _pcall(
        functools.partial(_mm_kernel, w_transposed=w_transposed), name=name,
        grid=(n_cols // tn, M // tm),
        in_specs=[pl.BlockSpec((tm, K), lambda j, i: (i, 0)), w_spec],
        inputs=[x, w],
        out_specs=pl.BlockSpec((tm, tn), lambda j, i: (i, j)),
        out_shape=jax.ShapeDtypeStruct((M, n_cols), out_dtype),
        scratch=[pltpu.VMEM((K, tn), BF16)])


def _swiglu_kernel(x_ref, wg_ref, wu_ref, o_ref, wgb_ref, wub_ref):
    @pl.when(pl.program_id(1) == 0)
    def _():
        wgb_ref[...] = wg_ref[0].astype(BF16)
        wub_ref[...] = wu_ref[0].astype(BF16)

    x = x_ref[...]
    g = jnp.dot(x, wgb_ref[...], preferred_element_type=F32)
    u = jnp.dot(x, wub_ref[...], preferred_element_type=F32)
    o_ref[...] = (g * jax.nn.sigmoid(g) * u).astype(o_ref.dtype)


def swiglu_up(x, w13, li, name):
    M, K = x.shape
    F = w13.shape[2] // 2
    tm = _pick(M, (1536, 768, 384, 128, 64))
    tn = _pick(F, (256, 128))
    nb = F // tn
    return _pcall(
        _swiglu_kernel, name=name, grid=(nb, M // tm),
        in_specs=[pl.BlockSpec((tm, K), lambda j, i: (i, 0)),
                  pl.BlockSpec((1, K, tn), lambda j, i: (li, 0, j)),
                  pl.BlockSpec((1, K, tn), lambda j, i: (li, 0, j + nb))],
        inputs=[x, w13, w13],
        out_specs=pl.BlockSpec((tm, tn), lambda j, i: (i, j)),
        out_shape=jax.ShapeDtypeStruct((M, F), BF16),
        scratch=[pltpu.VMEM((K, tn), BF16), pltpu.VMEM((K, tn), BF16)])


def _split_rows_specs(tm, D, n_first):
    return [pl.BlockSpec((tm, D), lambda i: (jnp.minimum(i, n_first - 1), 0)),
            pl.BlockSpec((tm, D), lambda i: (jnp.maximum(i - n_first, 0), 0))]


def _cast_split_kernel(a_ref, b_ref, o_ref, *, n_first):
    o_ref[...] = jnp.where(pl.program_id(0) < n_first, a_ref[...], b_ref[...]).astype(o_ref.dtype)


def cast_rows(a, b, dtype, name):
    D = a.shape[1]
    tm = _pick(b.shape[0], (256, 128, 64))
    assert a.shape[0] % tm == 0
    n = (a.shape[0] + b.shape[0]) // tm
    return _pcall(
        functools.partial(_cast_split_kernel, n_first=a.shape[0] // tm), name=name, grid=(n,),
        in_specs=_split_rows_specs(tm, D, a.shape[0] // tm), inputs=[a, b],
        out_specs=pl.BlockSpec((tm, D), lambda i: (i, 0)),
        out_shape=jax.ShapeDtypeStruct((n * tm, D), dtype))


def _res_ln_kernel(*refs, alpha, yscale, n_first):
    if n_first is None:
        x_ref, y_ref, g_ref, b_ref, *o_refs = refs
        x = x_ref[...]
    else:
        xa_ref, xb_ref, y_ref, g_ref, b_ref, *o_refs = refs
        x = jnp.where(pl.program_id(0) < n_first, xa_ref[...], xb_ref[...])
    z = alpha * x + yscale * y_ref[...].astype(F32)
    mu = jnp.mean(z, axis=-1, keepdims=True)
    zc = z - mu
    var = jnp.mean(zc * zc, axis=-1, keepdims=True)
    o = zc * lax.rsqrt(var + LN_EPS) * g_ref[...] + b_ref[...]
    o_refs[0][...] = o
    if len(o_refs) > 1:
        o_refs[1][...] = o.astype(BF16)


def residual_layer_norm(x, y, g, b, alpha, yscale, name, rows=None):
    M, D = y.shape
    row0, n = rows if rows is not None else (0, M)
    tm = _pick(n if not isinstance(x, tuple) else x[1].shape[0], (256, 128, 64))
    assert row0 % tm == 0
    rb0 = row0 // tm
    src = pl.BlockSpec((tm, D), lambda i: (rb0 + i, 0))
    dst = pl.BlockSpec((tm, D), lambda i: (i, 0))
    vec = pl.BlockSpec((1, D), lambda i: (0, 0))
    shapes = [jax.ShapeDtypeStruct((n, D), F32)]
    if rows is None:
        shapes.append(jax.ShapeDtypeStruct((n, D), BF16))
    if isinstance(x, tuple):
        assert rows is None and x[0].shape[0] % tm == 0
        n_first = x[0].shape[0] // tm
        x_specs, xs = _split_rows_specs(tm, D, n_first), list(x)
    else:
        n_first, x_specs, xs = None, [src], [x]
    return _pcall(
        functools.partial(_res_ln_kernel, alpha=alpha, yscale=yscale, n_first=n_first), name=name,
        grid=(n // tm,), in_specs=x_specs + [src, vec, vec],
        inputs=xs + [y, g.reshape(1, D), b.reshape(1, D)],
        out_specs=[dst] * len(shapes), out_shape=shapes)


def _rope_tables(pos, period, half):
    inv = ROPE_THETA ** (-jnp.arange(half, dtype=F32) / half)
    ang = pos.astype(F32)[:, None] * inv[None, :]
    cos, sin = jnp.cos(ang), jnp.sin(ang)
    T = pos.shape[0]
    reps = LANES // period
    zeros = jnp.zeros((T, period - 2 * half), F32)
    c = jnp.concatenate([cos, cos, jnp.ones((T, period - 2 * half), F32)], axis=1)
    s1 = jnp.concatenate([-sin, jnp.zeros((T, half), F32), zeros], axis=1)
    s2 = jnp.concatenate([jnp.zeros((T, half), F32), sin, zeros], axis=1)
    return [jnp.tile(t, (1, reps)) for t in (c, s1, s2)]


def _rope(x, c, s1, s2, half):
    outs = []
    for g in range(x.shape[-1] // LANES):
        xg = x[:, g * LANES:(g + 1) * LANES]
        outs.append(xg * c + pltpu.roll(xg, LANES - half, 1) * s1 + pltpu.roll(xg, half, 1) * s2)
    return outs[0] if len(outs) == 1 else jnp.concatenate(outs, axis=1)


def _store_heads(o_ref, x):
    n_heads = x.shape[1] // HEAD_DIM
    for h in range(n_heads):
        o_ref[pl.ds(h, x.shape[0], stride=n_heads), :] = x[:, h * HEAD_DIM:(h + 1) * HEAD_DIM]


def _prep_kernel(bq_ref, bk_ref, bv_ref, bqi_ref, cq_ref, ck_ref, cv_ref, sm_ref, tab_ref, bf_ref,
                 q_o, kb_o, v_o, qi_o, sm_o, kiki_o, cq_o, ck_o, cv_o, lf_o,
                 ks_o, vs_o, cks_o, cvs_o, *, transposed):
    t = tab_ref[...]
    q_o[...] = _rope(bq_ref[...], t[0], t[1], t[2], ROPE_DIM // 2).astype(BF16)
    k = _rope(bk_ref[...], t[0], t[1], t[2], ROPE_DIM // 2)
    _store_heads(ks_o, k)
    kb_o[...] = k.astype(BF16)
    v = bv_ref[...]
    _store_heads(vs_o, v)
    qi_o[...] = _rope(bqi_ref[...], t[3], t[4], t[5], IDX_ROPE_DIM // 2).astype(BF16)
    sm = _rope(sm_ref[...], t[6], t[7], t[8], IDX_ROPE_DIM // 2)
    if transposed:
        v_o[0] = v.T.astype(BF16)
        sm_o[0] = sm.T
    else:
        v_o[...] = v.astype(BF16)
        sm_o[...] = sm
    lane = lax.broadcasted_iota(jnp.int32, sm.shape, 1)
    ki_lo = jnp.where(lane < IDX_DIM, sm, 0.0)
    kiki_o[...] = (ki_lo + pltpu.roll(ki_lo, IDX_DIM, 1)).astype(BF16)
    cq_o[...] = cq_ref[...].astype(BF16)
    ck = ck_ref[...]
    _store_heads(cks_o, ck)
    ck_o[...] = ck.astype(BF16)
    cv = cv_ref[...]
    _store_heads(cvs_o, cv)
    cv_o[...] = cv.astype(BF16)
    z = -(sm + bf_ref[...])
    lf_o[...] = jnp.where(lane < IDX_DIM, sm, -(jnp.maximum(z, 0.0) + jnp.log1p(jnp.exp(-jnp.abs(z)))))


def prep_group(P1, P2, P3, bf_row, grp, W, layer, depth, state_bufs):
    row0, B, T = grp["row0"], grp["B"], grp["T"]
    tr = grp["transposed"]
    tt = _pick(T, (256, 128, 64))
    nt = T // tt
    rb0 = row0 // tt
    R = B * T

    def col(cb, width):
        return pl.BlockSpec((tt, width), lambda b, i, cb=cb: (rb0 + b * nt + i, cb))

    rows = lambda w: pl.BlockSpec((tt, w), lambda b, i: (b * nt + i, 0))
    n_heads = W // HEAD_DIM
    srows = pl.BlockSpec((tt * n_heads, HEAD_DIM), lambda b, i: (layer * (R // tt) + b * nt + i, 0))
    bf = lambda w: (rows(w), jax.ShapeDtypeStruct((R, w), BF16))
    if tr:
        v_out = (pl.BlockSpec((1, W, tt), lambda b, i: (b, 0, i)), jax.ShapeDtypeStruct((B, W, T), BF16))
        sm_out = (pl.BlockSpec((1, LANES, tt), lambda b, i: (b, 0, i)),
                  jax.ShapeDtypeStruct((B, LANES, T), F32))
    else:
        v_out = bf(W)
        sm_out = (rows(LANES), jax.ShapeDtypeStruct((R, LANES), F32))
    state = (srows, jax.ShapeDtypeStruct((depth * R * n_heads, HEAD_DIM), F32))
    outs = [bf(W), bf(W), v_out, bf(IDX_HEADS * IDX_DIM), sm_out, bf(LANES),
            bf(W), bf(W), bf(W), (rows(LANES), jax.ShapeDtypeStruct((R, LANES), F32)),
            state, state, state, state]
    carries = [] if state_bufs is None else [(10 + k, a) for k, a in enumerate(state_bufs)]
    res = _pcall(
        functools.partial(_prep_kernel, transposed=tr), name=f"prep_{grp['name']}", grid=(B, nt),
        in_specs=[col(3, W), col(4, W), col(5, W), col(6, W), col(0, W), col(1, W), col(2, W),
                  col(0, LANES),
                  pl.BlockSpec((9, tt, LANES), lambda b, i: (0, i, 0)),
                  pl.BlockSpec((1, LANES), lambda b, i: (0, 0))],
        inputs=[P1] * 4 + [P2] * 3 + [P3, grp["tabs"], bf_row],
        out_specs=[s for s, _ in outs], out_shape=[s for _, s in outs], carries=carries)
    return res[:10], list(res[10:])


def _shift_rows(x, d, fill_rows):
    row = lax.broadcasted_iota(jnp.int32, x.shape, 0)
    y = pltpu.roll(x, d, 0)
    for t in range(d):
        y = jnp.where(row == t, fill_rows[t:t + 1, :], y)
    return y


def _conv_a_kernel(ah_ref, ab_ref, ac_ref, st_ref, w_ref, y_ref, nst_ref):
    z = ac_ref[...] * ah_ref[...]
    st = st_ref[0]
    w = w_ref[...]
    T = z.shape[0]
    z1 = _shift_rows(z, 1, st[1:2, :])
    z2 = _shift_rows(z, 2, st)
    conv = z2 * w[0:1, :] + z1 * w[1:2, :] + z * w[2:3, :]
    y_ref[...] = (ab_ref[...] * conv).astype(y_ref.dtype)
    nst_ref[0] = z[T - 2:T, :]


def conv_a_group(P, state_a, conv_w, grp, W, ybuf, ybuf_shape):
    row0, B, T = grp["row0"], grp["B"], grp["T"]
    assert T >= CONV_A_W - 1 and row0 % T == 0
    tc = _pick(W, (256, 128))
    nc = W // tc
    rb0 = row0 // T

    def col(k):
        return pl.BlockSpec((T, tc), lambda b, c, k=k: (rb0 + b, k * nc + c))

    return _pcall(
        _conv_a_kernel, name=f"conv_a_{grp['name']}", grid=(B, nc),
        in_specs=[col(0), col(1), col(2),
                  pl.BlockSpec((1, CONV_A_W - 1, tc), lambda b, c: (b, 0, c)),
                  pl.BlockSpec((CONV_A_W, tc), lambda b, c: (0, c))],
        inputs=[P, P, P, state_a, conv_w],
        out_specs=[pl.BlockSpec((T, tc), lambda b, c: (rb0 + b, c)),
                   pl.BlockSpec((1, CONV_A_W - 1, tc), lambda b, c: (b, 0, c))],
        out_shape=[ybuf_shape, jax.ShapeDtypeStruct((B, CONV_A_W - 1, W), F32)],
        carries=[] if ybuf is None else [(0, ybuf)])


def _lru_kernel(dx_ref, dg_ref, st_ref, h0_ref, cw_ref, cb_ref, wa_ref, ba_ref, wx_ref, bx_ref,
                lam_ref, y_ref, nst_ref, hl_ref):
    x = dx_ref[...]
    T = x.shape[0]
    st = st_ref[0]
    w = cw_ref[...]
    x1 = _shift_rows(x, 1, st[2:3, :])
    x2 = _shift_rows(x, 2, st[1:3, :])
    x3 = _shift_rows(x, 3, st)
    xc = x3 * w[0:1, :] + x2 * w[1:2, :] + x1 * w[2:3, :] + x * w[3:4, :] + cb_ref[...]
    xb = xc.astype(BF16)
    r = jax.nn.sigmoid(jnp.dot(xb, wa_ref[0].astype(BF16), preferred_element_type=F32) + ba_ref[...])
    i = jax.nn.sigmoid(jnp.dot(xb, wx_ref[0].astype(BF16), preferred_element_type=F32) + bx_ref[...])
    nl = -lam_ref[...]
    sp = jnp.maximum(nl, 0.0) + jnp.log1p(jnp.exp(-jnp.abs(nl)))
    log_a = -LRU_C * r * sp
    a = jnp.exp(log_a)
    u = jnp.sqrt(-jnp.tanh(log_a) * (a * a + 1.0)) * i * xc
    row = lax.broadcasted_iota(jnp.int32, x.shape, 0)
    u = jnp.where(row == 0, u + a * h0_ref[0], u)
    d = 1
    while d < T:
        keep = row >= d
        a_prev = jnp.where(keep, pltpu.roll(a, d, 0), 1.0)
        u_prev = jnp.where(keep, pltpu.roll(u, d, 0), 0.0)
        u = a * u_prev + u
        a = a * a_prev
        d *= 2
    y_ref[...] = (jax.nn.gelu(dg_ref[...]) * u).astype(y_ref.dtype)
    nst_ref[0] = x[T - 3:T, :]
    hl_ref[0] = u[T - 1:T, :]


def lru_group(P, state_d, h0, lw, grp, W, ybuf, ybuf_shape):
    row0, B, T = grp["row0"], grp["B"], grp["T"]
    wa, wx = lw["lru_wa"], lw["lru_wx"]
    nblk, bw = wa.shape[0], wa.shape[1]
    assert T >= CONV_D_W - 1 and row0 % T == 0 and bw % LANES == 0 and nblk * bw == W
    rb0 = row0 // T
    cb = (3 * W) // bw
    yb = (3 * W) // bw

    def col(k):
        return pl.BlockSpec((T, bw), lambda b, c, k=k: (rb0 + b, cb + k * nblk + c))

    vec = pl.BlockSpec((1, bw), lambda b, c: (0, c))
    mat = pl.BlockSpec((1, bw, bw), lambda b, c: (c, 0, 0))
    row = lambda a: a.reshape(1, W)
    return _pcall(
        _lru_kernel, name=f"lru_{grp['name']}", grid=(B, nblk),
        in_specs=[col(0), col(1),
                  pl.BlockSpec((1, CONV_D_W - 1, bw), lambda b, c: (b, 0, c)),
                  pl.BlockSpec((1, 1, bw), lambda b, c: (b, 0, c)),
                  pl.BlockSpec((CONV_D_W, bw), lambda b, c: (0, c)),
                  vec, mat, vec, mat, vec, vec],
        inputs=[P, P, state_d, h0.reshape(B, 1, W), lw["conv_d_w"], row(lw["conv_d_b"]), wa,
                row(lw["lru_ba"]), wx, row(lw["lru_bx"]), row(lw["lru_lambda"])],
        out_specs=[pl.BlockSpec((T, bw), lambda b, c: (rb0 + b, yb + c)),
                   pl.BlockSpec((1, CONV_D_W - 1, bw), lambda b, c: (b, 0, c)),
                   pl.BlockSpec((1, 1, bw), lambda b, c: (b, 0, c))],
        out_shape=[ybuf_shape, jax.ShapeDtypeStruct((B, CONV_D_W - 1, W), F32),
                   jax.ShapeDtypeStruct((B, 1, W), F32)],
        carries=[(0, ybuf)])


def _order_key(x):
    bits = pltpu.bitcast(x + 0.0, jnp.int32)
    return jnp.where(bits < 0, bits ^ jnp.int32(0x7FFFFFFF), bits)


def _count(mask, axis):
    ind = jnp.where(mask, 1.0, 0.0)
    if axis == 1:
        return jnp.sum(ind, axis=1, keepdims=True)
    S, tq = ind.shape
    g = 8 if S % 64 == 0 else 1
    part = ind.reshape(g, S // (8 * g), 8, tq).sum(axis=1)
    return part.sum(axis=0).sum(axis=0, keepdims=True)


def _select_topk(key, adm, tri, n_sel, axis):
    shape = tuple(1 if a == axis else s for a, s in enumerate(key.shape))

    def bit_step(b, t):
        cand = t + lax.shift_left(jnp.int32(1), 31 - b)
        return jnp.where(_count(key >= cand, axis) >= float(n_sel), cand, t)

    thr = lax.fori_loop(0, 32, bit_step, jnp.full(shape, INT_MIN, jnp.int32))
    eq = jnp.where(key == thr, 1.0, 0.0).astype(BF16)
    blk = tri.shape[0]
    total = jnp.zeros(shape, F32)
    ranks = []
    for b in range(key.shape[axis] // blk):
        if axis == 0:
            r = jnp.dot(tri, eq[b * blk:(b + 1) * blk, :], preferred_element_type=F32) + total
            total = r[blk - 1:blk, :]
        else:
            r = jnp.dot(eq[:, b * blk:(b + 1) * blk], tri, preferred_element_type=F32) + total
            total = r[:, blk - 1:blk]
        ranks.append(r)
    rank = jnp.concatenate(ranks, axis=axis)
    need = float(n_sel) - _count(key > thr, axis)
    ninf = -jnp.inf
    tie = jnp.where(key == thr, jnp.where(rank <= need, 0.0, ninf), ninf)
    return jnp.where(adm, jnp.where(key > thr, 0.0, tie), ninf)


IDX_SCALE = (IDX_DIM ** -0.5) * (IDX_HEADS ** -0.5)
EXP2_SCALE = (HEAD_DIM ** -0.5) * math.log2(math.e)


def _tri(n, lower):
    r = lax.broadcasted_iota(jnp.int32, (n, n), 0)
    c = lax.broadcasted_iota(jnp.int32, (n, n), 1)
    return jnp.where(r >= c if lower else r <= c, 1.0, 0.0).astype(BF16)


def _head_pair_rows(x):
    lane = lax.broadcasted_iota(jnp.int32, x.shape, 1)
    zero = jnp.zeros_like(x)
    half = x.shape[1] // 2
    return jnp.concatenate([jnp.where(lane < half, x, zero), jnp.where(lane >= half, x, zero)], axis=0)


def _pad_rows(x, n):
    if x.shape[0] == n:
        return x
    return jnp.concatenate([x, jnp.zeros((n - x.shape[0], x.shape[1]), x.dtype)], axis=0)


def _past_head(ref, h, n_heads):
    return ref[pl.ds(h, ref.shape[0] // n_heads, stride=n_heads), :].astype(BF16)


def _dsa_rows_kernel(q_ref, qi_ref, sm_ref, kp_ref, vp_ref, kikip_ref, kn_ref, vn_ref, kikin_ref,
                     tri_ref, o_ref, *, chunk0, n_new, s_valid, n_sel, n_heads):
    c = pl.program_id(1)
    tq = q_ref.shape[0]
    n_past = kp_ref.shape[0] // n_heads
    S = n_past + n_new
    sm = sm_ref[...]
    kn = _pad_rows(kn_ref[...], n_new)
    vn = _pad_rows(vn_ref[...], n_new)
    kikip = kikip_ref[...].astype(BF16)
    kikin = _pad_rows(kikin_ref[...], n_new)
    score = jnp.zeros((tq, S), F32)
    for j in range(IDX_HEADS // 2):
        qq = _head_pair_rows(qi_ref[:, j * LANES:(j + 1) * LANES])
        s2 = jnp.concatenate([lax.dot_general(qq, kikip, NT, preferred_element_type=F32),
                              lax.dot_general(qq, kikin, NT, preferred_element_type=F32)], axis=1)
        for half in range(2):
            h = 2 * j + half
            w = sm[:, IDX_DIM + h:IDX_DIM + h + 1] * IDX_SCALE
            score = score + jnp.maximum(s2[half * tq:(half + 1) * tq, :], 0.0) * w
    kpos = lax.broadcasted_iota(jnp.int32, (tq, S), 1)
    adm = kpos < jnp.minimum((chunk0 + c + 1) * CHUNK, s_valid)
    key = jnp.where(adm, _order_key(score), jnp.int32(INT_MIN))
    bias = _select_topk(key, adm, tri_ref[...], n_sel, 1)

    for h in range(n_heads):
        sl = slice(h * HEAD_DIM, (h + 1) * HEAD_DIM)
        qh = q_ref[:, sl]
        lg = jnp.concatenate(
            [lax.dot_general(qh, _past_head(kp_ref, h, n_heads), NT, preferred_element_type=F32),
             lax.dot_general(qh, kn[:, sl], NT, preferred_element_type=F32)], axis=1) + bias
        p = jnp.exp2((lg - jnp.max(lg, axis=-1, keepdims=True)) * EXP2_SCALE)
        l = jnp.sum(p, axis=-1, keepdims=True)
        pb = p.astype(BF16)
        o = (jnp.dot(pb[:, :n_past], _past_head(vp_ref, h, n_heads), preferred_element_type=F32)
             + jnp.dot(pb[:, n_past:], vn[:, sl], preferred_element_type=F32))
        o_ref[:, sl] = (o / l).astype(o_ref.dtype)


def _dsa_cols_kernel(q_ref, qi_ref, wt_ref, k_ref, vt_ref, kiki_ref, tri_ref, o_ref, *,
                     tile0, n_sel, n_heads):
    i = pl.program_id(1) + tile0
    tq = q_ref.shape[0]
    S = k_ref.shape[1]
    wt = wt_ref[0]
    kiki = kiki_ref[0]
    score = jnp.zeros((S, tq), F32)
    for j in range(IDX_HEADS // 2):
        qq = _head_pair_rows(qi_ref[:, j * LANES:(j + 1) * LANES])
        s2 = lax.dot_general(kiki, qq, NT, preferred_element_type=F32)
        for half in range(2):
            h = 2 * j + half
            w = wt[IDX_DIM + h:IDX_DIM + h + 1, :] * IDX_SCALE
            score = score + jnp.maximum(s2[:, half * tq:(half + 1) * tq], 0.0) * w
    kpos = lax.broadcasted_iota(jnp.int32, (S, tq), 0)
    qpos = i * tq + lax.broadcasted_iota(jnp.int32, (1, tq), 1)
    adm = kpos < (lax.shift_right_logical(qpos, 6) + 1) * CHUNK
    key = jnp.where(adm, _order_key(score), jnp.int32(INT_MIN))
    bias = _select_topk(key, adm, tri_ref[...], n_sel, 0)

    for hp in range(n_heads // 2):
        sl2 = slice(2 * hp * HEAD_DIM, (2 * hp + 2) * HEAD_DIM)
        lg2 = lax.dot_general(k_ref[0, :, sl2], _head_pair_rows(q_ref[:, sl2]), NT,
                              preferred_element_type=F32)
        for half in range(2):
            sl = slice((2 * hp + half) * HEAD_DIM, (2 * hp + half + 1) * HEAD_DIM)
            lg = lg2[:, half * tq:(half + 1) * tq] + bias
            p = jnp.exp2((lg - jnp.max(lg, axis=0, keepdims=True)) * EXP2_SCALE)
            l = jnp.sum(p, axis=0, keepdims=True)
            ot = jnp.dot(vt_ref[0, sl, :], p.astype(BF16), preferred_element_type=F32)
            o_ref[:, sl] = (ot / l).T.astype(o_ref.dtype)


def _tri_block(n):
    return 256 if n % 256 == 0 else LANES


def dsa_rows_group(q, qi, smf, k_new, v_new, kiki_new, k_past, v_past, kiki_past, layer, grp, W,
                   ybuf, ybuf_shape):
    B, T, S, S_pad, pos0, row0 = (grp[k] for k in ("B", "T", "S", "S_pad", "pos0", "row0"))
    assert pos0 % LANES == 0 and T % CHUNK == 0 and pos0 > 0
    nq = T // CHUNK
    blk = _tri_block(S_pad)
    qrow = lambda w: pl.BlockSpec((CHUNK, w), lambda b, c: (b * nq + c, 0))
    new = lambda w: pl.BlockSpec((T, w), lambda b, c: (b, 0))
    kern = functools.partial(_dsa_rows_kernel, chunk0=pos0 // CHUNK, n_new=S_pad - pos0, s_valid=S,
                             n_sel=min(TOPK_MAX, S // 4), n_heads=W // HEAD_DIM)
    return _pcall(
        kern, name=f"dsa_{grp['name']}", grid=(B, nq),
        in_specs=[qrow(W), qrow(IDX_HEADS * IDX_DIM), qrow(LANES),
                  pl.BlockSpec((pos0 * (W // HEAD_DIM), HEAD_DIM), lambda b, c: (layer * B + b, 0)),
                  pl.BlockSpec((pos0 * (W // HEAD_DIM), HEAD_DIM), lambda b, c: (layer * B + b, 0)),
                  pl.BlockSpec((pos0, LANES), lambda b, c: (b, 0)),
                  new(W), new(W), new(LANES),
                  pl.BlockSpec((blk, blk), lambda b, c: (0, 0))],
        inputs=[q, qi, smf, k_past, v_past, kiki_past, k_new, v_new, kiki_new, _tri(blk, False)],
        out_specs=pl.BlockSpec((CHUNK, W), lambda b, c: (row0 // CHUNK + b * nq + c, 1)),
        out_shape=ybuf_shape, carries=[(0, ybuf)])


def dsa_cols_group(q, qi, smT, k_b, vT, kiki, grp, W, ybuf, ybuf_shape):
    B, T, row0 = grp["B"], grp["T"], grp["row0"]
    tq = LANES
    H = W // HEAD_DIM
    assert grp["pos0"] == 0 and T % tq == 0 and row0 % tq == 0 and H % 2 == 0
    nq = T // tq
    kb = max(tq, T // 8)
    tpc = kb // tq
    blk = _tri_block(kb)
    k3, kiki3 = k_b.reshape(B, T, W), kiki.reshape(B, T, LANES)
    for cls in range(nq // tpc):
        s_eff = (cls + 1) * kb
        t0 = cls * tpc
        qrow = lambda w: pl.BlockSpec((tq, w), lambda b, i: (b * nq + t0 + i, 0))
        keys = lambda w: pl.BlockSpec((1, s_eff, w), lambda b, i: (b, 0, 0))
        kern = functools.partial(_dsa_cols_kernel, tile0=t0, n_sel=min(TOPK_MAX, T // 4), n_heads=H)
        ybuf = _pcall(
            kern, name=f"dsa_{grp['name']}_{cls}", grid=(B, tpc),
            in_specs=[qrow(W), qrow(IDX_HEADS * IDX_DIM),
                      pl.BlockSpec((1, LANES, tq), lambda b, i: (b, 0, t0 + i)),
                      keys(W), pl.BlockSpec((1, W, s_eff), lambda b, i: (b, 0, 0)), keys(LANES),
                      pl.BlockSpec((blk, blk), lambda b, i: (0, 0))],
            inputs=[q, qi, smT, k3, vT, kiki3, _tri(blk, True)],
            out_specs=pl.BlockSpec((tq, W), lambda b, i: (row0 // tq + b * nq + t0 + i, 1)),
            out_shape=ybuf_shape, carries=[(0, ybuf)])
    return ybuf


def _cumsum_kernel(x_ref, o_ref):
    x = x_ref[...]
    n = x.shape[0]
    row = lax.broadcasted_iota(jnp.int32, x.shape, 0)
    d = 1
    while d < n:
        x = x + jnp.where(row >= d, pltpu.roll(x, d, 0), 0.0)
        d *= 2
    o_ref[...] = x


def cumsum_rows(x, B, n, name):
    spec = pl.BlockSpec((n, LANES), lambda b: (b, 0))
    return _pcall(_cumsum_kernel, name=name, grid=(B,), in_specs=[spec], inputs=[x], out_specs=spec,
                  out_shape=jax.ShapeDtypeStruct((B * n, LANES), F32))


def _fox_kernel(*refs, pos0, tile0, n_new, s_valid, fcol0, n_heads):
    if pos0:
        q_ref, fq_ref, fk_ref, kp_ref, vp_ref, kn_ref, vn_ref, o_ref = refs
    else:
        q_ref, fq_ref, fk_ref, kn_ref, vn_ref, o_ref = refs
    i = pl.program_id(1) + tile0
    tq = q_ref.shape[0]
    S = pos0 + n_new
    kn = _pad_rows(kn_ref[0], n_new)
    vn = _pad_rows(vn_ref[0], n_new)
    qpos = pos0 + i * tq + lax.broadcasted_iota(jnp.int32, (tq, S), 0)
    kpos = lax.broadcasted_iota(jnp.int32, (tq, S), 1)
    mask = jnp.where((kpos <= qpos) & (kpos < s_valid), 0.0, -jnp.inf)
    fq = fq_ref[...] * (HEAD_DIM ** 0.5)
    fk = fk_ref[0] * (HEAD_DIM ** 0.5)
    for h in range(n_heads):
        sl = slice(h * HEAD_DIM, (h + 1) * HEAD_DIM)
        qh = q_ref[:, sl]
        lg = lax.dot_general(qh, kn[:, sl], NT, preferred_element_type=F32)
        if pos0:
            lg = jnp.concatenate(
                [lax.dot_general(qh, _past_head(kp_ref, h, n_heads), NT, preferred_element_type=F32),
                 lg], axis=1)
        lg = lg + fq[:, fcol0 + h:fcol0 + h + 1] + (mask - fk[h:h + 1, :])
        p = jnp.exp2((lg - jnp.max(lg, axis=-1, keepdims=True)) * EXP2_SCALE)
        l = jnp.sum(p, axis=-1, keepdims=True)
        pb = p.astype(BF16)
        o = jnp.dot(pb[:, pos0:], vn[:, sl], preferred_element_type=F32)
        if pos0:
            o = o + jnp.dot(pb[:, :pos0], _past_head(vp_ref, h, n_heads), preferred_element_type=F32)
        o_ref[:, sl] = (o / l).astype(o_ref.dtype)


def fox_group(q, Fq, Fk_t, k_new, v_new, k_past, v_past, layer, grp, W, fcol0, ybuf, ybuf_shape):
    B, T, S, S_pad, pos0, row0 = (grp[k] for k in ("B", "T", "S", "S_pad", "pos0", "row0"))
    H = W // HEAD_DIM
    assert pos0 % LANES == 0
    tq = _pick(T, (256, 128, 64))
    nq = T // tq
    tpc = max(1, nq // 8) if pos0 == 0 else nq
    k3, v3 = k_new.reshape(B, T, W), v_new.reshape(B, T, W)
    past = pl.BlockSpec((pos0 * H, HEAD_DIM), lambda b, i: (layer * B + b, 0))
    for cls in range(nq // tpc):
        t0 = cls * tpc
        n_keys = min(T, (t0 + tpc) * tq)
        n_new = -(-n_keys // LANES) * LANES
        qrow = lambda w: pl.BlockSpec((tq, w), lambda b, i: (b * nq + t0 + i, 0))
        new = pl.BlockSpec((1, n_keys, W), lambda b, i: (b, 0, 0))
        kern = functools.partial(_fox_kernel, pos0=pos0, tile0=t0, n_new=n_new, s_valid=S,
                                 fcol0=fcol0, n_heads=H)
        ybuf = _pcall(
            kern, name=f"fox_{grp['name']}_{cls}", grid=(B, tpc),
            in_specs=[qrow(W), qrow(LANES), pl.BlockSpec((1, H, pos0 + n_new), lambda b, i: (b, 0, 0))]
            + ([past, past] if pos0 else []) + [new, new],
            inputs=[q, Fq, Fk_t] + ([k_past, v_past] if pos0 else []) + [k3, v3],
            out_specs=pl.BlockSpec((tq, W), lambda b, i: (row0 // tq + b * nq + t0 + i, 2)),
            out_shape=ybuf_shape, carries=[(0, ybuf)])
    return ybuf


def _pack_w_tail(w_in_t, W, H_C):
    sizes = (W, W, W, W, W, W, IDX_HEADS * IDX_DIM, IDX_DIM, IDX_HEADS, W, W, W, H_C, W, W)
    offs = [0]
    for sz in sizes:
        offs.append(offs[-1] + sz)
    seg = lambda k: w_in_t[:, offs[k]:offs[k + 1], :]
    L, _, D = w_in_t.shape
    pad = jnp.zeros((L, LANES - IDX_DIM - IDX_HEADS - H_C, D), w_in_t.dtype)
    return (jnp.concatenate([seg(k) for k in (9, 10, 11, 13, 14)], axis=1),
            jnp.concatenate([seg(7), seg(8), seg(12), pad], axis=1))


def _token_mix(xb, lw, groups, pasts, layer, depth, state_bufs):
    W = lw["W"]
    H = W // HEAD_DIM
    M = xb.shape[0]
    fcol0 = IDX_DIM + IDX_HEADS
    tms = (1536, 768, 384, 128, 64)
    P1 = matmul(xb, lw["w_in_t"], layer, 7 * W, F32, "in_proj_a", tms, 512, w_transposed=True)
    P2 = matmul(xb, lw["w_tail_t"], layer, 5 * W, F32, "in_proj_b", tms, 512, w_transposed=True)
    P3 = matmul(xb, lw["w_small_t"], layer, LANES, F32, "in_proj_c", tms, LANES, w_transposed=True)
    bf_row = jnp.zeros((1, LANES), F32).at[0, fcol0:fcol0 + H].set(lw["fox_b_f"])
    ybuf_shape = jax.ShapeDtypeStruct((M, 4 * W), BF16)
    ybuf = None
    small_states, new_bufs = [], []
    for gi, (grp, past) in enumerate(zip(groups, pasts)):
        B, T, S, S_pad = grp["B"], grp["T"], grp["S"], grp["S_pad"]
        pk_b, pv_b, pki_b, pk_c, pv_c, plf_c, st_a, st_d, h0 = past
        npast = S - T
        (q_b, k_b, v_x, qi_b, sm_x, kiki, cq_b, ck_b, cv_b, lf), bufs = prep_group(
            P1, P2, P3, bf_row, grp, W, layer, depth, None if state_bufs is None else state_bufs[gi])
        new_bufs.append(bufs)

        ybuf, nst_a = conv_a_group(P1, st_a, lw["conv_a_w"], grp, W, ybuf, ybuf_shape)
        ybuf, nst_d, h_last = lru_group(P2, st_d, h0, lw, grp, W, ybuf, ybuf_shape)

        lf3 = lf.reshape(B, T, LANES)
        if grp["transposed"]:
            ybuf = dsa_cols_group(q_b, qi_b, sm_x, k_b, v_x, kiki, grp, W, ybuf, ybuf_shape)
            lf_all = lf3
        else:
            pki = pki_b[layer]
            kiki_past = jnp.concatenate([pki, pki], axis=-1).reshape(B * npast, LANES)
            ybuf = dsa_rows_group(q_b, qi_b, sm_x, k_b, v_x, kiki, pk_b, pv_b, kiki_past, layer, grp,
                                  W, ybuf, ybuf_shape)
            plf = jnp.pad(plf_c[layer].astype(F32), ((0, 0), (0, 0), (fcol0, LANES - fcol0 - H)))
            lf_all = jnp.concatenate([plf, lf3, jnp.zeros((B, S_pad - S, LANES), F32)], axis=1)

        F_all = cumsum_rows(lf_all.reshape(B * S_pad, LANES), B, S_pad, f"cumsum_{grp['name']}")
        F3 = F_all.reshape(B, S_pad, LANES)
        Fq = F3[:, npast:npast + T, :].reshape(B * T, LANES)
        Fk_t = jnp.swapaxes(F3[:, :, fcol0:fcol0 + H], 1, 2)
        ybuf = fox_group(cq_b, Fq, Fk_t, ck_b, cv_b, pk_c, pv_c, layer, grp, W, fcol0, ybuf, ybuf_shape)

        small_states.append((lf3[:, :, :IDX_DIM], lf3[:, :, fcol0:fcol0 + H],
                             nst_a, nst_d, h_last.reshape(B, W)))
    y = matmul(ybuf, lw["w_out"], layer, 4 * W, BF16, "out_proj", tms, 512)
    return y, small_states, new_bufs


def kernel(x_prompt, x_sample, cache_dsa_k, cache_dsa_v, cache_dsa_kidx, cache_fox_k, cache_fox_v,
           cache_fox_logf, state_conv_a, state_conv_d, state_lru, ln_g, ln_b, ffn_w13, ffn_w2, w_in,
           fox_b_f, conv_a_w, conv_d_w, conv_d_b, lru_wa, lru_ba, lru_wx, lru_bx, lru_lambda, w_out):
    Bp, Tp, D = x_prompt.shape
    Bs, Ts, _ = x_sample.shape
    depth = ln_g.shape[0]
    past_len = cache_dsa_k.shape[2]
    W = D // 4
    H = W // HEAD_DIM
    assert IDX_HEADS * IDX_DIM == W and W % HEAD_DIM == 0
    alpha = (2.0 * depth) ** 0.25
    dt = x_prompt.dtype

    def group(name, row0, B, T, pos0):
        S = pos0 + T
        S_pad = -(-S // LANES) * LANES
        pos = pos0 + jnp.arange(T, dtype=jnp.int32)
        tabs = jnp.stack(_rope_tables(pos, HEAD_DIM, ROPE_DIM // 2)
                         + _rope_tables(pos, IDX_DIM, IDX_ROPE_DIM // 2)
                         + _rope_tables(pos, LANES, IDX_ROPE_DIM // 2))
        return dict(name=name, row0=row0, B=B, T=T, pos0=pos0, S=S, S_pad=S_pad, tabs=tabs,
                    transposed=(pos0 == 0 and T % LANES == 0))

    groups = [group("prompt", 0, Bp, Tp, 0), group("sample", Bp * Tp, Bs, Ts, past_len)]
    Rp, Rs = Bp * Tp, Bs * Ts

    x = (x_prompt.reshape(Rp, D), x_sample.reshape(Rs, D))
    xb = cast_rows(x[0], x[1], BF16, "cast_x")

    w13 = ffn_w13.reshape((2 * depth,) + ffn_w13.shape[2:])
    w2 = ffn_w2.reshape((2 * depth,) + ffn_w2.shape[2:])
    w_in_t = jnp.swapaxes(w_in, 1, 2)
    w_tail_t, w_small_t = _pack_w_tail(w_in_t, W, H)
    rows2d = lambda c: c.reshape(-1, HEAD_DIM)
    cache_rows = (rows2d(cache_dsa_k), rows2d(cache_dsa_v), cache_dsa_kidx, rows2d(cache_fox_k),
                  rows2d(cache_fox_v), cache_fox_logf)
    down_tms = (768, 384, 128, 64)

    small = [[], []]
    state_bufs = None
    for l in range(depth):
        lw = dict(W=W, w_in_t=w_in_t, w_tail_t=w_tail_t, w_small_t=w_small_t, fox_b_f=fox_b_f[l],
                  conv_a_w=conv_a_w[l],
                  conv_d_w=conv_d_w[l], conv_d_b=conv_d_b[l], lru_wa=lru_wa[l], lru_ba=lru_ba[l],
                  lru_wx=lru_wx[l], lru_bx=lru_bx[l], lru_lambda=lru_lambda[l], w_out=w_out)
        empty = (None,) * 6 + (jnp.zeros((Bp, CONV_A_W - 1, W), dt),
                               jnp.zeros((Bp, CONV_D_W - 1, W), dt), jnp.zeros((Bp, W), dt))
        sample_past = cache_rows + (state_conv_a[l], state_conv_d[l], state_lru[l])

        h = swiglu_up(xb, w13, 2 * l, "ffn_up")
        f = matmul(h, w2, 2 * l, D, BF16, "ffn_down", down_tms, 512, single_buffer_w=True)
        x, xb = residual_layer_norm(x, f, ln_g[l, 0], ln_b[l, 0], alpha, 0.5, "res_ln")

        y, small_states, state_bufs = _token_mix(xb, lw, groups, [empty, sample_past], l, depth,
                                                 state_bufs)
        x, xb = residual_layer_norm(x, y, ln_g[l, 1], ln_b[l, 1], alpha, 1.0, "res_ln")

        h = swiglu_up(xb, w13, 2 * l + 1, "ffn_up")
        f = matmul(h, w2, 2 * l + 1, D, BF16, "ffn_down", down_tms, 512, single_buffer_w=True)
        if l + 1 < depth:
            x, xb = residual_layer_norm(x, f, ln_g[l, 2], ln_b[l, 2], alpha, 0.5, "res_ln")
        else:
            outs = [residual_layer_norm(x, f, ln_g[l, 2], ln_b[l, 2], alpha, 0.5, "res_ln_out",
                                        rows=r)[0] for r in ((0, Rp), (Rp, Rs))]
        for g in range(2):
            small[g].append(small_states[g])

    res = [outs[0].reshape(Bp, Tp, D), outs[1].reshape(Bs, Ts, D)]
    for g, (B, T) in enumerate(((Bp, Tp), (Bs, Ts))):
        ks, vs, cks, cvs = (a.reshape(depth, B, T, H, HEAD_DIM) for a in state_bufs[g])
        kidx, logf, st_a, st_d, h_last = (jnp.stack([s[i] for s in small[g]]) for i in range(5))
        res += [ks, vs, kidx, cks, cvs, logf, st_a, st_d, h_last]
    return tuple(res)
```

```python
import functools
import math

import jax
import jax.numpy as jnp
from jax import lax
from jax.experimental import pallas as pl
from jax.experimental.pallas import tpu as pltpu

F32 = jnp.float32
BF16 = jnp.bfloat16

CHUNK = 64
HEAD_DIM = 128
ROPE_DIM = HEAD_DIM // 4
ROPE_THETA = 500000.0
IDX_HEADS = 16
IDX_DIM = 64
IDX_ROPE_DIM = IDX_DIM // 4
TOPK_MAX = 256
CONV_A_W = 3
CONV_D_W = 4
LRU_C = 8.0
LN_EPS = 1e-5
LANES = 128
SMALL_W = 512
VMEM_LIMIT = 58 * 1024 * 1024
INT_MIN = -2 ** 31
NT = (((1,), (1,)), ((), ()))


def _pcall(body, *, name, grid, in_specs, inputs, out_specs, out_shape, carries=(), scratch=()):
    n_in, n_c = len(inputs), len(carries)

    def kern(*refs):
        body(*refs[:n_in], *refs[n_in + n_c:])

    return pl.pallas_call(
        kern, name=name, grid=grid,
        in_specs=list(in_specs) + [pl.BlockSpec(memory_space=pl.ANY)] * n_c,
        out_specs=out_specs, out_shape=out_shape,
        input_output_aliases={n_in + k: oi for k, (oi, _) in enumerate(carries)},
        scratch_shapes=list(scratch),
        compiler_params=pltpu.CompilerParams(dimension_semantics=("arbitrary",) * len(grid),
                                             vmem_limit_bytes=VMEM_LIMIT),
    )(*inputs, *[a for _, a in carries])


def _pick(n, cands):
    for c in cands:
        if n % c == 0:
            return c
    raise ValueError(f"no tile for {n} in {cands}")


def _mm_kernel(x_ref, w_ref, o_ref, wb_ref, *, w_transposed):
    @pl.when(pl.program_id(1) == 0)
    def _():
        w = w_ref[0]
        wb_ref[...] = (w.T if w_transposed else w).astype(BF16)

    o_ref[...] = jnp.dot(x_ref[...], wb_ref[...], preferred_element_type=F32).astype(o_ref.dtype)


def matmul(x, w, li, n_cols, out_dtype, name, tms, tn, single_buffer_w=False, w_transposed=False):
    M, K = x.shape
    tm = _pick(M, tms)
    assert n_cols % tn == 0 and w.shape[2 if w_transposed else 1] == K
    mode = dict(pipeline_mode=pl.Buffered(1)) if single_buffer_w else {}
    if w_transposed:
        w_spec = pl.BlockSpec((1, tn, K), lambda j, i: (li, j, 0), **mode)
    else:
        w_spec = pl.BlockSpec((1, K, tn), lambda j, i: (li, 0, j), **mode)
    return _pcall(
        functools.partial(_mm_kernel, w_transposed=w_transposed), name=name,
        grid=(n_cols // tn, M // tm),
        in_specs=[pl.BlockSpec((tm, K), lambda j, i: (i, 0)), w_spec],
        inputs=[x, w],
        out_specs=pl.BlockSpec((tm, tn), lambda j, i: (i, j)),
        out_shape=jax.ShapeDtypeStruct((M, n_cols), out_dtype),
        scratch=[pltpu.VMEM((K, tn), BF16)])


def _swiglu_kernel(x_ref, wg_ref, wu_ref, o_ref, wgb_ref, wub_ref):
    @pl.when(pl.program_id(1) == 0)
    def _():
        wgb_ref[...] = wg_ref[0].astype(BF16)
        wub_ref[...] = wu_ref[0].astype(BF16)

    x = x_ref[...]
    g = jnp.dot(x, wgb_ref[...], preferred_element_type=F32)
    u = jnp.dot(x, wub_ref[...], preferred_element_type=F32)
    o_ref[...] = (g * jax.nn.sigmoid(g) * u).astype(o_ref.dtype)


def swiglu_up(x, w13, li, name):
    M, K = x.shape
    F = w13.shape[2] // 2
    tm = _pick(M, (1536, 768, 384, 128, 64))
    tn = _pick(F, (256, 128))
    nb = F // tn
    return _pcall(
        _swiglu_kernel, name=name, grid=(nb, M // tm),
        in_specs=[pl.BlockSpec((tm, K), lambda j, i: (i, 0)),
                  pl.BlockSpec((1, K, tn), lambda j, i: (li, 0, j)),
                  pl.BlockSpec((1, K, tn), lambda j, i: (li, 0, j + nb))],
        inputs=[x, w13, w13],
        out_specs=pl.BlockSpec((tm, tn), lambda j, i: (i, j)),
        out_shape=jax.ShapeDtypeStruct((M, F), BF16),
        scratch=[pltpu.VMEM((K, tn), BF16), pltpu.VMEM((K, tn), BF16)])


def _split_rows_specs(tm, D, n_first):
    return [pl.BlockSpec((tm, D), lambda i: (jnp.minimum(i, n_first - 1), 0)),
            pl.BlockSpec((tm, D), lambda i: (jnp.maximum(i - n_first, 0), 0))]


def _cast_split_kernel(a_ref, b_ref, o_ref, *, n_first):
    o_ref[...] = jnp.where(pl.program_id(0) < n_first, a_ref[...], b_ref[...]).astype(o_ref.dtype)


def cast_rows(a, b, dtype, name):
    D = a.shape[1]
    tm = _pick(b.shape[0], (256, 128, 64))
    assert a.shape[0] % tm == 0
    n = (a.shape[0] + b.shape[0]) // tm
    return _pcall(
        functools.partial(_cast_split_kernel, n_first=a.shape[0] // tm), name=name, grid=(n,),
        in_specs=_split_rows_specs(tm, D, a.shape[0] // tm), inputs=[a, b],
        out_specs=pl.BlockSpec((tm, D), lambda i: (i, 0)),
        out_shape=jax.ShapeDtypeStruct((n * tm, D), dtype))


def _res_ln_kernel(*refs, alpha, yscale, n_first):
    if n_first is None:
        x_ref, y_ref, g_ref, b_ref, *o_refs = refs
        x = x_ref[...]
    else:
        xa_ref, xb_ref, y_ref, g_ref, b_ref, *o_refs = refs
        x = jnp.where(pl.program_id(0) < n_first, xa_ref[...], xb_ref[...])
    z = alpha * x + yscale * y_ref[...].astype(F32)
    mu = jnp.mean(z, axis=-1, keepdims=True)
    zc = z - mu
    var = jnp.mean(zc * zc, axis=-1, keepdims=True)
    o = zc * lax.rsqrt(var + LN_EPS) * g_ref[...] + b_ref[...]
    o_refs[0][...] = o
    if len(o_refs) > 1:
        o_refs[1][...] = o.astype(BF16)


def residual_layer_norm(x, y, g, b, alpha, yscale, name, rows=None):
    M, D = y.shape
    row0, n = rows if rows is not None else (0, M)
    tm = _pick(n if not isinstance(x, tuple) else x[1].shape[0], (256, 128, 64))
    assert row0 % tm == 0
    rb0 = row0 // tm
    src = pl.BlockSpec((tm, D), lambda i: (rb0 + i, 0))
    dst = pl.BlockSpec((tm, D), lambda i: (i, 0))
    vec = pl.BlockSpec((1, D), lambda i: (0, 0))
    shapes = [jax.ShapeDtypeStruct((n, D), F32)]
    if rows is None:
        shapes.append(jax.ShapeDtypeStruct((n, D), BF16))
    if isinstance(x, tuple):
        assert rows is None and x[0].shape[0] % tm == 0
        n_first = x[0].shape[0] // tm
        x_specs, xs = _split_rows_specs(tm, D, n_first), list(x)
    else:
        n_first, x_specs, xs = None, [src], [x]
    return _pcall(
        functools.partial(_res_ln_kernel, alpha=alpha, yscale=yscale, n_first=n_first), name=name,
        grid=(n // tm,), in_specs=x_specs + [src, vec, vec],
        inputs=xs + [y, g.reshape(1, D), b.reshape(1, D)],
        out_specs=[dst] * len(shapes), out_shape=shapes)


def _rope_tables(pos, period, half):
    inv = ROPE_THETA ** (-jnp.arange(half, dtype=F32) / half)
    ang = pos.astype(F32)[:, None] * inv[None, :]
    cos, sin = jnp.cos(ang), jnp.sin(ang)
    T = pos.shape[0]
    reps = LANES // period
    zeros = jnp.zeros((T, period - 2 * half), F32)
    c = jnp.concatenate([cos, cos, jnp.ones((T, period - 2 * half), F32)], axis=1)
    s1 = jnp.concatenate([-sin, jnp.zeros((T, half), F32), zeros], axis=1)
    s2 = jnp.concatenate([jnp.zeros((T, half), F32), sin, zeros], axis=1)
    return [jnp.tile(t, (1, reps)) for t in (c, s1, s2)]


def _rope(x, c, s1, s2, half):
    outs = []
    for g in range(x.shape[-1] // LANES):
        xg = x[:, g * LANES:(g + 1) * LANES]
        outs.append(xg * c + pltpu.roll(xg, LANES - half, 1) * s1 + pltpu.roll(xg, half, 1) * s2)
    return outs[0] if len(outs) == 1 else jnp.concatenate(outs, axis=1)


def _store_heads(o_ref, x):
    n_heads = x.shape[1] // HEAD_DIM
    for h in range(n_heads):
        o_ref[pl.ds(h, x.shape[0], stride=n_heads), :] = x[:, h * HEAD_DIM:(h + 1) * HEAD_DIM]


def _prep_kernel(bq_ref, bk_ref, bv_ref, bqi_ref, cq_ref, ck_ref, cv_ref, sm_ref, tab_ref, bf_ref,
                 q_o, kb_o, v_o, qi_o, sm_o, kiki_o, cq_o, ck_o, cv_o, lf_o,
                 ks_o, vs_o, cks_o, cvs_o, *, transposed):
    t = tab_ref[...]
    q_o[...] = _rope(bq_ref[...], t[0], t[1], t[2], ROPE_DIM // 2).astype(BF16)
    k = _rope(bk_ref[...], t[0], t[1], t[2], ROPE_DIM // 2)
    _store_heads(ks_o, k)
    kb_o[...] = k.astype(BF16)
    v = bv_ref[...]
    _store_heads(vs_o, v)
    qi_o[...] = _rope(bqi_ref[...], t[3], t[4], t[5], IDX_ROPE_DIM // 2).astype(BF16)
    sm = _rope(sm_ref[...], t[6], t[7], t[8], IDX_ROPE_DIM // 2)
    if transposed:
        v_o[0] = v.T.astype(BF16)
        sm_o[0] = sm.T
    else:
        v_o[...] = v.astype(BF16)
        sm_o[...] = sm
    lane = lax.broadcasted_iota(jnp.int32, sm.shape, 1)
    ki_lo = jnp.where(lane < IDX_DIM, sm, 0.0)
    kiki_o[...] = (ki_lo + pltpu.roll(ki_lo, IDX_DIM, 1)).astype(BF16)
    cq_o[...] = cq_ref[...].astype(BF16)
    ck = ck_ref[...]
    _store_heads(cks_o, ck)
    ck_o[...] = ck.astype(BF16)
    cv = cv_ref[...]
    _store_heads(cvs_o, cv)
    cv_o[...] = cv.astype(BF16)
    z = -(sm + bf_ref[...])
    lf_o[...] = jnp.where(lane < IDX_DIM, sm, -(jnp.maximum(z, 0.0) + jnp.log1p(jnp.exp(-jnp.abs(z)))))


def prep_group(P1, P2, bf_row, grp, W, layer, depth, state_bufs):
    row0, B, T = grp["row0"], grp["B"], grp["T"]
    tr = grp["transposed"]
    tt = _pick(T, (256, 128, 64))
    nt = T // tt
    rb0 = row0 // tt
    R = B * T

    def col(cb, width):
        return pl.BlockSpec((tt, width), lambda b, i, cb=cb: (rb0 + b * nt + i, cb))

    rows = lambda w: pl.BlockSpec((tt, w), lambda b, i: (b * nt + i, 0))
    n_heads = W // HEAD_DIM
    srows = pl.BlockSpec((tt * n_heads, HEAD_DIM), lambda b, i: (layer * (R // tt) + b * nt + i, 0))
    bf = lambda w: (rows(w), jax.ShapeDtypeStruct((R, w), BF16))
    if tr:
        v_out = (pl.BlockSpec((1, W, tt), lambda b, i: (b, 0, i)), jax.ShapeDtypeStruct((B, W, T), BF16))
        sm_out = (pl.BlockSpec((1, LANES, tt), lambda b, i: (b, 0, i)),
                  jax.ShapeDtypeStruct((B, LANES, T), F32))
    else:
        v_out = bf(W)
        sm_out = (rows(LANES), jax.ShapeDtypeStruct((R, LANES), F32))
    state = (srows, jax.ShapeDtypeStruct((depth * R * n_heads, HEAD_DIM), F32))
    outs = [bf(W), bf(W), v_out, bf(IDX_HEADS * IDX_DIM), sm_out, bf(LANES),
            bf(W), bf(W), bf(W), (rows(LANES), jax.ShapeDtypeStruct((R, LANES), F32)),
            state, state, state, state]
    carries = [] if state_bufs is None else [(10 + k, a) for k, a in enumerate(state_bufs)]
    small_cb = (5 * W) // LANES
    res = _pcall(
        functools.partial(_prep_kernel, transposed=tr), name=f"prep_{grp['name']}", grid=(B, nt),
        in_specs=[col(3, W), col(4, W), col(5, W), col(6, W), col(0, W), col(1, W), col(2, W),
                  col(small_cb, LANES),
                  pl.BlockSpec((9, tt, LANES), lambda b, i: (0, i, 0)),
                  pl.BlockSpec((1, LANES), lambda b, i: (0, 0))],
        inputs=[P1] * 4 + [P2] * 4 + [grp["tabs"], bf_row],
        out_specs=[s for s, _ in outs], out_shape=[s for _, s in outs], carries=carries)
    return res[:10], list(res[10:])


def _shift_rows(x, d, fill_rows):
    row = lax.broadcasted_iota(jnp.int32, x.shape, 0)
    y = pltpu.roll(x, d, 0)
    for t in range(d):
        y = jnp.where(row == t, fill_rows[t:t + 1, :], y)
    return y


def _conv_a_kernel(ah_ref, ab_ref, ac_ref, st_ref, w_ref, y_ref, nst_ref):
    z = ac_ref[...] * ah_ref[...]
    st = st_ref[0]
    w = w_ref[...]
    T = z.shape[0]
    z1 = _shift_rows(z, 1, st[1:2, :])
    z2 = _shift_rows(z, 2, st)
    conv = z2 * w[0:1, :] + z1 * w[1:2, :] + z * w[2:3, :]
    y_ref[...] = (ab_ref[...] * conv).astype(y_ref.dtype)
    nst_ref[0] = z[T - 2:T, :]


def conv_a_group(P, state_a, conv_w, grp, W, ybuf, ybuf_shape):
    row0, B, T = grp["row0"], grp["B"], grp["T"]
    assert T >= CONV_A_W - 1 and row0 % T == 0
    tc = _pick(W, (256, 128))
    nc = W // tc
    rb0 = row0 // T

    def col(k):
        return pl.BlockSpec((T, tc), lambda b, c, k=k: (rb0 + b, k * nc + c))

    return _pcall(
        _conv_a_kernel, name=f"conv_a_{grp['name']}", grid=(B, nc),
        in_specs=[col(0), col(1), col(2),
                  pl.BlockSpec((1, CONV_A_W - 1, tc), lambda b, c: (b, 0, c)),
                  pl.BlockSpec((CONV_A_W, tc), lambda b, c: (0, c))],
        inputs=[P, P, P, state_a, conv_w],
        out_specs=[pl.BlockSpec((T, tc), lambda b, c: (rb0 + b, c)),
                   pl.BlockSpec((1, CONV_A_W - 1, tc), lambda b, c: (b, 0, c))],
        out_shape=[ybuf_shape, jax.ShapeDtypeStruct((B, CONV_A_W - 1, W), F32)],
        carries=[] if ybuf is None else [(0, ybuf)])


def _lru_kernel(dx_ref, dg_ref, st_ref, h0_ref, cw_ref, cb_ref, wa_ref, ba_ref, wx_ref, bx_ref,
                lam_ref, y_ref, nst_ref, hl_ref):
    x = dx_ref[...]
    T = x.shape[0]
    st = st_ref[0]
    w = cw_ref[...]
    x1 = _shift_rows(x, 1, st[2:3, :])
    x2 = _shift_rows(x, 2, st[1:3, :])
    x3 = _shift_rows(x, 3, st)
    xc = x3 * w[0:1, :] + x2 * w[1:2, :] + x1 * w[2:3, :] + x * w[3:4, :] + cb_ref[...]
    xb = xc.astype(BF16)
    r = jax.nn.sigmoid(jnp.dot(xb, wa_ref[0].astype(BF16), preferred_element_type=F32) + ba_ref[...])
    i = jax.nn.sigmoid(jnp.dot(xb, wx_ref[0].astype(BF16), preferred_element_type=F32) + bx_ref[...])
    nl = -lam_ref[...]
    sp = jnp.maximum(nl, 0.0) + jnp.log1p(jnp.exp(-jnp.abs(nl)))
    log_a = -LRU_C * r * sp
    a = jnp.exp(log_a)
    u = jnp.sqrt(-jnp.tanh(log_a) * (a * a + 1.0)) * i * xc
    row = lax.broadcasted_iota(jnp.int32, x.shape, 0)
    u = jnp.where(row == 0, u + a * h0_ref[0], u)
    d = 1
    while d < T:
        keep = row >= d
        a_prev = jnp.where(keep, pltpu.roll(a, d, 0), 1.0)
        u_prev = jnp.where(keep, pltpu.roll(u, d, 0), 0.0)
        u = a * u_prev + u
        a = a * a_prev
        d *= 2
    y_ref[...] = (jax.nn.gelu(dg_ref[...]) * u).astype(y_ref.dtype)
    nst_ref[0] = x[T - 3:T, :]
    hl_ref[0] = u[T - 1:T, :]


def lru_group(P, state_d, h0, lw, grp, W, ybuf, ybuf_shape):
    row0, B, T = grp["row0"], grp["B"], grp["T"]
    wa, wx = lw["lru_wa"], lw["lru_wx"]
    nblk, bw = wa.shape[0], wa.shape[1]
    assert T >= CONV_D_W - 1 and row0 % T == 0 and bw % LANES == 0 and nblk * bw == W
    rb0 = row0 // T
    cb = (3 * W) // bw
    yb = (3 * W) // bw

    def col(k):
        return pl.BlockSpec((T, bw), lambda b, c, k=k: (rb0 + b, cb + k * nblk + c))

    vec = pl.BlockSpec((1, bw), lambda b, c: (0, c))
    mat = pl.BlockSpec((1, bw, bw), lambda b, c: (c, 0, 0))
    row = lambda a: a.reshape(1, W)
    return _pcall(
        _lru_kernel, name=f"lru_{grp['name']}", grid=(B, nblk),
        in_specs=[col(0), col(1),
                  pl.BlockSpec((1, CONV_D_W - 1, bw), lambda b, c: (b, 0, c)),
                  pl.BlockSpec((1, 1, bw), lambda b, c: (b, 0, c)),
                  pl.BlockSpec((CONV_D_W, bw), lambda b, c: (0, c)),
                  vec, mat, vec, mat, vec, vec],
        inputs=[P, P, state_d, h0.reshape(B, 1, W), lw["conv_d_w"], row(lw["conv_d_b"]), wa,
                row(lw["lru_ba"]), wx, row(lw["lru_bx"]), row(lw["lru_lambda"])],
        out_specs=[pl.BlockSpec((T, bw), lambda b, c: (rb0 + b, yb + c)),
                   pl.BlockSpec((1, CONV_D_W - 1, bw), lambda b, c: (b, 0, c)),
                   pl.BlockSpec((1, 1, bw), lambda b, c: (b, 0, c))],
        out_shape=[ybuf_shape, jax.ShapeDtypeStruct((B, CONV_D_W - 1, W), F32),
                   jax.ShapeDtypeStruct((B, 1, W), F32)],
        carries=[(0, ybuf)])


def _order_key(x):
    bits = pltpu.bitcast(x + 0.0, jnp.int32)
    return jnp.where(bits < 0, bits ^ jnp.int32(0x7FFFFFFF), bits)


def _count(mask, axis):
    ind = jnp.where(mask, 1.0, 0.0)
    if axis == 1:
        return jnp.sum(ind, axis=1, keepdims=True)
    S, tq = ind.shape
    g = 8 if S % 64 == 0 else 1
    part = ind.reshape(g, S // (8 * g), 8, tq).sum(axis=1)
    return part.sum(axis=0).sum(axis=0, keepdims=True)


def _select_topk(key, adm, tri, n_sel, axis):
    shape = tuple(1 if a == axis else s for a, s in enumerate(key.shape))

    def bit_step(b, t):
        cand = t + lax.shift_left(jnp.int32(1), 31 - b)
        return jnp.where(_count(key >= cand, axis) >= float(n_sel), cand, t)

    thr = lax.fori_loop(0, 32, bit_step, jnp.full(shape, INT_MIN, jnp.int32))
    eq = jnp.where(key == thr, 1.0, 0.0).astype(BF16)
    blk = tri.shape[0]
    total = jnp.zeros(shape, F32)
    ranks = []
    for b in range(key.shape[axis] // blk):
        if axis == 0:
            r = jnp.dot(tri, eq[b * blk:(b + 1) * blk, :], preferred_element_type=F32) + total
            total = r[blk - 1:blk, :]
        else:
            r = jnp.dot(eq[:, b * blk:(b + 1) * blk], tri, preferred_element_type=F32) + total
            total = r[:, blk - 1:blk]
        ranks.append(r)
    rank = jnp.concatenate(ranks, axis=axis)
    need = float(n_sel) - _count(key > thr, axis)
    ninf = -jnp.inf
    tie = jnp.where(key == thr, jnp.where(rank <= need, 0.0, ninf), ninf)
    return jnp.where(adm, jnp.where(key > thr, 0.0, tie), ninf)


IDX_SCALE = (IDX_DIM ** -0.5) * (IDX_HEADS ** -0.5)
EXP2_SCALE = (HEAD_DIM ** -0.5) * math.log2(math.e)


def _tri(n, lower):
    r = lax.broadcasted_iota(jnp.int32, (n, n), 0)
    c = lax.broadcasted_iota(jnp.int32, (n, n), 1)
    return jnp.where(r >= c if lower else r <= c, 1.0, 0.0).astype(BF16)


def _head_pair_rows(x):
    lane = lax.broadcasted_iota(jnp.int32, x.shape, 1)
    zero = jnp.zeros_like(x)
    half = x.shape[1] // 2
    return jnp.concatenate([jnp.where(lane < half, x, zero), jnp.where(lane >= half, x, zero)], axis=0)


def _pad_rows(x, n):
    if x.shape[0] == n:
        return x
    return jnp.concatenate([x, jnp.zeros((n - x.shape[0], x.shape[1]), x.dtype)], axis=0)


def _past_head(ref, h, n_heads):
    return ref[pl.ds(h, ref.shape[0] // n_heads, stride=n_heads), :].astype(BF16)


def _dsa_rows_kernel(q_ref, qi_ref, sm_ref, kp_ref, vp_ref, kikip_ref, kn_ref, vn_ref, kikin_ref,
                     tri_ref, o_ref, *, chunk0, n_new, s_valid, n_sel, n_heads):
    c = pl.program_id(1)
    tq = q_ref.shape[0]
    n_past = kp_ref.shape[0] // n_heads
    S = n_past + n_new
    sm = sm_ref[...]
    kn = _pad_rows(kn_ref[...], n_new)
    vn = _pad_rows(vn_ref[...], n_new)
    kikip = kikip_ref[...].astype(BF16)
    kikin = _pad_rows(kikin_ref[...], n_new)
    score = jnp.zeros((tq, S), F32)
    for j in range(IDX_HEADS // 2):
        qq = _head_pair_rows(qi_ref[:, j * LANES:(j + 1) * LANES])
        s2 = jnp.concatenate([lax.dot_general(qq, kikip, NT, preferred_element_type=F32),
                              lax.dot_general(qq, kikin, NT, preferred_element_type=F32)], axis=1)
        for half in range(2):
            h = 2 * j + half
            w = sm[:, IDX_DIM + h:IDX_DIM + h + 1] * IDX_SCALE
            score = score + jnp.maximum(s2[half * tq:(half + 1) * tq, :], 0.0) * w
    kpos = lax.broadcasted_iota(jnp.int32, (tq, S), 1)
    adm = kpos < jnp.minimum((chunk0 + c + 1) * CHUNK, s_valid)
    key = jnp.where(adm, _order_key(score), jnp.int32(INT_MIN))
    bias = _select_topk(key, adm, tri_ref[...], n_sel, 1)

    for h in range(n_heads):
        sl = slice(h * HEAD_DIM, (h + 1) * HEAD_DIM)
        qh = q_ref[:, sl]
        lg = jnp.concatenate(
            [lax.dot_general(qh, _past_head(kp_ref, h, n_heads), NT, preferred_element_type=F32),
             lax.dot_general(qh, kn[:, sl], NT, preferred_element_type=F32)], axis=1) + bias
        p = jnp.exp2((lg - jnp.max(lg, axis=-1, keepdims=True)) * EXP2_SCALE)
        l = jnp.sum(p, axis=-1, keepdims=True)
        pb = p.astype(BF16)
        o = (jnp.dot(pb[:, :n_past], _past_head(vp_ref, h, n_heads), preferred_element_type=F32)
             + jnp.dot(pb[:, n_past:], vn[:, sl], preferred_element_type=F32))
        o_ref[:, sl] = (o / l).astype(o_ref.dtype)


def _dsa_cols_kernel(q_ref, qi_ref, wt_ref, k_ref, vt_ref, kiki_ref, tri_ref, o_ref, *,
                     tile0, n_sel, n_heads):
    i = pl.program_id(1) + tile0
    tq = q_ref.shape[0]
    S = k_ref.shape[1]
    wt = wt_ref[0]
    kiki = kiki_ref[0]
    score = jnp.zeros((S, tq), F32)
    for j in range(IDX_HEADS // 2):
        qq = _head_pair_rows(qi_ref[:, j * LANES:(j + 1) * LANES])
        s2 = lax.dot_general(kiki, qq, NT, preferred_element_type=F32)
        for half in range(2):
            h = 2 * j + half
            w = wt[IDX_DIM + h:IDX_DIM + h + 1, :] * IDX_SCALE
            score = score + jnp.maximum(s2[:, half * tq:(half + 1) * tq], 0.0) * w
    kpos = lax.broadcasted_iota(jnp.int32, (S, tq), 0)
    qpos = i * tq + lax.broadcasted_iota(jnp.int32, (1, tq), 1)
    adm = kpos < (lax.shift_right_logical(qpos, 6) + 1) * CHUNK
    key = jnp.where(adm, _order_key(score), jnp.int32(INT_MIN))
    bias = _select_topk(key, adm, tri_ref[...], n_sel, 0)

    for hp in range(n_heads // 2):
        sl2 = slice(2 * hp * HEAD_DIM, (2 * hp + 2) * HEAD_DIM)
        lg2 = lax.dot_general(k_ref[0, :, sl2], _head_pair_rows(q_ref[:, sl2]), NT,
                              preferred_element_type=F32)
        for half in range(2):
            sl = slice((2 * hp + half) * HEAD_DIM, (2 * hp + half + 1) * HEAD_DIM)
            lg = lg2[:, half * tq:(half + 1) * tq] + bias
            p = jnp.exp2((lg - jnp.max(lg, axis=0, keepdims=True)) * EXP2_SCALE)
            l = jnp.sum(p, axis=0, keepdims=True)
            ot = jnp.dot(vt_ref[0, sl, :], p.astype(BF16), preferred_element_type=F32)
            o_ref[:, sl] = (ot / l).T.astype(o_ref.dtype)


def _tri_block(n):
    return 256 if n % 256 == 0 else LANES


def dsa_rows_group(q, qi, smf, k_new, v_new, kiki_new, k_past, v_past, kiki_past, layer, grp, W,
                   ybuf, ybuf_shape):
    B, T, S, S_pad, pos0, row0 = (grp[k] for k in ("B", "T", "S", "S_pad", "pos0", "row0"))
    assert pos0 % LANES == 0 and T % CHUNK == 0 and pos0 > 0
    nq = T // CHUNK
    blk = _tri_block(S_pad)
    qrow = lambda w: pl.BlockSpec((CHUNK, w), lambda b, c: (b * nq + c, 0))
    new = lambda w: pl.BlockSpec((T, w), lambda b, c: (b, 0))
    kern = functools.partial(_dsa_rows_kernel, chunk0=pos0 // CHUNK, n_new=S_pad - pos0, s_valid=S,
                             n_sel=min(TOPK_MAX, S // 4), n_heads=W // HEAD_DIM)
    return _pcall(
        kern, name=f"dsa_{grp['name']}", grid=(B, nq),
        in_specs=[qrow(W), qrow(IDX_HEADS * IDX_DIM), qrow(LANES),
                  pl.BlockSpec((pos0 * (W // HEAD_DIM), HEAD_DIM), lambda b, c: (layer * B + b, 0)),
                  pl.BlockSpec((pos0 * (W // HEAD_DIM), HEAD_DIM), lambda b, c: (layer * B + b, 0)),
                  pl.BlockSpec((pos0, LANES), lambda b, c: (b, 0)),
                  new(W), new(W), new(LANES),
                  pl.BlockSpec((blk, blk), lambda b, c: (0, 0))],
        inputs=[q, qi, smf, k_past, v_past, kiki_past, k_new, v_new, kiki_new, _tri(blk, False)],
        out_specs=pl.BlockSpec((CHUNK, W), lambda b, c: (row0 // CHUNK + b * nq + c, 1)),
        out_shape=ybuf_shape, carries=[(0, ybuf)])


def dsa_cols_group(q, qi, smT, k_b, vT, kiki, grp, W, ybuf, ybuf_shape):
    B, T, row0 = grp["B"], grp["T"], grp["row0"]
    tq = LANES
    H = W // HEAD_DIM
    assert grp["pos0"] == 0 and T % tq == 0 and row0 % tq == 0 and H % 2 == 0
    nq = T // tq
    kb = max(tq, T // 8)
    tpc = kb // tq
    blk = _tri_block(kb)
    k3, kiki3 = k_b.reshape(B, T, W), kiki.reshape(B, T, LANES)
    for cls in range(nq // tpc):
        s_eff = (cls + 1) * kb
        t0 = cls * tpc
        qrow = lambda w: pl.BlockSpec((tq, w), lambda b, i: (b * nq + t0 + i, 0))
        keys = lambda w: pl.BlockSpec((1, s_eff, w), lambda b, i: (b, 0, 0))
        kern = functools.partial(_dsa_cols_kernel, tile0=t0, n_sel=min(TOPK_MAX, T // 4), n_heads=H)
        ybuf = _pcall(
            kern, name=f"dsa_{grp['name']}_{cls}", grid=(B, tpc),
            in_specs=[qrow(W), qrow(IDX_HEADS * IDX_DIM),
                      pl.BlockSpec((1, LANES, tq), lambda b, i: (b, 0, t0 + i)),
                      keys(W), pl.BlockSpec((1, W, s_eff), lambda b, i: (b, 0, 0)), keys(LANES),
                      pl.BlockSpec((blk, blk), lambda b, i: (0, 0))],
            inputs=[q, qi, smT, k3, vT, kiki3, _tri(blk, True)],
            out_specs=pl.BlockSpec((tq, W), lambda b, i: (row0 // tq + b * nq + t0 + i, 1)),
            out_shape=ybuf_shape, carries=[(0, ybuf)])
    return ybuf


def _cumsum_kernel(x_ref, o_ref):
    x = x_ref[...]
    n = x.shape[0]
    row = lax.broadcasted_iota(jnp.int32, x.shape, 0)
    d = 1
    while d < n:
        x = x + jnp.where(row >= d, pltpu.roll(x, d, 0), 0.0)
        d *= 2
    o_ref[...] = x


def cumsum_rows(x, B, n, name):
    spec = pl.BlockSpec((n, LANES), lambda b: (b, 0))
    return _pcall(_cumsum_kernel, name=name, grid=(B,), in_specs=[spec], inputs=[x], out_specs=spec,
                  out_shape=jax.ShapeDtypeStruct((B * n, LANES), F32))


def _fox_kernel(*refs, pos0, tile0, n_new, s_valid, fcol0, n_heads):
    if pos0:
        q_ref, fq_ref, fk_ref, kp_ref, vp_ref, kn_ref, vn_ref, o_ref = refs
    else:
        q_ref, fq_ref, fk_ref, kn_ref, vn_ref, o_ref = refs
    i = pl.program_id(1) + tile0
    tq = q_ref.shape[0]
    S = pos0 + n_new
    kn = _pad_rows(kn_ref[0], n_new)
    vn = _pad_rows(vn_ref[0], n_new)
    qpos = pos0 + i * tq + lax.broadcasted_iota(jnp.int32, (tq, S), 0)
    kpos = lax.broadcasted_iota(jnp.int32, (tq, S), 1)
    mask = jnp.where((kpos <= qpos) & (kpos < s_valid), 0.0, -jnp.inf)
    fq = fq_ref[...] * (HEAD_DIM ** 0.5)
    fk = fk_ref[0] * (HEAD_DIM ** 0.5)
    for h in range(n_heads):
        sl = slice(h * HEAD_DIM, (h + 1) * HEAD_DIM)
        qh = q_ref[:, sl]
        lg = lax.dot_general(qh, kn[:, sl], NT, preferred_element_type=F32)
        if pos0:
            lg = jnp.concatenate(
                [lax.dot_general(qh, _past_head(kp_ref, h, n_heads), NT, preferred_element_type=F32),
                 lg], axis=1)
        lg = lg + fq[:, fcol0 + h:fcol0 + h + 1] + (mask - fk[h:h + 1, :])
        p = jnp.exp2((lg - jnp.max(lg, axis=-1, keepdims=True)) * EXP2_SCALE)
        l = jnp.sum(p, axis=-1, keepdims=True)
        pb = p.astype(BF16)
        o = jnp.dot(pb[:, pos0:], vn[:, sl], preferred_element_type=F32)
        if pos0:
            o = o + jnp.dot(pb[:, :pos0], _past_head(vp_ref, h, n_heads), preferred_element_type=F32)
        o_ref[:, sl] = (o / l).astype(o_ref.dtype)


def fox_group(q, Fq, Fk_t, k_new, v_new, k_past, v_past, layer, grp, W, fcol0, ybuf, ybuf_shape):
    B, T, S, S_pad, pos0, row0 = (grp[k] for k in ("B", "T", "S", "S_pad", "pos0", "row0"))
    H = W // HEAD_DIM
    assert pos0 % LANES == 0
    tq = _pick(T, (256, 128, 64))
    nq = T // tq
    tpc = max(1, nq // 8) if pos0 == 0 else nq
    k3, v3 = k_new.reshape(B, T, W), v_new.reshape(B, T, W)
    past = pl.BlockSpec((pos0 * H, HEAD_DIM), lambda b, i: (layer * B + b, 0))
    for cls in range(nq // tpc):
        t0 = cls * tpc
        n_keys = min(T, (t0 + tpc) * tq)
        n_new = -(-n_keys // LANES) * LANES
        qrow = lambda w: pl.BlockSpec((tq, w), lambda b, i: (b * nq + t0 + i, 0))
        new = pl.BlockSpec((1, n_keys, W), lambda b, i: (b, 0, 0))
        kern = functools.partial(_fox_kernel, pos0=pos0, tile0=t0, n_new=n_new, s_valid=S,
                                 fcol0=fcol0, n_heads=H)
        ybuf = _pcall(
            kern, name=f"fox_{grp['name']}_{cls}", grid=(B, tpc),
            in_specs=[qrow(W), qrow(LANES), pl.BlockSpec((1, H, pos0 + n_new), lambda b, i: (b, 0, 0))]
            + ([past, past] if pos0 else []) + [new, new],
            inputs=[q, Fq, Fk_t] + ([k_past, v_past] if pos0 else []) + [k3, v3],
            out_specs=pl.BlockSpec((tq, W), lambda b, i: (row0 // tq + b * nq + t0 + i, 2)),
            out_shape=ybuf_shape, carries=[(0, ybuf)])
    return ybuf


def _pack_w_tail(w_in_t, W, H_C):
    sizes = (W, W, W, W, W, W, IDX_HEADS * IDX_DIM, IDX_DIM, IDX_HEADS, W, W, W, H_C, W, W)
    offs = [0]
    for sz in sizes:
        offs.append(offs[-1] + sz)
    seg = lambda k: w_in_t[:, offs[k]:offs[k + 1], :]
    L, _, D = w_in_t.shape
    pad = jnp.zeros((L, SMALL_W - IDX_DIM - IDX_HEADS - H_C, D), w_in_t.dtype)
    return jnp.concatenate([seg(k) for k in (9, 10, 11, 13, 14, 7, 8, 12)] + [pad], axis=1)


def _token_mix(xb, lw, groups, pasts, layer, depth, state_bufs):
    W = lw["W"]
    H = W // HEAD_DIM
    M = xb.shape[0]
    fcol0 = IDX_DIM + IDX_HEADS
    tms = (1536, 768, 384, 128, 64)
    P1 = matmul(xb, lw["w_in_t"], layer, 7 * W, F32, "in_proj_a", tms, 512, w_transposed=True)
    P2 = matmul(xb, lw["w_tail_t"], layer, 5 * W + SMALL_W, F32, "in_proj_b", tms, 512,
                w_transposed=True)
    bf_row = jnp.zeros((1, LANES), F32).at[0, fcol0:fcol0 + H].set(lw["fox_b_f"])
    ybuf_shape = jax.ShapeDtypeStruct((M, 4 * W), BF16)
    ybuf = None
    small_states, new_bufs = [], []
    for gi, (grp, past) in enumerate(zip(groups, pasts)):
        B, T, S, S_pad = grp["B"], grp["T"], grp["S"], grp["S_pad"]
        pk_b, pv_b, pki_b, pk_c, pv_c, plf_c, st_a, st_d, h0 = past
        npast = S - T
        (q_b, k_b, v_x, qi_b, sm_x, kiki, cq_b, ck_b, cv_b, lf), bufs = prep_group(
            P1, P2, bf_row, grp, W, layer, depth, None if state_bufs is None else state_bufs[gi])
        new_bufs.append(bufs)

        ybuf, nst_a = conv_a_group(P1, st_a, lw["conv_a_w"], grp, W, ybuf, ybuf_shape)
        ybuf, nst_d, h_last = lru_group(P2, st_d, h0, lw, grp, W, ybuf, ybuf_shape)

        lf3 = lf.reshape(B, T, LANES)
        if grp["transposed"]:
            ybuf = dsa_cols_group(q_b, qi_b, sm_x, k_b, v_x, kiki, grp, W, ybuf, ybuf_shape)
            lf_all = lf3
        else:
            pki = pki_b[layer]
            kiki_past = jnp.concatenate([pki, pki], axis=-1).reshape(B * npast, LANES)
            ybuf = dsa_rows_group(q_b, qi_b, sm_x, k_b, v_x, kiki, pk_b, pv_b, kiki_past, layer, grp,
                                  W, ybuf, ybuf_shape)
            plf = jnp.pad(plf_c[layer].astype(F32), ((0, 0), (0, 0), (fcol0, LANES - fcol0 - H)))
            lf_all = jnp.concatenate([plf, lf3, jnp.zeros((B, S_pad - S, LANES), F32)], axis=1)

        F_all = cumsum_rows(lf_all.reshape(B * S_pad, LANES), B, S_pad, f"cumsum_{grp['name']}")
        F3 = F_all.reshape(B, S_pad, LANES)
        Fq = F3[:, npast:npast + T, :].reshape(B * T, LANES)
        Fk_t = jnp.swapaxes(F3[:, :, fcol0:fcol0 + H], 1, 2)
        ybuf = fox_group(cq_b, Fq, Fk_t, ck_b, cv_b, pk_c, pv_c, layer, grp, W, fcol0, ybuf, ybuf_shape)

        small_states.append((lf3[:, :, :IDX_DIM], lf3[:, :, fcol0:fcol0 + H],
                             nst_a, nst_d, h_last.reshape(B, W)))
    y = matmul(ybuf, lw["w_out"], layer, 4 * W, BF16, "out_proj", tms, 512)
    return y, small_states, new_bufs


def kernel(x_prompt, x_sample, cache_dsa_k, cache_dsa_v, cache_dsa_kidx, cache_fox_k, cache_fox_v,
           cache_fox_logf, state_conv_a, state_conv_d, state_lru, ln_g, ln_b, ffn_w13, ffn_w2, w_in,
           fox_b_f, conv_a_w, conv_d_w, conv_d_b, lru_wa, lru_ba, lru_wx, lru_bx, lru_lambda, w_out):
    Bp, Tp, D = x_prompt.shape
    Bs, Ts, _ = x_sample.shape
    depth = ln_g.shape[0]
    past_len = cache_dsa_k.shape[2]
    W = D // 4
    H = W // HEAD_DIM
    assert IDX_HEADS * IDX_DIM == W and W % HEAD_DIM == 0
    alpha = (2.0 * depth) ** 0.25
    dt = x_prompt.dtype

    def group(name, row0, B, T, pos0):
        S = pos0 + T
        S_pad = -(-S // LANES) * LANES
        pos = pos0 + jnp.arange(T, dtype=jnp.int32)
        tabs = jnp.stack(_rope_tables(pos, HEAD_DIM, ROPE_DIM // 2)
                         + _rope_tables(pos, IDX_DIM, IDX_ROPE_DIM // 2)
                         + _rope_tables(pos, LANES, IDX_ROPE_DIM // 2))
        return dict(name=name, row0=row0, B=B, T=T, pos0=pos0, S=S, S_pad=S_pad, tabs=tabs,
                    transposed=(pos0 == 0 and T % LANES == 0))

    groups = [group("prompt", 0, Bp, Tp, 0), group("sample", Bp * Tp, Bs, Ts, past_len)]
    Rp, Rs = Bp * Tp, Bs * Ts

    x = (x_prompt.reshape(Rp, D), x_sample.reshape(Rs, D))
    xb = cast_rows(x[0], x[1], BF16, "cast_x")

    w13 = ffn_w13.reshape((2 * depth,) + ffn_w13.shape[2:])
    w2 = ffn_w2.reshape((2 * depth,) + ffn_w2.shape[2:])
    w_in_t = jnp.swapaxes(w_in, 1, 2)
    w_tail_t = _pack_w_tail(w_in_t, W, H)
    rows2d = lambda c: c.reshape(-1, HEAD_DIM)
    cache_rows = (rows2d(cache_dsa_k), rows2d(cache_dsa_v), cache_dsa_kidx, rows2d(cache_fox_k),
                  rows2d(cache_fox_v), cache_fox_logf)
    down_tms = (768, 384, 128, 64)

    small = [[], []]
    state_bufs = None
    for l in range(depth):
        lw = dict(W=W, w_in_t=w_in_t, w_tail_t=w_tail_t, fox_b_f=fox_b_f[l], conv_a_w=conv_a_w[l],
                  conv_d_w=conv_d_w[l], conv_d_b=conv_d_b[l], lru_wa=lru_wa[l], lru_ba=lru_ba[l],
                  lru_wx=lru_wx[l], lru_bx=lru_bx[l], lru_lambda=lru_lambda[l], w_out=w_out)
        empty = (None,) * 6 + (jnp.zeros((Bp, CONV_A_W - 1, W), dt),
                               jnp.zeros((Bp, CONV_D_W - 1, W), dt), jnp.zeros((Bp, W), dt))
        sample_past = cache_rows + (state_conv_a[l], state_conv_d[l], state_lru[l])

        h = swiglu_up(xb, w13, 2 * l, "ffn_up")
        f = matmul(h, w2, 2 * l, D, BF16, "ffn_down", down_tms, 512, single_buffer_w=True)
        x, xb = residual_layer_norm(x, f, ln_g[l, 0], ln_b[l, 0], alpha, 0.5, "res_ln")

        y, small_states, state_bufs = _token_mix(xb, lw, groups, [empty, sample_past], l, depth,
                                                 state_bufs)
        x, xb = residual_layer_norm(x, y, ln_g[l, 1], ln_b[l, 1], alpha, 1.0, "res_ln")

        h = swiglu_up(xb, w13, 2 * l + 1, "ffn_up")
        f = matmul(h, w2, 2 * l + 1, D, BF16, "ffn_down", down_tms, 512, single_buffer_w=True)
        if l + 1 < depth:
            x, xb = residual_layer_norm(x, f, ln_g[l, 2], ln_b[l, 2], alpha, 0.5, "res_ln")
        else:
            outs = [residual_layer_norm(x, f, ln_g[l, 2], ln_b[l, 2], alpha, 0.5, "res_ln_out",
                                        rows=r)[0] for r in ((0, Rp), (Rp, Rs))]
        for g in range(2):
            small[g].append(small_states[g])

    res = [outs[0].reshape(Bp, Tp, D), outs[1].reshape(Bs, Ts, D)]
    for g, (B, T) in enumerate(((Bp, Tp), (Bs, Ts))):
        ks, vs, cks, cvs = (a.reshape(depth, B, T, H, HEAD_DIM) for a in state_bufs[g])
        kidx, logf, st_a, st_d, h_last = (jnp.stack([s[i] for s in small[g]]) for i in range(5))
        res += [ks, vs, kidx, cks, cvs, logf, st_a, st_d, h_last]
    return tuple(res)
```

```python
import functools
import math

import jax
import jax.numpy as jnp
from jax import lax
from jax.experimental import pallas as pl
from jax.experimental.pallas import tpu as pltpu

F32 = jnp.float32
BF16 = jnp.bfloat16

CHUNK = 64
HEAD_DIM = 128
ROPE_DIM = HEAD_DIM // 4
ROPE_THETA = 500000.0
IDX_HEADS = 16
IDX_DIM = 64
IDX_ROPE_DIM = IDX_DIM // 4
TOPK_MAX = 256
CONV_A_W = 3
CONV_D_W = 4
LRU_C = 8.0
LN_EPS = 1e-5
LANES = 128
SMALL_W = 512
VMEM_LIMIT = 58 * 1024 * 1024
INT_MIN = -2 ** 31
NT = (((1,), (1,)), ((), ()))


def _pcall(body, *, name, grid, in_specs, inputs, out_specs, out_shape, carries=(), scratch=()):
    n_in, n_c = len(inputs), len(carries)

    def kern(*refs):
        body(*refs[:n_in], *refs[n_in + n_c:])

    return pl.pallas_call(
        kern, name=name, grid=grid,
        in_specs=list(in_specs) + [pl.BlockSpec(memory_space=pl.ANY)] * n_c,
        out_specs=out_specs, out_shape=out_shape,
        input_output_aliases={n_in + k: oi for k, (oi, _) in enumerate(carries)},
        scratch_shapes=list(scratch),
        compiler_params=pltpu.CompilerParams(dimension_semantics=("arbitrary",) * len(grid),
                                             vmem_limit_bytes=VMEM_LIMIT),
    )(*inputs, *[a for _, a in carries])


def _pick(n, cands):
    for c in cands:
        if n % c == 0:
            return c
    raise ValueError(f"no tile for {n} in {cands}")


def _mm_kernel(x_ref, w_ref, o_ref, wb_ref, *, w_transposed):
    @pl.when(pl.program_id(1) == 0)
    def _():
        w = w_ref[0]
        wb_ref[...] = (w.T if w_transposed else w).astype(BF16)

    o_ref[...] = jnp.dot(x_ref[...], wb_ref[...], preferred_element_type=F32).astype(o_ref.dtype)


def matmul(x, w, li, n_cols, out_dtype, name, tms, tn, single_buffer_w=False, w_transposed=False):
    M, K = x.shape
    tm = _pick(M, tms)
    assert n_cols % tn == 0 and w.shape[2 if w_transposed else 1] == K
    mode = dict(pipeline_mode=pl.Buffered(1)) if single_buffer_w else {}
    if w_transposed:
        w_spec = pl.BlockSpec((1, tn, K), lambda j, i: (li, j, 0), **mode)
    else:
        w_spec = pl.BlockSpec((1, K, tn), lambda j, i: (li, 0, j), **mode)
    return _pcall(
        functools.partial(_mm_kernel, w_transposed=w_transposed), name=name,
        grid=(n_cols // tn, M // tm),
        in_specs=[pl.BlockSpec((tm, K), lambda j, i: (i, 0)), w_spec],
        inputs=[x, w],
        out_specs=pl.BlockSpec((tm, tn), lambda j, i: (i, j)),
        out_shape=jax.ShapeDtypeStruct((M, n_cols), out_dtype),
        scratch=[pltpu.VMEM((K, tn), BF16)])


def _swiglu_kernel(x_ref, wg_ref, wu_ref, o_ref, wgb_ref, wub_ref):
    @pl.when(pl.program_id(1) == 0)
    def _():
        wgb_ref[...] = wg_ref[0].astype(BF16)
        wub_ref[...] = wu_ref[0].astype(BF16)

    x = x_ref[...]
    g = jnp.dot(x, wgb_ref[...], preferred_element_type=F32)
    u = jnp.dot(x, wub_ref[...], preferred_element_type=F32)
    o_ref[...] = (g * jax.nn.sigmoid(g) * u).astype(o_ref.dtype)


def swiglu_up(x, w13, li, name):
    M, K = x.shape
    F = w13.shape[2] // 2
    tm = _pick(M, (1536, 768, 384, 128, 64))
    tn = _pick(F, (256, 128))
    nb = F // tn
    return _pcall(
        _swiglu_kernel, name=name, grid=(nb, M // tm),
        in_specs=[pl.BlockSpec((tm, K), lambda j, i: (i, 0)),
                  pl.BlockSpec((1, K, tn), lambda j, i: (li, 0, j)),
                  pl.BlockSpec((1, K, tn), lambda j, i: (li, 0, j + nb))],
        inputs=[x, w13, w13],
        out_specs=pl.BlockSpec((tm, tn), lambda j, i: (i, j)),
        out_shape=jax.ShapeDtypeStruct((M, F), BF16),
        scratch=[pltpu.VMEM((K, tn), BF16), pltpu.VMEM((K, tn), BF16)])


def _split_rows_specs(tm, D, n_first):
    return [pl.BlockSpec((tm, D), lambda i: (jnp.minimum(i, n_first - 1), 0)),
            pl.BlockSpec((tm, D), lambda i: (jnp.maximum(i - n_first, 0), 0))]


def _cast_split_kernel(a_ref, b_ref, o_ref, *, n_first):
    o_ref[...] = jnp.where(pl.program_id(0) < n_first, a_ref[...], b_ref[...]).astype(o_ref.dtype)


def cast_rows(a, b, dtype, name):
    D = a.shape[1]
    tm = _pick(b.shape[0], (256, 128, 64))
    assert a.shape[0] % tm == 0
    n = (a.shape[0] + b.shape[0]) // tm
    return _pcall(
        functools.partial(_cast_split_kernel, n_first=a.shape[0] // tm), name=name, grid=(n,),
        in_specs=_split_rows_specs(tm, D, a.shape[0] // tm), inputs=[a, b],
        out_specs=pl.BlockSpec((tm, D), lambda i: (i, 0)),
        out_shape=jax.ShapeDtypeStruct((n * tm, D), dtype))


def _res_ln_kernel(*refs, alpha, yscale, n_first):
    if n_first is None:
        x_ref, y_ref, g_ref, b_ref, *o_refs = refs
        x = x_ref[...]
    else:
        xa_ref, xb_ref, y_ref, g_ref, b_ref, *o_refs = refs
        x = jnp.where(pl.program_id(0) < n_first, xa_ref[...], xb_ref[...])
    z = alpha * x + yscale * y_ref[...].astype(F32)
    mu = jnp.mean(z, axis=-1, keepdims=True)
    zc = z - mu
    var = jnp.mean(zc * zc, axis=-1, keepdims=True)
    o = zc * lax.rsqrt(var + LN_EPS) * g_ref[...] + b_ref[...]
    o_refs[0][...] = o
    if len(o_refs) > 1:
        o_refs[1][...] = o.astype(BF16)


def residual_layer_norm(x, y, g, b, alpha, yscale, name, rows=None):
    M, D = y.shape
    row0, n = rows if rows is not None else (0, M)
    tm = _pick(n if not isinstance(x, tuple) else x[1].shape[0], (256, 128, 64))
    assert row0 % tm == 0
    rb0 = row0 // tm
    src = pl.BlockSpec((tm, D), lambda i: (rb0 + i, 0))
    dst = pl.BlockSpec((tm, D), lambda i: (i, 0))
    vec = pl.BlockSpec((1, D), lambda i: (0, 0))
    shapes = [jax.ShapeDtypeStruct((n, D), F32)]
    if rows is None:
        shapes.append(jax.ShapeDtypeStruct((n, D), BF16))
    if isinstance(x, tuple):
        assert rows is None and x[0].shape[0] % tm == 0
        n_first = x[0].shape[0] // tm
        x_specs, xs = _split_rows_specs(tm, D, n_first), list(x)
    else:
        n_first, x_specs, xs = None, [src], [x]
    return _pcall(
        functools.partial(_res_ln_kernel, alpha=alpha, yscale=yscale, n_first=n_first), name=name,
        grid=(n // tm,), in_specs=x_specs + [src, vec, vec],
        inputs=xs + [y, g.reshape(1, D), b.reshape(1, D)],
        out_specs=[dst] * len(shapes), out_shape=shapes)


def _rope_tables(pos, period, half):
    inv = ROPE_THETA ** (-jnp.arange(half, dtype=F32) / half)
    ang = pos.astype(F32)[:, None] * inv[None, :]
    cos, sin = jnp.cos(ang), jnp.sin(ang)
    T = pos.shape[0]
    reps = LANES // period
    zeros = jnp.zeros((T, period - 2 * half), F32)
    c = jnp.concatenate([cos, cos, jnp.ones((T, period - 2 * half), F32)], axis=1)
    s1 = jnp.concatenate([-sin, jnp.zeros((T, half), F32), zeros], axis=1)
    s2 = jnp.concatenate([jnp.zeros((T, half), F32), sin, zeros], axis=1)
    return [jnp.tile(t, (1, reps)) for t in (c, s1, s2)]


def _rope(x, c, s1, s2, half):
    outs = []
    for g in range(x.shape[-1] // LANES):
        xg = x[:, g * LANES:(g + 1) * LANES]
        outs.append(xg * c + pltpu.roll(xg, LANES - half, 1) * s1 + pltpu.roll(xg, half, 1) * s2)
    return outs[0] if len(outs) == 1 else jnp.concatenate(outs, axis=1)


def _store_heads(o_ref, x):
    n_heads = x.shape[1] // HEAD_DIM
    for h in range(n_heads):
        o_ref[pl.ds(h, x.shape[0], stride=n_heads), :] = x[:, h * HEAD_DIM:(h + 1) * HEAD_DIM]


def _prep_kernel(bq_ref, bk_ref, bv_ref, bqi_ref, cq_ref, ck_ref, cv_ref, sm_ref, tab_ref, bf_ref,
                 q_o, kb_o, v_o, qi_o, sm_o, kiki_o, cq_o, ck_o, cv_o, lf_o,
                 ks_o, vs_o, cks_o, cvs_o, *, transposed):
    t = tab_ref[...]
    q_o[...] = _rope(bq_ref[...], t[0], t[1], t[2], ROPE_DIM // 2).astype(BF16)
    k = _rope(bk_ref[...], t[0], t[1], t[2], ROPE_DIM // 2)
    _store_heads(ks_o, k)
    kb_o[...] = k.astype(BF16)
    v = bv_ref[...]
    _store_heads(vs_o, v)
    qi_o[...] = _rope(bqi_ref[...], t[3], t[4], t[5], IDX_ROPE_DIM // 2).astype(BF16)
    sm = _rope(sm_ref[...], t[6], t[7], t[8], IDX_ROPE_DIM // 2)
    if transposed:
        v_o[0] = v.T.astype(BF16)
        sm_o[0] = sm.T
    else:
        v_o[...] = v.astype(BF16)
        sm_o[...] = sm
    lane = lax.broadcasted_iota(jnp.int32, sm.shape, 1)
    ki_lo = jnp.where(lane < IDX_DIM, sm, 0.0)
    kiki_o[...] = (ki_lo + pltpu.roll(ki_lo, IDX_DIM, 1)).astype(BF16)
    cq_o[...] = cq_ref[...].astype(BF16)
    ck = ck_ref[...]
    _store_heads(cks_o, ck)
    ck_o[...] = ck.astype(BF16)
    cv = cv_ref[...]
    _store_heads(cvs_o, cv)
    cv_o[...] = cv.astype(BF16)
    z = -(sm + bf_ref[...])
    lf_o[...] = jnp.where(lane < IDX_DIM, sm, -(jnp.maximum(z, 0.0) + jnp.log1p(jnp.exp(-jnp.abs(z)))))


def prep_group(P1, P2, bf_row, grp, W, layer, depth, state_bufs):
    row0, B, T = grp["row0"], grp["B"], grp["T"]
    tr = grp["transposed"]
    tt = _pick(T, (256, 128, 64))
    nt = T // tt
    rb0 = row0 // tt
    R = B * T

    def col(cb, width):
        return pl.BlockSpec((tt, width), lambda b, i, cb=cb: (rb0 + b * nt + i, cb))

    rows = lambda w: pl.BlockSpec((tt, w), lambda b, i: (b * nt + i, 0))
    n_heads = W // HEAD_DIM
    srows = pl.BlockSpec((tt * n_heads, HEAD_DIM), lambda b, i: (layer * (R // tt) + b * nt + i, 0))
    bf = lambda w: (rows(w), jax.ShapeDtypeStruct((R, w), BF16))
    if tr:
        v_out = (pl.BlockSpec((1, W, tt), lambda b, i: (b, 0, i)), jax.ShapeDtypeStruct((B, W, T), BF16))
        sm_out = (pl.BlockSpec((1, LANES, tt), lambda b, i: (b, 0, i)),
                  jax.ShapeDtypeStruct((B, LANES, T), F32))
    else:
        v_out = bf(W)
        sm_out = (rows(LANES), jax.ShapeDtypeStruct((R, LANES), F32))
    state = (srows, jax.ShapeDtypeStruct((depth * R * n_heads, HEAD_DIM), F32))
    outs = [bf(W), bf(W), v_out, bf(IDX_HEADS * IDX_DIM), sm_out, bf(LANES),
            bf(W), bf(W), bf(W), (rows(LANES), jax.ShapeDtypeStruct((R, LANES), F32)),
            state, state, state, state]
    carries = [] if state_bufs is None else [(10 + k, a) for k, a in enumerate(state_bufs)]
    small_cb = (5 * W) // LANES
    res = _pcall(
        functools.partial(_prep_kernel, transposed=tr), name=f"prep_{grp['name']}", grid=(B, nt),
        in_specs=[col(3, W), col(4, W), col(5, W), col(6, W), col(0, W), col(1, W), col(2, W),
                  col(small_cb, LANES),
                  pl.BlockSpec((9, tt, LANES), lambda b, i: (0, i, 0)),
                  pl.BlockSpec((1, LANES), lambda b, i: (0, 0))],
        inputs=[P1] * 4 + [P2] * 4 + [grp["tabs"], bf_row],
        out_specs=[s for s, _ in outs], out_shape=[s for _, s in outs], carries=carries)
    return res[:10], list(res[10:])


HALO = 8


def _delayed_rows(buf, x, state, delays):
    T, n = x.shape[0], state.shape[0]
    buf[HALO:HALO + T, :] = x
    buf[HALO - n:HALO, :] = state
    return [buf[HALO - d:HALO - d + T, :] for d in delays]


def _conv_a_kernel(ah_ref, ab_ref, ac_ref, st_ref, w_ref, y_ref, nst_ref, buf):
    z = ac_ref[...] * ah_ref[...]
    st = st_ref[0]
    w = w_ref[...]
    T = z.shape[0]
    z1, z2 = _delayed_rows(buf, z, st, (1, 2))
    conv = z2 * w[0:1, :] + z1 * w[1:2, :] + z * w[2:3, :]
    y_ref[...] = (ab_ref[...] * conv).astype(y_ref.dtype)
    nst_ref[0] = z[T - 2:T, :]


def conv_a_group(P, state_a, conv_w, grp, W, ybuf, ybuf_shape):
    row0, B, T = grp["row0"], grp["B"], grp["T"]
    assert T >= CONV_A_W - 1 and row0 % T == 0
    tc = _pick(W, (256, 128))
    nc = W // tc
    rb0 = row0 // T

    def col(k):
        return pl.BlockSpec((T, tc), lambda b, c, k=k: (rb0 + b, k * nc + c))

    return _pcall(
        _conv_a_kernel, name=f"conv_a_{grp['name']}", grid=(B, nc),
        in_specs=[col(0), col(1), col(2),
                  pl.BlockSpec((1, CONV_A_W - 1, tc), lambda b, c: (b, 0, c)),
                  pl.BlockSpec((CONV_A_W, tc), lambda b, c: (0, c))],
        inputs=[P, P, P, state_a, conv_w],
        out_specs=[pl.BlockSpec((T, tc), lambda b, c: (rb0 + b, c)),
                   pl.BlockSpec((1, CONV_A_W - 1, tc), lambda b, c: (b, 0, c))],
        out_shape=[ybuf_shape, jax.ShapeDtypeStruct((B, CONV_A_W - 1, W), F32)],
        carries=[] if ybuf is None else [(0, ybuf)], scratch=[pltpu.VMEM((HALO + T, tc), F32)])


def _lru_kernel(dx_ref, dg_ref, st_ref, h0_ref, cw_ref, cb_ref, wa_ref, ba_ref, wx_ref, bx_ref,
                lam_ref, y_ref, nst_ref, hl_ref, buf):
    x = dx_ref[...]
    T = x.shape[0]
    st = st_ref[0]
    w = cw_ref[...]
    x1, x2, x3 = _delayed_rows(buf, x, st, (1, 2, 3))
    xc = x3 * w[0:1, :] + x2 * w[1:2, :] + x1 * w[2:3, :] + x * w[3:4, :] + cb_ref[...]
    xb = xc.astype(BF16)
    r = jax.nn.sigmoid(jnp.dot(xb, wa_ref[0].astype(BF16), preferred_element_type=F32) + ba_ref[...])
    i = jax.nn.sigmoid(jnp.dot(xb, wx_ref[0].astype(BF16), preferred_element_type=F32) + bx_ref[...])
    nl = -lam_ref[...]
    sp = jnp.maximum(nl, 0.0) + jnp.log1p(jnp.exp(-jnp.abs(nl)))
    log_a = -LRU_C * r * sp
    a = jnp.exp(log_a)
    u = jnp.sqrt(-jnp.tanh(log_a) * (a * a + 1.0)) * i * xc
    row = lax.broadcasted_iota(jnp.int32, x.shape, 0)
    u = jnp.where(row == 0, u + a * h0_ref[0], u)
    d = 1
    while d < T:
        keep = row >= d
        a_prev = jnp.where(keep, pltpu.roll(a, d, 0), 1.0)
        u_prev = jnp.where(keep, pltpu.roll(u, d, 0), 0.0)
        u = a * u_prev + u
        a = a * a_prev
        d *= 2
    y_ref[...] = (jax.nn.gelu(dg_ref[...]) * u).astype(y_ref.dtype)
    nst_ref[0] = x[T - 3:T, :]
    hl_ref[0] = u[T - 1:T, :]


def lru_group(P, state_d, h0, lw, grp, W, ybuf, ybuf_shape):
    row0, B, T = grp["row0"], grp["B"], grp["T"]
    wa, wx = lw["lru_wa"], lw["lru_wx"]
    nblk, bw = wa.shape[0], wa.shape[1]
    assert T >= CONV_D_W - 1 and row0 % T == 0 and bw % LANES == 0 and nblk * bw == W
    rb0 = row0 // T
    cb = (3 * W) // bw
    yb = (3 * W) // bw

    def col(k):
        return pl.BlockSpec((T, bw), lambda b, c, k=k: (rb0 + b, cb + k * nblk + c))

    vec = pl.BlockSpec((1, bw), lambda b, c: (0, c))
    mat = pl.BlockSpec((1, bw, bw), lambda b, c: (c, 0, 0))
    row = lambda a: a.reshape(1, W)
    return _pcall(
        _lru_kernel, name=f"lru_{grp['name']}", grid=(B, nblk),
        in_specs=[col(0), col(1),
                  pl.BlockSpec((1, CONV_D_W - 1, bw), lambda b, c: (b, 0, c)),
                  pl.BlockSpec((1, 1, bw), lambda b, c: (b, 0, c)),
                  pl.BlockSpec((CONV_D_W, bw), lambda b, c: (0, c)),
                  vec, mat, vec, mat, vec, vec],
        inputs=[P, P, state_d, h0.reshape(B, 1, W), lw["conv_d_w"], row(lw["conv_d_b"]), wa,
                row(lw["lru_ba"]), wx, row(lw["lru_bx"]), row(lw["lru_lambda"])],
        out_specs=[pl.BlockSpec((T, bw), lambda b, c: (rb0 + b, yb + c)),
                   pl.BlockSpec((1, CONV_D_W - 1, bw), lambda b, c: (b, 0, c)),
                   pl.BlockSpec((1, 1, bw), lambda b, c: (b, 0, c))],
        out_shape=[ybuf_shape, jax.ShapeDtypeStruct((B, CONV_D_W - 1, W), F32),
                   jax.ShapeDtypeStruct((B, 1, W), F32)],
        carries=[(0, ybuf)], scratch=[pltpu.VMEM((HALO + T, bw), F32)])


def _order_key(x):
    bits = pltpu.bitcast(x + 0.0, jnp.int32)
    return jnp.where(bits < 0, bits ^ jnp.int32(0x7FFFFFFF), bits)


def _count(mask, axis):
    ind = jnp.where(mask, 1.0, 0.0)
    if axis == 1:
        return jnp.sum(ind, axis=1, keepdims=True)
    S, tq = ind.shape
    g = 8 if S % 64 == 0 else 1
    part = ind.reshape(g, S // (8 * g), 8, tq).sum(axis=1)
    return part.sum(axis=0).sum(axis=0, keepdims=True)


def _select_topk(key, adm, tri, n_sel, axis):
    shape = tuple(1 if a == axis else s for a, s in enumerate(key.shape))

    def bit_step(b, t):
        cand = t + lax.shift_left(jnp.int32(1), 31 - b)
        return jnp.where(_count(key >= cand, axis) >= float(n_sel), cand, t)

    thr = lax.fori_loop(0, 32, bit_step, jnp.full(shape, INT_MIN, jnp.int32))
    eq = jnp.where(key == thr, 1.0, 0.0).astype(BF16)
    blk = tri.shape[0]
    total = jnp.zeros(shape, F32)
    ranks = []
    for b in range(key.shape[axis] // blk):
        if axis == 0:
            r = jnp.dot(tri, eq[b * blk:(b + 1) * blk, :], preferred_element_type=F32) + total
            total = r[blk - 1:blk, :]
        else:
            r = jnp.dot(eq[:, b * blk:(b + 1) * blk], tri, preferred_element_type=F32) + total
            total = r[:, blk - 1:blk]
        ranks.append(r)
    rank = jnp.concatenate(ranks, axis=axis)
    need = float(n_sel) - _count(key > thr, axis)
    ninf = -jnp.inf
    tie = jnp.where(key == thr, jnp.where(rank <= need, 0.0, ninf), ninf)
    return jnp.where(adm, jnp.where(key > thr, 0.0, tie), ninf)


IDX_SCALE = (IDX_DIM ** -0.5) * (IDX_HEADS ** -0.5)
EXP2_SCALE = (HEAD_DIM ** -0.5) * math.log2(math.e)


def _tri(n, lower):
    r = lax.broadcasted_iota(jnp.int32, (n, n), 0)
    c = lax.broadcasted_iota(jnp.int32, (n, n), 1)
    return jnp.where(r >= c if lower else r <= c, 1.0, 0.0).astype(BF16)


def _head_pair_rows(x):
    lane = lax.broadcasted_iota(jnp.int32, x.shape, 1)
    zero = jnp.zeros_like(x)
    half = x.shape[1] // 2
    return jnp.concatenate([jnp.where(lane < half, x, zero), jnp.where(lane >= half, x, zero)], axis=0)


def _pad_rows(x, n):
    if x.shape[0] == n:
        return x
    return jnp.concatenate([x, jnp.zeros((n - x.shape[0], x.shape[1]), x.dtype)], axis=0)


def _past_head(ref, h, n_heads):
    return ref[pl.ds(h, ref.shape[0] // n_heads, stride=n_heads), :].astype(BF16)


def _dsa_rows_kernel(q_ref, qi_ref, sm_ref, kp_ref, vp_ref, kikip_ref, kn_ref, vn_ref, kikin_ref,
                     tri_ref, o_ref, *, chunk0, n_new, s_valid, n_sel, n_heads):
    c = pl.program_id(1)
    tq = q_ref.shape[0]
    n_past = kp_ref.shape[0] // n_heads
    S = n_past + n_new
    sm = sm_ref[...]
    kn = _pad_rows(kn_ref[...], n_new)
    vn = _pad_rows(vn_ref[...], n_new)
    kikip = kikip_ref[...].astype(BF16)
    kikin = _pad_rows(kikin_ref[...], n_new)
    score = jnp.zeros((tq, S), F32)
    for j in range(IDX_HEADS // 2):
        qq = _head_pair_rows(qi_ref[:, j * LANES:(j + 1) * LANES])
        s2 = jnp.concatenate([lax.dot_general(qq, kikip, NT, preferred_element_type=F32),
                              lax.dot_general(qq, kikin, NT, preferred_element_type=F32)], axis=1)
        for half in range(2):
            h = 2 * j + half
            w = sm[:, IDX_DIM + h:IDX_DIM + h + 1] * IDX_SCALE
            score = score + jnp.maximum(s2[half * tq:(half + 1) * tq, :], 0.0) * w
    kpos = lax.broadcasted_iota(jnp.int32, (tq, S), 1)
    adm = kpos < jnp.minimum((chunk0 + c + 1) * CHUNK, s_valid)
    key = jnp.where(adm, _order_key(score), jnp.int32(INT_MIN))
    bias = _select_topk(key, adm, tri_ref[...], n_sel, 1)

    for h in range(n_heads):
        sl = slice(h * HEAD_DIM, (h + 1) * HEAD_DIM)
        qh = q_ref[:, sl]
        lg = jnp.concatenate(
            [lax.dot_general(qh, _past_head(kp_ref, h, n_heads), NT, preferred_element_type=F32),
             lax.dot_general(qh, kn[:, sl], NT, preferred_element_type=F32)], axis=1) + bias
        p = jnp.exp2((lg - jnp.max(lg, axis=-1, keepdims=True)) * EXP2_SCALE)
        l = jnp.sum(p, axis=-1, keepdims=True)
        pb = p.astype(BF16)
        o = (jnp.dot(pb[:, :n_past], _past_head(vp_ref, h, n_heads), preferred_element_type=F32)
             + jnp.dot(pb[:, n_past:], vn[:, sl], preferred_element_type=F32))
        o_ref[:, sl] = (o / l).astype(o_ref.dtype)


def _dsa_cols_kernel(q_ref, qi_ref, wt_ref, k_ref, vt_ref, kiki_ref, tri_ref, o_ref, *,
                     tile0, n_sel, n_heads):
    i = pl.program_id(1) + tile0
    tq = q_ref.shape[0]
    S = k_ref.shape[1]
    wt = wt_ref[0]
    kiki = kiki_ref[0]
    score = jnp.zeros((S, tq), F32)
    for j in range(IDX_HEADS // 2):
        qq = _head_pair_rows(qi_ref[:, j * LANES:(j + 1) * LANES])
        s2 = lax.dot_general(kiki, qq, NT, preferred_element_type=F32)
        for half in range(2):
            h = 2 * j + half
            w = wt[IDX_DIM + h:IDX_DIM + h + 1, :] * IDX_SCALE
            score = score + jnp.maximum(s2[:, half * tq:(half + 1) * tq], 0.0) * w
    kpos = lax.broadcasted_iota(jnp.int32, (S, tq), 0)
    qpos = i * tq + lax.broadcasted_iota(jnp.int32, (1, tq), 1)
    adm = kpos < (lax.shift_right_logical(qpos, 6) + 1) * CHUNK
    key = jnp.where(adm, _order_key(score), jnp.int32(INT_MIN))
    bias = _select_topk(key, adm, tri_ref[...], n_sel, 0)

    for hp in range(n_heads // 2):
        sl2 = slice(2 * hp * HEAD_DIM, (2 * hp + 2) * HEAD_DIM)
        lg2 = lax.dot_general(k_ref[0, :, sl2], _head_pair_rows(q_ref[:, sl2]), NT,
                              preferred_element_type=F32)
        for half in range(2):
            sl = slice((2 * hp + half) * HEAD_DIM, (2 * hp + half + 1) * HEAD_DIM)
            lg = lg2[:, half * tq:(half + 1) * tq] + bias
            p = jnp.exp2((lg - jnp.max(lg, axis=0, keepdims=True)) * EXP2_SCALE)
            l = jnp.sum(p, axis=0, keepdims=True)
            ot = jnp.dot(vt_ref[0, sl, :], p.astype(BF16), preferred_element_type=F32)
            o_ref[:, sl] = (ot / l).T.astype(o_ref.dtype)


def _tri_block(n):
    return 256 if n % 256 == 0 else LANES


def dsa_rows_group(q, qi, smf, k_new, v_new, kiki_new, k_past, v_past, kiki_past, layer, grp, W,
                   ybuf, ybuf_shape):
    B, T, S, S_pad, pos0, row0 = (grp[k] for k in ("B", "T", "S", "S_pad", "pos0", "row0"))
    assert pos0 % LANES == 0 and T % CHUNK == 0 and pos0 > 0
    nq = T // CHUNK
    blk = _tri_block(S_pad)
    qrow = lambda w: pl.BlockSpec((CHUNK, w), lambda b, c: (b * nq + c, 0))
    new = lambda w: pl.BlockSpec((T, w), lambda b, c: (b, 0))
    kern = functools.partial(_dsa_rows_kernel, chunk0=pos0 // CHUNK, n_new=S_pad - pos0, s_valid=S,
                             n_sel=min(TOPK_MAX, S // 4), n_heads=W // HEAD_DIM)
    return _pcall(
        kern, name=f"dsa_{grp['name']}", grid=(B, nq),
        in_specs=[qrow(W), qrow(IDX_HEADS * IDX_DIM), qrow(LANES),
                  pl.BlockSpec((pos0 * (W // HEAD_DIM), HEAD_DIM), lambda b, c: (layer * B + b, 0)),
                  pl.BlockSpec((pos0 * (W // HEAD_DIM), HEAD_DIM), lambda b, c: (layer * B + b, 0)),
                  pl.BlockSpec((pos0, LANES), lambda b, c: (b, 0)),
                  new(W), new(W), new(LANES),
                  pl.BlockSpec((blk, blk), lambda b, c: (0, 0))],
        inputs=[q, qi, smf, k_past, v_past, kiki_past, k_new, v_new, kiki_new, _tri(blk, False)],
        out_specs=pl.BlockSpec((CHUNK, W), lambda b, c: (row0 // CHUNK + b * nq + c, 1)),
        out_shape=ybuf_shape, carries=[(0, ybuf)])


def dsa_cols_group(q, qi, smT, k_b, vT, kiki, grp, W, ybuf, ybuf_shape):
    B, T, row0 = grp["B"], grp["T"], grp["row0"]
    tq = LANES
    H = W // HEAD_DIM
    assert grp["pos0"] == 0 and T % tq == 0 and row0 % tq == 0 and H % 2 == 0
    nq = T // tq
    kb = max(tq, T // 16)
    tpc = kb // tq
    blk = _tri_block(kb)
    k3, kiki3 = k_b.reshape(B, T, W), kiki.reshape(B, T, LANES)
    for cls in range(nq // tpc):
        s_eff = (cls + 1) * kb
        t0 = cls * tpc
        qrow = lambda w: pl.BlockSpec((tq, w), lambda b, i: (b * nq + t0 + i, 0))
        keys = lambda w: pl.BlockSpec((1, s_eff, w), lambda b, i: (b, 0, 0))
        kern = functools.partial(_dsa_cols_kernel, tile0=t0, n_sel=min(TOPK_MAX, T // 4), n_heads=H)
        ybuf = _pcall(
            kern, name=f"dsa_{grp['name']}_{cls}", grid=(B, tpc),
            in_specs=[qrow(W), qrow(IDX_HEADS * IDX_DIM),
                      pl.BlockSpec((1, LANES, tq), lambda b, i: (b, 0, t0 + i)),
                      keys(W), pl.BlockSpec((1, W, s_eff), lambda b, i: (b, 0, 0)), keys(LANES),
                      pl.BlockSpec((blk, blk), lambda b, i: (0, 0))],
            inputs=[q, qi, smT, k3, vT, kiki3, _tri(blk, True)],
            out_specs=pl.BlockSpec((tq, W), lambda b, i: (row0 // tq + b * nq + t0 + i, 1)),
            out_shape=ybuf_shape, carries=[(0, ybuf)])
    return ybuf


def _cumsum_kernel(x_ref, o_ref):
    x = x_ref[...]
    n = x.shape[0]
    row = lax.broadcasted_iota(jnp.int32, x.shape, 0)
    d = 1
    while d < n:
        x = x + jnp.where(row >= d, pltpu.roll(x, d, 0), 0.0)
        d *= 2
    o_ref[...] = x


def cumsum_rows(x, B, n, name):
    spec = pl.BlockSpec((n, LANES), lambda b: (b, 0))
    return _pcall(_cumsum_kernel, name=name, grid=(B,), in_specs=[spec], inputs=[x], out_specs=spec,
                  out_shape=jax.ShapeDtypeStruct((B * n, LANES), F32))


def _fox_kernel(*refs, pos0, tile0, n_new, s_valid, fcol0, n_heads):
    if pos0:
        q_ref, fq_ref, fk_ref, kp_ref, vp_ref, kn_ref, vn_ref, o_ref = refs
    else:
        q_ref, fq_ref, fk_ref, kn_ref, vn_ref, o_ref = refs
    i = pl.program_id(1) + tile0
    tq = q_ref.shape[0]
    S = pos0 + n_new
    kn = _pad_rows(kn_ref[0], n_new)
    vn = _pad_rows(vn_ref[0], n_new)
    qpos = pos0 + i * tq + lax.broadcasted_iota(jnp.int32, (tq, S), 0)
    kpos = lax.broadcasted_iota(jnp.int32, (tq, S), 1)
    mask = jnp.where((kpos <= qpos) & (kpos < s_valid), 0.0, -jnp.inf)
    fq = fq_ref[...] * (HEAD_DIM ** 0.5)
    fk = fk_ref[0] * (HEAD_DIM ** 0.5)
    for h in range(n_heads):
        sl = slice(h * HEAD_DIM, (h + 1) * HEAD_DIM)
        qh = q_ref[:, sl]
        lg = lax.dot_general(qh, kn[:, sl], NT, preferred_element_type=F32)
        if pos0:
            lg = jnp.concatenate(
                [lax.dot_general(qh, _past_head(kp_ref, h, n_heads), NT, preferred_element_type=F32),
                 lg], axis=1)
        lg = lg + fq[:, fcol0 + h:fcol0 + h + 1] + (mask - fk[h:h + 1, :])
        p = jnp.exp2((lg - jnp.max(lg, axis=-1, keepdims=True)) * EXP2_SCALE)
        l = jnp.sum(p, axis=-1, keepdims=True)
        pb = p.astype(BF16)
        o = jnp.dot(pb[:, pos0:], vn[:, sl], preferred_element_type=F32)
        if pos0:
            o = o + jnp.dot(pb[:, :pos0], _past_head(vp_ref, h, n_heads), preferred_element_type=F32)
        o_ref[:, sl] = (o / l).astype(o_ref.dtype)


def fox_group(q, Fq, Fk_t, k_new, v_new, k_past, v_past, layer, grp, W, fcol0, ybuf, ybuf_shape):
    B, T, S, S_pad, pos0, row0 = (grp[k] for k in ("B", "T", "S", "S_pad", "pos0", "row0"))
    H = W // HEAD_DIM
    assert pos0 % LANES == 0
    tq = _pick(T, (256, 128, 64))
    nq = T // tq
    tpc = max(1, nq // 8) if pos0 == 0 else nq
    k3, v3 = k_new.reshape(B, T, W), v_new.reshape(B, T, W)
    past = pl.BlockSpec((pos0 * H, HEAD_DIM), lambda b, i: (layer * B + b, 0))
    for cls in range(nq // tpc):
        t0 = cls * tpc
        n_keys = min(T, (t0 + tpc) * tq)
        n_new = -(-n_keys // LANES) * LANES
        qrow = lambda w: pl.BlockSpec((tq, w), lambda b, i: (b * nq + t0 + i, 0))
        new = pl.BlockSpec((1, n_keys, W), lambda b, i: (b, 0, 0))
        kern = functools.partial(_fox_kernel, pos0=pos0, tile0=t0, n_new=n_new, s_valid=S,
                                 fcol0=fcol0, n_heads=H)
        ybuf = _pcall(
            kern, name=f"fox_{grp['name']}_{cls}", grid=(B, tpc),
            in_specs=[qrow(W), qrow(LANES), pl.BlockSpec((1, H, pos0 + n_new), lambda b, i: (b, 0, 0))]
            + ([past, past] if pos0 else []) + [new, new],
            inputs=[q, Fq, Fk_t] + ([k_past, v_past] if pos0 else []) + [k3, v3],
            out_specs=pl.BlockSpec((tq, W), lambda b, i: (row0 // tq + b * nq + t0 + i, 2)),
            out_shape=ybuf_shape, carries=[(0, ybuf)])
    return ybuf


def _pack_w_tail(w_in_t, W, H_C):
    sizes = (W, W, W, W, W, W, IDX_HEADS * IDX_DIM, IDX_DIM, IDX_HEADS, W, W, W, H_C, W, W)
    offs = [0]
    for sz in sizes:
        offs.append(offs[-1] + sz)
    seg = lambda k: w_in_t[:, offs[k]:offs[k + 1], :]
    L, _, D = w_in_t.shape
    pad = jnp.zeros((L, SMALL_W - IDX_DIM - IDX_HEADS - H_C, D), w_in_t.dtype)
    return jnp.concatenate([seg(k) for k in (9, 10, 11, 13, 14, 7, 8, 12)] + [pad], axis=1)


def _token_mix(xb, lw, groups, pasts, layer, depth, state_bufs):
    W = lw["W"]
    H = W // HEAD_DIM
    M = xb.shape[0]
    fcol0 = IDX_DIM + IDX_HEADS
    tms = (1536, 768, 384, 128, 64)
    P1 = matmul(xb, lw["w_in_t"], layer, 7 * W, F32, "in_proj_a", tms, 512, w_transposed=True)
    P2 = matmul(xb, lw["w_tail_t"], layer, 5 * W + SMALL_W, F32, "in_proj_b", tms, 512,
                w_transposed=True)
    bf_row = jnp.zeros((1, LANES), F32).at[0, fcol0:fcol0 + H].set(lw["fox_b_f"])
    ybuf_shape = jax.ShapeDtypeStruct((M, 4 * W), BF16)
    ybuf = None
    small_states, new_bufs = [], []
    for gi, (grp, past) in enumerate(zip(groups, pasts)):
        B, T, S, S_pad = grp["B"], grp["T"], grp["S"], grp["S_pad"]
        pk_b, pv_b, pki_b, pk_c, pv_c, plf_c, st_a, st_d, h0 = past
        npast = S - T
        (q_b, k_b, v_x, qi_b, sm_x, kiki, cq_b, ck_b, cv_b, lf), bufs = prep_group(
            P1, P2, bf_row, grp, W, layer, depth, None if state_bufs is None else state_bufs[gi])
        new_bufs.append(bufs)

        ybuf, nst_a = conv_a_group(P1, st_a, lw["conv_a_w"], grp, W, ybuf, ybuf_shape)
        ybuf, nst_d, h_last = lru_group(P2, st_d, h0, lw, grp, W, ybuf, ybuf_shape)

        lf3 = lf.reshape(B, T, LANES)
        if grp["transposed"]:
            ybuf = dsa_cols_group(q_b, qi_b, sm_x, k_b, v_x, kiki, grp, W, ybuf, ybuf_shape)
            lf_all = lf3
        else:
            pki = pki_b[layer]
            kiki_past = jnp.concatenate([pki, pki], axis=-1).reshape(B * npast, LANES)
            ybuf = dsa_rows_group(q_b, qi_b, sm_x, k_b, v_x, kiki, pk_b, pv_b, kiki_past, layer, grp,
                                  W, ybuf, ybuf_shape)
            plf = jnp.pad(plf_c[layer].astype(F32), ((0, 0), (0, 0), (fcol0, LANES - fcol0 - H)))
            lf_all = jnp.concatenate([plf, lf3, jnp.zeros((B, S_pad - S, LANES), F32)], axis=1)

        F_all = cumsum_rows(lf_all.reshape(B * S_pad, LANES), B, S_pad, f"cumsum_{grp['name']}")
        F3 = F_all.reshape(B, S_pad, LANES)
        Fq = F3[:, npast:npast + T, :].reshape(B * T, LANES)
        Fk_t = jnp.swapaxes(F3[:, :, fcol0:fcol0 + H], 1, 2)
        ybuf = fox_group(cq_b, Fq, Fk_t, ck_b, cv_b, pk_c, pv_c, layer, grp, W, fcol0, ybuf, ybuf_shape)

        small_states.append((lf3[:, :, :IDX_DIM], lf3[:, :, fcol0:fcol0 + H],
                             nst_a, nst_d, h_last.reshape(B, W)))
    y = matmul(ybuf, lw["w_out"], layer, 4 * W, BF16, "out_proj", tms, 512)
    return y, small_states, new_bufs


def kernel(x_prompt, x_sample, cache_dsa_k, cache_dsa_v, cache_dsa_kidx, cache_fox_k, cache_fox_v,
           cache_fox_logf, state_conv_a, state_conv_d, state_lru, ln_g, ln_b, ffn_w13, ffn_w2, w_in,
           fox_b_f, conv_a_w, conv_d_w, conv_d_b, lru_wa, lru_ba, lru_wx, lru_bx, lru_lambda, w_out):
    Bp, Tp, D = x_prompt.shape
    Bs, Ts, _ = x_sample.shape
    depth = ln_g.shape[0]
    past_len = cache_dsa_k.shape[2]
    W = D // 4
    H = W // HEAD_DIM
    assert IDX_HEADS * IDX_DIM == W and W % HEAD_DIM == 0
    alpha = (2.0 * depth) ** 0.25
    dt = x_prompt.dtype

    def group(name, row0, B, T, pos0):
        S = pos0 + T
        S_pad = -(-S // LANES) * LANES
        pos = pos0 + jnp.arange(T, dtype=jnp.int32)
        tabs = jnp.stack(_rope_tables(pos, HEAD_DIM, ROPE_DIM // 2)
                         + _rope_tables(pos, IDX_DIM, IDX_ROPE_DIM // 2)
                         + _rope_tables(pos, LANES, IDX_ROPE_DIM // 2))
        return dict(name=name, row0=row0, B=B, T=T, pos0=pos0, S=S, S_pad=S_pad, tabs=tabs,
                    transposed=(pos0 == 0 and T % LANES == 0))

    groups = [group("prompt", 0, Bp, Tp, 0), group("sample", Bp * Tp, Bs, Ts, past_len)]
    Rp, Rs = Bp * Tp, Bs * Ts

    x = (x_prompt.reshape(Rp, D), x_sample.reshape(Rs, D))
    xb = cast_rows(x[0], x[1], BF16, "cast_x")

    w13 = ffn_w13.reshape((2 * depth,) + ffn_w13.shape[2:])
    w2 = ffn_w2.reshape((2 * depth,) + ffn_w2.shape[2:])
    w_in_t = jnp.swapaxes(w_in, 1, 2)
    w_tail_t = _pack_w_tail(w_in_t, W, H)
    rows2d = lambda c: c.reshape(-1, HEAD_DIM)
    cache_rows = (rows2d(cache_dsa_k), rows2d(cache_dsa_v), cache_dsa_kidx, rows2d(cache_fox_k),
                  rows2d(cache_fox_v), cache_fox_logf)
    down_tms = (768, 384, 128, 64)

    small = [[], []]
    state_bufs = None
    for l in range(depth):
        lw = dict(W=W, w_in_t=w_in_t, w_tail_t=w_tail_t, fox_b_f=fox_b_f[l], conv_a_w=conv_a_w[l],
                  conv_d_w=conv_d_w[l], conv_d_b=conv_d_b[l], lru_wa=lru_wa[l], lru_ba=lru_ba[l],
                  lru_wx=lru_wx[l], lru_bx=lru_bx[l], lru_lambda=lru_lambda[l], w_out=w_out)
        empty = (None,) * 6 + (jnp.zeros((Bp, CONV_A_W - 1, W), dt),
                               jnp.zeros((Bp, CONV_D_W - 1, W), dt), jnp.zeros((Bp, W), dt))
        sample_past = cache_rows + (state_conv_a[l], state_conv_d[l], state_lru[l])

        h = swiglu_up(xb, w13, 2 * l, "ffn_up")
        f = matmul(h, w2, 2 * l, D, BF16, "ffn_down", down_tms, 512, single_buffer_w=True)
        x, xb = residual_layer_norm(x, f, ln_g[l, 0], ln_b[l, 0], alpha, 0.5, "res_ln")

        y, small_states, state_bufs = _token_mix(xb, lw, groups, [empty, sample_past], l, depth,
                                                 state_bufs)
        x, xb = residual_layer_norm(x, y, ln_g[l, 1], ln_b[l, 1], alpha, 1.0, "res_ln")

        h = swiglu_up(xb, w13, 2 * l + 1, "ffn_up")
        f = matmul(h, w2, 2 * l + 1, D, BF16, "ffn_down", down_tms, 512, single_buffer_w=True)
        if l + 1 < depth:
            x, xb = residual_layer_norm(x, f, ln_g[l, 2], ln_b[l, 2], alpha, 0.5, "res_ln")
        else:
            outs = [residual_layer_norm(x, f, ln_g[l, 2], ln_b[l, 2], alpha, 0.5, "res_ln_out",
                                        rows=r)[0] for r in ((0, Rp), (Rp, Rs))]
        for g in range(2):
            small[g].append(small_states[g])

    res = [outs[0].reshape(Bp, Tp, D), outs[1].reshape(Bs, Ts, D)]
    for g, (B, T) in enumerate(((Bp, Tp), (Bs, Ts))):
        ks, vs, cks, cvs = (a.reshape(depth, B, T, H, HEAD_DIM) for a in state_bufs[g])
        kidx, logf, st_a, st_d, h_last = (jnp.stack([s[i] for s in small[g]]) for i in range(5))
        res += [ks, vs, kidx, cks, cvs, logf, st_a, st_d, h_last]
    return tuple(res)
```

```python
import functools
import math

import jax
import jax.numpy as jnp
from jax import lax
from jax.experimental import pallas as pl
from jax.experimental.pallas import tpu as pltpu

F32 = jnp.float32
BF16 = jnp.bfloat16

CHUNK = 64
HEAD_DIM = 128
ROPE_DIM = HEAD_DIM // 4
ROPE_THETA = 500000.0
IDX_HEADS = 16
IDX_DIM = 64
IDX_ROPE_DIM = IDX_DIM // 4
TOPK_MAX = 256
CONV_A_W = 3
CONV_D_W = 4
LRU_C = 8.0
LN_EPS = 1e-5
LANES = 128
SMALL_W = 512
VMEM_LIMIT = 58 * 1024 * 1024
INT_MIN = -2 ** 31
NT = (((1,), (1,)), ((), ()))


def _pcall(body, *, name, grid, in_specs, inputs, out_specs, out_shape, carries=(), scratch=()):
    n_in, n_c = len(inputs), len(carries)

    def kern(*refs):
        body(*refs[:n_in], *refs[n_in + n_c:])

    return pl.pallas_call(
        kern, name=name, grid=grid,
        in_specs=list(in_specs) + [pl.BlockSpec(memory_space=pl.ANY)] * n_c,
        out_specs=out_specs, out_shape=out_shape,
        input_output_aliases={n_in + k: oi for k, (oi, _) in enumerate(carries)},
        scratch_shapes=list(scratch),
        compiler_params=pltpu.CompilerParams(dimension_semantics=("arbitrary",) * len(grid),
                                             vmem_limit_bytes=VMEM_LIMIT),
    )(*inputs, *[a for _, a in carries])


def _pick(n, cands):
    for c in cands:
        if n % c == 0:
            return c
    raise ValueError(f"no tile for {n} in {cands}")


def _mm_kernel(x_ref, w_ref, o_ref, wb_ref, *, w_transposed):
    @pl.when(pl.program_id(1) == 0)
    def _():
        w = w_ref[0]
        wb_ref[...] = (w.T if w_transposed else w).astype(BF16)

    o_ref[...] = jnp.dot(x_ref[...], wb_ref[...], preferred_element_type=F32).astype(o_ref.dtype)


def matmul(x, w, li, n_cols, out_dtype, name, tms, tn, single_buffer_w=False, w_transposed=False):
    M, K = x.shape
    tm = _pick(M, tms)
    assert n_cols % tn == 0 and w.shape[2 if w_transposed else 1] == K
    mode = dict(pipeline_mode=pl.Buffered(1)) if single_buffer_w else {}
    if w_transposed:
        w_spec = pl.BlockSpec((1, tn, K), lambda j, i: (li, j, 0), **mode)
    else:
        w_spec = pl.BlockSpec((1, K, tn), lambda j, i: (li, 0, j), **mode)
    return _pcall(
        functools.partial(_mm_kernel, w_transposed=w_transposed), name=name,
        grid=(n_cols // tn, M // tm),
        in_specs=[pl.BlockSpec((tm, K), lambda j, i: (i, 0)), w_spec],
        inputs=[x, w],
        out_specs=pl.BlockSpec((tm, tn), lambda j, i: (i, j)),
        out_shape=jax.ShapeDtypeStruct((M, n_cols), out_dtype),
        scratch=[pltpu.VMEM((K, tn), BF16)])


def _swiglu_kernel(x_ref, wg_ref, wu_ref, o_ref, wgb_ref, wub_ref):
    @pl.when(pl.program_id(1) == 0)
    def _():
        wgb_ref[...] = wg_ref[0].astype(BF16)
        wub_ref[...] = wu_ref[0].astype(BF16)

    x = x_ref[...]
    g = jnp.dot(x, wgb_ref[...], preferred_element_type=F32)
    u = jnp.dot(x, wub_ref[...], preferred_element_type=F32)
    o_ref[...] = (g * jax.nn.sigmoid(g) * u).astype(o_ref.dtype)


def swiglu_up(x, w13, li, name):
    M, K = x.shape
    F = w13.shape[2] // 2
    tm = _pick(M, (1536, 768, 384, 128, 64))
    tn = _pick(F, (256, 128))
    nb = F // tn
    return _pcall(
        _swiglu_kernel, name=name, grid=(nb, M // tm),
        in_specs=[pl.BlockSpec((tm, K), lambda j, i: (i, 0)),
                  pl.BlockSpec((1, K, tn), lambda j, i: (li, 0, j)),
                  pl.BlockSpec((1, K, tn), lambda j, i: (li, 0, j + nb))],
        inputs=[x, w13, w13],
        out_specs=pl.BlockSpec((tm, tn), lambda j, i: (i, j)),
        out_shape=jax.ShapeDtypeStruct((M, F), BF16),
        scratch=[pltpu.VMEM((K, tn), BF16), pltpu.VMEM((K, tn), BF16)])


def _split_rows_specs(tm, D, n_first):
    return [pl.BlockSpec((tm, D), lambda i: (jnp.minimum(i, n_first - 1), 0)),
            pl.BlockSpec((tm, D), lambda i: (jnp.maximum(i - n_first, 0), 0))]


def _cast_split_kernel(a_ref, b_ref, o_ref, *, n_first):
    o_ref[...] = jnp.where(pl.program_id(0) < n_first, a_ref[...], b_ref[...]).astype(o_ref.dtype)


def cast_rows(a, b, dtype, name):
    D = a.shape[1]
    tm = _pick(b.shape[0], (256, 128, 64))
    assert a.shape[0] % tm == 0
    n = (a.shape[0] + b.shape[0]) // tm
    return _pcall(
        functools.partial(_cast_split_kernel, n_first=a.shape[0] // tm), name=name, grid=(n,),
        in_specs=_split_rows_specs(tm, D, a.shape[0] // tm), inputs=[a, b],
        out_specs=pl.BlockSpec((tm, D), lambda i: (i, 0)),
        out_shape=jax.ShapeDtypeStruct((n * tm, D), dtype))


def _res_ln_kernel(*refs, alpha, yscale, n_first):
    if n_first is None:
        x_ref, y_ref, g_ref, b_ref, *o_refs = refs
        x = x_ref[...]
    else:
        xa_ref, xb_ref, y_ref, g_ref, b_ref, *o_refs = refs
        x = jnp.where(pl.program_id(0) < n_first, xa_ref[...], xb_ref[...])
    z = alpha * x + yscale * y_ref[...].astype(F32)
    mu = jnp.mean(z, axis=-1, keepdims=True)
    zc = z - mu
    var = jnp.mean(zc * zc, axis=-1, keepdims=True)
    o = zc * lax.rsqrt(var + LN_EPS) * g_ref[...] + b_ref[...]
    o_refs[0][...] = o
    if len(o_refs) > 1:
        o_refs[1][...] = o.astype(BF16)


def residual_layer_norm(x, y, g, b, alpha, yscale, name, rows=None):
    M, D = y.shape
    row0, n = rows if rows is not None else (0, M)
    tm = _pick(n if not isinstance(x, tuple) else x[1].shape[0], (256, 128, 64))
    assert row0 % tm == 0
    rb0 = row0 // tm
    src = pl.BlockSpec((tm, D), lambda i: (rb0 + i, 0))
    dst = pl.BlockSpec((tm, D), lambda i: (i, 0))
    vec = pl.BlockSpec((1, D), lambda i: (0, 0))
    shapes = [jax.ShapeDtypeStruct((n, D), F32)]
    if rows is None:
        shapes.append(jax.ShapeDtypeStruct((n, D), BF16))
    if isinstance(x, tuple):
        assert rows is None and x[0].shape[0] % tm == 0
        n_first = x[0].shape[0] // tm
        x_specs, xs = _split_rows_specs(tm, D, n_first), list(x)
    else:
        n_first, x_specs, xs = None, [src], [x]
    return _pcall(
        functools.partial(_res_ln_kernel, alpha=alpha, yscale=yscale, n_first=n_first), name=name,
        grid=(n // tm,), in_specs=x_specs + [src, vec, vec],
        inputs=xs + [y, g.reshape(1, D), b.reshape(1, D)],
        out_specs=[dst] * len(shapes), out_shape=shapes)


def _rope_tables(pos, period, half):
    inv = ROPE_THETA ** (-jnp.arange(half, dtype=F32) / half)
    ang = pos.astype(F32)[:, None] * inv[None, :]
    cos, sin = jnp.cos(ang), jnp.sin(ang)
    T = pos.shape[0]
    reps = LANES // period
    zeros = jnp.zeros((T, period - 2 * half), F32)
    c = jnp.concatenate([cos, cos, jnp.ones((T, period - 2 * half), F32)], axis=1)
    s1 = jnp.concatenate([-sin, jnp.zeros((T, half), F32), zeros], axis=1)
    s2 = jnp.concatenate([jnp.zeros((T, half), F32), sin, zeros], axis=1)
    return [jnp.tile(t, (1, reps)) for t in (c, s1, s2)]


def _rope(x, c, s1, s2, half):
    outs = []
    for g in range(x.shape[-1] // LANES):
        xg = x[:, g * LANES:(g + 1) * LANES]
        outs.append(xg * c + pltpu.roll(xg, LANES - half, 1) * s1 + pltpu.roll(xg, half, 1) * s2)
    return outs[0] if len(outs) == 1 else jnp.concatenate(outs, axis=1)


def _store_heads(o_ref, x):
    n_heads = x.shape[1] // HEAD_DIM
    for h in range(n_heads):
        o_ref[pl.ds(h, x.shape[0], stride=n_heads), :] = x[:, h * HEAD_DIM:(h + 1) * HEAD_DIM]


def _prep_kernel(bq_ref, bk_ref, bv_ref, bqi_ref, cq_ref, ck_ref, cv_ref, sm_ref, tab_ref, bf_ref,
                 q_o, kb_o, v_o, qi_o, sm_o, kiki_o, cq_o, ck_o, cv_o, lf_o,
                 ks_o, vs_o, cks_o, cvs_o, *, transposed):
    t = tab_ref[...]
    q_o[...] = _rope(bq_ref[...], t[0], t[1], t[2], ROPE_DIM // 2).astype(BF16)
    k = _rope(bk_ref[...], t[0], t[1], t[2], ROPE_DIM // 2)
    _store_heads(ks_o, k)
    kb_o[...] = k.astype(BF16)
    v = bv_ref[...]
    _store_heads(vs_o, v)
    qi_o[...] = _rope(bqi_ref[...], t[3], t[4], t[5], IDX_ROPE_DIM // 2).astype(BF16)
    sm = _rope(sm_ref[...], t[6], t[7], t[8], IDX_ROPE_DIM // 2)
    if transposed:
        v_o[0] = v.T.astype(BF16)
        sm_o[0] = sm.T
    else:
        v_o[...] = v.astype(BF16)
        sm_o[...] = sm
    lane = lax.broadcasted_iota(jnp.int32, sm.shape, 1)
    ki_lo = jnp.where(lane < IDX_DIM, sm, 0.0)
    kiki_o[...] = (ki_lo + pltpu.roll(ki_lo, IDX_DIM, 1)).astype(BF16)
    cq_o[...] = cq_ref[...].astype(BF16)
    ck = ck_ref[...]
    _store_heads(cks_o, ck)
    ck_o[...] = ck.astype(BF16)
    cv = cv_ref[...]
    _store_heads(cvs_o, cv)
    cv_o[...] = cv.astype(BF16)
    z = -(sm + bf_ref[...])
    lf_o[...] = jnp.where(lane < IDX_DIM, sm, -(jnp.maximum(z, 0.0) + jnp.log1p(jnp.exp(-jnp.abs(z)))))


def prep_group(P1, P2, bf_row, grp, W, layer, depth, state_bufs):
    row0, B, T = grp["row0"], grp["B"], grp["T"]
    tr = grp["transposed"]
    tt = _pick(T, (256, 128, 64))
    nt = T // tt
    rb0 = row0 // tt
    R = B * T

    def col(cb, width):
        return pl.BlockSpec((tt, width), lambda b, i, cb=cb: (rb0 + b * nt + i, cb))

    rows = lambda w: pl.BlockSpec((tt, w), lambda b, i: (b * nt + i, 0))
    n_heads = W // HEAD_DIM
    srows = pl.BlockSpec((tt * n_heads, HEAD_DIM), lambda b, i: (layer * (R // tt) + b * nt + i, 0))
    bf = lambda w: (rows(w), jax.ShapeDtypeStruct((R, w), BF16))
    if tr:
        v_out = (pl.BlockSpec((1, W, tt), lambda b, i: (b, 0, i)), jax.ShapeDtypeStruct((B, W, T), BF16))
        sm_out = (pl.BlockSpec((1, LANES, tt), lambda b, i: (b, 0, i)),
                  jax.ShapeDtypeStruct((B, LANES, T), F32))
    else:
        v_out = bf(W)
        sm_out = (rows(LANES), jax.ShapeDtypeStruct((R, LANES), F32))
    state = (srows, jax.ShapeDtypeStruct((depth * R * n_heads, HEAD_DIM), F32))
    outs = [bf(W), bf(W), v_out, bf(IDX_HEADS * IDX_DIM), sm_out, bf(LANES),
            bf(W), bf(W), bf(W), (rows(LANES), jax.ShapeDtypeStruct((R, LANES), F32)),
            state, state, state, state]
    carries = [] if state_bufs is None else [(10 + k, a) for k, a in enumerate(state_bufs)]
    small_cb = (5 * W) // LANES
    res = _pcall(
        functools.partial(_prep_kernel, transposed=tr), name=f"prep_{grp['name']}", grid=(B, nt),
        in_specs=[col(3, W), col(4, W), col(5, W), col(6, W), col(0, W), col(1, W), col(2, W),
                  col(small_cb, LANES),
                  pl.BlockSpec((9, tt, LANES), lambda b, i: (0, i, 0)),
                  pl.BlockSpec((1, LANES), lambda b, i: (0, 0))],
        inputs=[P1] * 4 + [P2] * 4 + [grp["tabs"], bf_row],
        out_specs=[s for s, _ in outs], out_shape=[s for _, s in outs], carries=carries)
    return res[:10], list(res[10:])


HALO = 8


def _delayed_rows(buf, x, state, delays):
    T, n = x.shape[0], state.shape[0]
    buf[HALO:HALO + T, :] = x
    buf[HALO - n:HALO, :] = state
    return [buf[HALO - d:HALO - d + T, :] for d in delays]


def _conv_a_kernel(ah_ref, ab_ref, ac_ref, st_ref, w_ref, y_ref, nst_ref, buf):
    z = ac_ref[...] * ah_ref[...]
    st = st_ref[0]
    w = w_ref[...]
    T = z.shape[0]
    z1, z2 = _delayed_rows(buf, z, st, (1, 2))
    conv = z2 * w[0:1, :] + z1 * w[1:2, :] + z * w[2:3, :]
    y_ref[...] = (ab_ref[...] * conv).astype(y_ref.dtype)
    nst_ref[0] = z[T - 2:T, :]


def conv_a_group(P, state_a, conv_w, grp, W, ybuf, ybuf_shape):
    row0, B, T = grp["row0"], grp["B"], grp["T"]
    assert T >= CONV_A_W - 1 and row0 % T == 0
    tc = _pick(W, (256, 128))
    nc = W // tc
    rb0 = row0 // T

    def col(k):
        return pl.BlockSpec((T, tc), lambda b, c, k=k: (rb0 + b, k * nc + c))

    return _pcall(
        _conv_a_kernel, name=f"conv_a_{grp['name']}", grid=(B, nc),
        in_specs=[col(0), col(1), col(2),
                  pl.BlockSpec((1, CONV_A_W - 1, tc), lambda b, c: (b, 0, c)),
                  pl.BlockSpec((CONV_A_W, tc), lambda b, c: (0, c))],
        inputs=[P, P, P, state_a, conv_w],
        out_specs=[pl.BlockSpec((T, tc), lambda b, c: (rb0 + b, c)),
                   pl.BlockSpec((1, CONV_A_W - 1, tc), lambda b, c: (b, 0, c))],
        out_shape=[ybuf_shape, jax.ShapeDtypeStruct((B, CONV_A_W - 1, W), F32)],
        carries=[] if ybuf is None else [(0, ybuf)], scratch=[pltpu.VMEM((HALO + T, tc), F32)])


def _lru_kernel(dx_ref, dg_ref, st_ref, h0_ref, cw_ref, cb_ref, wa_ref, ba_ref, wx_ref, bx_ref,
                lam_ref, y_ref, nst_ref, hl_ref, buf):
    x = dx_ref[...]
    T = x.shape[0]
    st = st_ref[0]
    w = cw_ref[...]
    x1, x2, x3 = _delayed_rows(buf, x, st, (1, 2, 3))
    xc = x3 * w[0:1, :] + x2 * w[1:2, :] + x1 * w[2:3, :] + x * w[3:4, :] + cb_ref[...]
    xb = xc.astype(BF16)
    r = jax.nn.sigmoid(jnp.dot(xb, wa_ref[0].astype(BF16), preferred_element_type=F32) + ba_ref[...])
    i = jax.nn.sigmoid(jnp.dot(xb, wx_ref[0].astype(BF16), preferred_element_type=F32) + bx_ref[...])
    nl = -lam_ref[...]
    sp = jnp.maximum(nl, 0.0) + jnp.log1p(jnp.exp(-jnp.abs(nl)))
    log_a = -LRU_C * r * sp
    a = jnp.exp(log_a)
    u = jnp.sqrt(-jnp.tanh(log_a) * (a * a + 1.0)) * i * xc
    row = lax.broadcasted_iota(jnp.int32, x.shape, 0)
    u = jnp.where(row == 0, u + a * h0_ref[0], u)
    d = 1
    while d < T:
        keep = row >= d
        a_prev = jnp.where(keep, pltpu.roll(a, d, 0), 1.0)
        u_prev = jnp.where(keep, pltpu.roll(u, d, 0), 0.0)
        u = a * u_prev + u
        a = a * a_prev
        d *= 2
    y_ref[...] = (jax.nn.gelu(dg_ref[...]) * u).astype(y_ref.dtype)
    nst_ref[0] = x[T - 3:T, :]
    hl_ref[0] = u[T - 1:T, :]


def lru_group(P, state_d, h0, lw, grp, W, ybuf, ybuf_shape):
    row0, B, T = grp["row0"], grp["B"], grp["T"]
    wa, wx = lw["lru_wa"], lw["lru_wx"]
    nblk, bw = wa.shape[0], wa.shape[1]
    assert T >= CONV_D_W - 1 and row0 % T == 0 and bw % LANES == 0 and nblk * bw == W
    rb0 = row0 // T
    cb = (3 * W) // bw
    yb = (3 * W) // bw

    def col(k):
        return pl.BlockSpec((T, bw), lambda b, c, k=k: (rb0 + b, cb + k * nblk + c))

    vec = pl.BlockSpec((1, bw), lambda b, c: (0, c))
    mat = pl.BlockSpec((1, bw, bw), lambda b, c: (c, 0, 0))
    row = lambda a: a.reshape(1, W)
    return _pcall(
        _lru_kernel, name=f"lru_{grp['name']}", grid=(B, nblk),
        in_specs=[col(0), col(1),
                  pl.BlockSpec((1, CONV_D_W - 1, bw), lambda b, c: (b, 0, c)),
                  pl.BlockSpec((1, 1, bw), lambda b, c: (b, 0, c)),
                  pl.BlockSpec((CONV_D_W, bw), lambda b, c: (0, c)),
                  vec, mat, vec, mat, vec, vec],
        inputs=[P, P, state_d, h0.reshape(B, 1, W), lw["conv_d_w"], row(lw["conv_d_b"]), wa,
                row(lw["lru_ba"]), wx, row(lw["lru_bx"]), row(lw["lru_lambda"])],
        out_specs=[pl.BlockSpec((T, bw), lambda b, c: (rb0 + b, yb + c)),
                   pl.BlockSpec((1, CONV_D_W - 1, bw), lambda b, c: (b, 0, c)),
                   pl.BlockSpec((1, 1, bw), lambda b, c: (b, 0, c))],
        out_shape=[ybuf_shape, jax.ShapeDtypeStruct((B, CONV_D_W - 1, W), F32),
                   jax.ShapeDtypeStruct((B, 1, W), F32)],
        carries=[(0, ybuf)], scratch=[pltpu.VMEM((HALO + T, bw), F32)])


def _order_key(x):
    bits = pltpu.bitcast(x + 0.0, jnp.int32)
    return jnp.where(bits < 0, bits ^ jnp.int32(0x7FFFFFFF), bits)


def _count(mask, axis):
    ind = jnp.where(mask, 1.0, 0.0)
    if axis == 1:
        return jnp.sum(ind, axis=1, keepdims=True)
    S, tq = ind.shape
    g = 8 if S % 64 == 0 else 1
    part = ind.reshape(g, S // (8 * g), 8, tq).sum(axis=1)
    return part.sum(axis=0).sum(axis=0, keepdims=True)


def _select_topk(key, adm, tri, n_sel, axis):
    shape = tuple(1 if a == axis else s for a, s in enumerate(key.shape))

    def bit_step(b, t):
        cand = t + lax.shift_left(jnp.int32(1), 31 - b)
        return jnp.where(_count(key >= cand, axis) >= float(n_sel), cand, t)

    thr = lax.fori_loop(0, 32, bit_step, jnp.full(shape, INT_MIN, jnp.int32))
    eq = jnp.where(key == thr, 1.0, 0.0).astype(BF16)
    blk = tri.shape[0]
    total = jnp.zeros(shape, F32)
    ranks = []
    for b in range(key.shape[axis] // blk):
        if axis == 0:
            r = jnp.dot(tri, eq[b * blk:(b + 1) * blk, :], preferred_element_type=F32) + total
            total = r[blk - 1:blk, :]
        else:
            r = jnp.dot(eq[:, b * blk:(b + 1) * blk], tri, preferred_element_type=F32) + total
            total = r[:, blk - 1:blk]
        ranks.append(r)
    rank = jnp.concatenate(ranks, axis=axis)
    need = float(n_sel) - _count(key > thr, axis)
    ninf = -jnp.inf
    tie = jnp.where(key == thr, jnp.where(rank <= need, 0.0, ninf), ninf)
    return jnp.where(adm, jnp.where(key > thr, 0.0, tie), ninf)


IDX_SCALE = (IDX_DIM ** -0.5) * (IDX_HEADS ** -0.5)
EXP2_SCALE = (HEAD_DIM ** -0.5) * math.log2(math.e)


def _tri(n, lower):
    r = lax.broadcasted_iota(jnp.int32, (n, n), 0)
    c = lax.broadcasted_iota(jnp.int32, (n, n), 1)
    return jnp.where(r >= c if lower else r <= c, 1.0, 0.0).astype(BF16)


def _head_pair_rows(x):
    lane = lax.broadcasted_iota(jnp.int32, x.shape, 1)
    zero = jnp.zeros_like(x)
    half = x.shape[1] // 2
    return jnp.concatenate([jnp.where(lane < half, x, zero), jnp.where(lane >= half, x, zero)], axis=0)


def _pad_rows(x, n):
    if x.shape[0] == n:
        return x
    return jnp.concatenate([x, jnp.zeros((n - x.shape[0], x.shape[1]), x.dtype)], axis=0)


def _past_head(ref, h, n_heads):
    return ref[pl.ds(h, ref.shape[0] // n_heads, stride=n_heads), :].astype(BF16)


def _dsa_rows_kernel(q_ref, qi_ref, sm_ref, kp_ref, vp_ref, kikip_ref, kn_ref, vn_ref, kikin_ref,
                     tri_ref, o_ref, *, chunk0, n_new, s_valid, n_sel, n_heads):
    c = pl.program_id(1)
    tq = q_ref.shape[0]
    n_past = kp_ref.shape[0] // n_heads
    S = n_past + n_new
    sm = sm_ref[...]
    kn = _pad_rows(kn_ref[...], n_new)
    vn = _pad_rows(vn_ref[...], n_new)
    kikip = kikip_ref[...].astype(BF16)
    kikin = _pad_rows(kikin_ref[...], n_new)
    score = jnp.zeros((tq, S), F32)
    for j in range(IDX_HEADS // 2):
        qq = _head_pair_rows(qi_ref[:, j * LANES:(j + 1) * LANES])
        s2 = jnp.concatenate([lax.dot_general(qq, kikip, NT, preferred_element_type=F32),
                              lax.dot_general(qq, kikin, NT, preferred_element_type=F32)], axis=1)
        for half in range(2):
            h = 2 * j + half
            w = sm[:, IDX_DIM + h:IDX_DIM + h + 1] * IDX_SCALE
            score = score + jnp.maximum(s2[half * tq:(half + 1) * tq, :], 0.0) * w
    kpos = lax.broadcasted_iota(jnp.int32, (tq, S), 1)
    adm = kpos < jnp.minimum((chunk0 + c + 1) * CHUNK, s_valid)
    key = jnp.where(adm, _order_key(score), jnp.int32(INT_MIN))
    bias = _select_topk(key, adm, tri_ref[...], n_sel, 1)

    for h in range(n_heads):
        sl = slice(h * HEAD_DIM, (h + 1) * HEAD_DIM)
        qh = q_ref[:, sl]
        lg = jnp.concatenate(
            [lax.dot_general(qh, _past_head(kp_ref, h, n_heads), NT, preferred_element_type=F32),
             lax.dot_general(qh, kn[:, sl], NT, preferred_element_type=F32)], axis=1) + bias
        p = jnp.exp2((lg - jnp.max(lg, axis=-1, keepdims=True)) * EXP2_SCALE)
        l = jnp.sum(p, axis=-1, keepdims=True)
        pb = p.astype(BF16)
        o = (jnp.dot(pb[:, :n_past], _past_head(vp_ref, h, n_heads), preferred_element_type=F32)
             + jnp.dot(pb[:, n_past:], vn[:, sl], preferred_element_type=F32))
        o_ref[:, sl] = (o / l).astype(o_ref.dtype)


def _dsa_cols_kernel(q_ref, qi_ref, wt_ref, k_ref, vt_ref, kiki_ref, tri_ref, o_ref, *,
                     tile0, n_sel, n_heads):
    i = pl.program_id(1) + tile0
    tq = q_ref.shape[0]
    S = k_ref.shape[1]
    wt = wt_ref[0]
    kiki = kiki_ref[0]
    score = jnp.zeros((S, tq), F32)
    for j in range(IDX_HEADS // 2):
        qq = _head_pair_rows(qi_ref[:, j * LANES:(j + 1) * LANES])
        s2 = lax.dot_general(kiki, qq, NT, preferred_element_type=F32)
        for half in range(2):
            h = 2 * j + half
            w = wt[IDX_DIM + h:IDX_DIM + h + 1, :] * IDX_SCALE
            score = score + jnp.maximum(s2[:, half * tq:(half + 1) * tq], 0.0) * w
    kpos = lax.broadcasted_iota(jnp.int32, (S, tq), 0)
    qpos = i * tq + lax.broadcasted_iota(jnp.int32, (1, tq), 1)
    adm = kpos < (lax.shift_right_logical(qpos, 6) + 1) * CHUNK
    key = jnp.where(adm, _order_key(score), jnp.int32(INT_MIN))
    bias = _select_topk(key, adm, tri_ref[...], n_sel, 0)

    for hp in range(n_heads // 2):
        sl2 = slice(2 * hp * HEAD_DIM, (2 * hp + 2) * HEAD_DIM)
        lg2 = lax.dot_general(k_ref[0, :, sl2], _head_pair_rows(q_ref[:, sl2]), NT,
                              preferred_element_type=F32)
        for half in range(2):
            sl = slice((2 * hp + half) * HEAD_DIM, (2 * hp + half + 1) * HEAD_DIM)
            lg = lg2[:, half * tq:(half + 1) * tq] + bias
            p = jnp.exp2((lg - jnp.max(lg, axis=0, keepdims=True)) * EXP2_SCALE)
            l = jnp.sum(p, axis=0, keepdims=True)
            ot = jnp.dot(vt_ref[0, sl, :], p.astype(BF16), preferred_element_type=F32)
            o_ref[:, sl] = (ot / l).T.astype(o_ref.dtype)


def _tri_block(n):
    return 256 if n % 256 == 0 else LANES


def dsa_rows_group(q, qi, smf, k_new, v_new, kiki_new, k_past, v_past, kiki_past, layer, grp, W,
                   ybuf, ybuf_shape):
    B, T, S, S_pad, pos0, row0 = (grp[k] for k in ("B", "T", "S", "S_pad", "pos0", "row0"))
    assert pos0 % LANES == 0 and T % CHUNK == 0 and pos0 > 0
    nq = T // CHUNK
    blk = _tri_block(S_pad)
    qrow = lambda w: pl.BlockSpec((CHUNK, w), lambda b, c: (b * nq + c, 0))
    new = lambda w: pl.BlockSpec((T, w), lambda b, c: (b, 0))
    kern = functools.partial(_dsa_rows_kernel, chunk0=pos0 // CHUNK, n_new=S_pad - pos0, s_valid=S,
                             n_sel=min(TOPK_MAX, S // 4), n_heads=W // HEAD_DIM)
    return _pcall(
        kern, name=f"dsa_{grp['name']}", grid=(B, nq),
        in_specs=[qrow(W), qrow(IDX_HEADS * IDX_DIM), qrow(LANES),
                  pl.BlockSpec((pos0 * (W // HEAD_DIM), HEAD_DIM), lambda b, c: (layer * B + b, 0)),
                  pl.BlockSpec((pos0 * (W // HEAD_DIM), HEAD_DIM), lambda b, c: (layer * B + b, 0)),
                  pl.BlockSpec((pos0, LANES), lambda b, c: (b, 0)),
                  new(W), new(W), new(LANES),
                  pl.BlockSpec((blk, blk), lambda b, c: (0, 0))],
        inputs=[q, qi, smf, k_past, v_past, kiki_past, k_new, v_new, kiki_new, _tri(blk, False)],
        out_specs=pl.BlockSpec((CHUNK, W), lambda b, c: (row0 // CHUNK + b * nq + c, 1)),
        out_shape=ybuf_shape, carries=[(0, ybuf)])


def dsa_cols_group(q, qi, smT, k_b, vT, kiki, grp, W, ybuf, ybuf_shape):
    B, T, row0 = grp["B"], grp["T"], grp["row0"]
    tq = 2 * LANES if T % (2 * LANES) == 0 else LANES
    H = W // HEAD_DIM
    assert grp["pos0"] == 0 and T % tq == 0 and row0 % tq == 0 and H % 2 == 0
    nq = T // tq
    kb = max(tq, T // 8)
    tpc = kb // tq
    blk = _tri_block(kb)
    k3, kiki3 = k_b.reshape(B, T, W), kiki.reshape(B, T, LANES)
    for cls in range(nq // tpc):
        s_eff = (cls + 1) * kb
        t0 = cls * tpc
        qrow = lambda w: pl.BlockSpec((tq, w), lambda b, i: (b * nq + t0 + i, 0))
        keys = lambda w: pl.BlockSpec((1, s_eff, w), lambda b, i: (b, 0, 0))
        kern = functools.partial(_dsa_cols_kernel, tile0=t0, n_sel=min(TOPK_MAX, T // 4), n_heads=H)
        ybuf = _pcall(
            kern, name=f"dsa_{grp['name']}_{cls}", grid=(B, tpc),
            in_specs=[qrow(W), qrow(IDX_HEADS * IDX_DIM),
                      pl.BlockSpec((1, LANES, tq), lambda b, i: (b, 0, t0 + i)),
                      keys(W), pl.BlockSpec((1, W, s_eff), lambda b, i: (b, 0, 0)), keys(LANES),
                      pl.BlockSpec((blk, blk), lambda b, i: (0, 0))],
            inputs=[q, qi, smT, k3, vT, kiki3, _tri(blk, True)],
            out_specs=pl.BlockSpec((tq, W), lambda b, i: (row0 // tq + b * nq + t0 + i, 1)),
            out_shape=ybuf_shape, carries=[(0, ybuf)])
    return ybuf


def _cumsum_kernel(x_ref, o_ref):
    x = x_ref[...]
    n = x.shape[0]
    row = lax.broadcasted_iota(jnp.int32, x.shape, 0)
    d = 1
    while d < n:
        x = x + jnp.where(row >= d, pltpu.roll(x, d, 0), 0.0)
        d *= 2
    o_ref[...] = x


def cumsum_rows(x, B, n, name):
    spec = pl.BlockSpec((n, LANES), lambda b: (b, 0))
    return _pcall(_cumsum_kernel, name=name, grid=(B,), in_specs=[spec], inputs=[x], out_specs=spec,
                  out_shape=jax.ShapeDtypeStruct((B * n, LANES), F32))


def _fox_kernel(*refs, pos0, tile0, n_new, s_valid, fcol0, n_heads):
    if pos0:
        q_ref, fq_ref, fk_ref, kp_ref, vp_ref, kn_ref, vn_ref, o_ref = refs
    else:
        q_ref, fq_ref, fk_ref, kn_ref, vn_ref, o_ref = refs
    i = pl.program_id(1) + tile0
    tq = q_ref.shape[0]
    S = pos0 + n_new
    kn = _pad_rows(kn_ref[0], n_new)
    vn = _pad_rows(vn_ref[0], n_new)
    qpos = pos0 + i * tq + lax.broadcasted_iota(jnp.int32, (tq, S), 0)
    kpos = lax.broadcasted_iota(jnp.int32, (tq, S), 1)
    mask = jnp.where((kpos <= qpos) & (kpos < s_valid), 0.0, -jnp.inf)
    fq = fq_ref[...] * (HEAD_DIM ** 0.5)
    fk = fk_ref[0] * (HEAD_DIM ** 0.5)
    for h in range(n_heads):
        sl = slice(h * HEAD_DIM, (h + 1) * HEAD_DIM)
        qh = q_ref[:, sl]
        lg = lax.dot_general(qh, kn[:, sl], NT, preferred_element_type=F32)
        if pos0:
            lg = jnp.concatenate(
                [lax.dot_general(qh, _past_head(kp_ref, h, n_heads), NT, preferred_element_type=F32),
                 lg], axis=1)
        lg = lg + fq[:, fcol0 + h:fcol0 + h + 1] + (mask - fk[h:h + 1, :])
        p = jnp.exp2((lg - jnp.max(lg, axis=-1, keepdims=True)) * EXP2_SCALE)
        l = jnp.sum(p, axis=-1, keepdims=True)
        pb = p.astype(BF16)
        o = jnp.dot(pb[:, pos0:], vn[:, sl], preferred_element_type=F32)
        if pos0:
            o = o + jnp.dot(pb[:, :pos0], _past_head(vp_ref, h, n_heads), preferred_element_type=F32)
        o_ref[:, sl] = (o / l).astype(o_ref.dtype)


def fox_group(q, Fq, Fk_t, k_new, v_new, k_past, v_past, layer, grp, W, fcol0, ybuf, ybuf_shape):
    B, T, S, S_pad, pos0, row0 = (grp[k] for k in ("B", "T", "S", "S_pad", "pos0", "row0"))
    H = W // HEAD_DIM
    assert pos0 % LANES == 0
    tq = _pick(T, (256, 128, 64))
    nq = T // tq
    tpc = max(1, nq // 8) if pos0 == 0 else nq
    k3, v3 = k_new.reshape(B, T, W), v_new.reshape(B, T, W)
    past = pl.BlockSpec((pos0 * H, HEAD_DIM), lambda b, i: (layer * B + b, 0))
    for cls in range(nq // tpc):
        t0 = cls * tpc
        n_keys = min(T, (t0 + tpc) * tq)
        n_new = -(-n_keys // LANES) * LANES
        qrow = lambda w: pl.BlockSpec((tq, w), lambda b, i: (b * nq + t0 + i, 0))
        new = pl.BlockSpec((1, n_keys, W), lambda b, i: (b, 0, 0))
        kern = functools.partial(_fox_kernel, pos0=pos0, tile0=t0, n_new=n_new, s_valid=S,
                                 fcol0=fcol0, n_heads=H)
        ybuf = _pcall(
            kern, name=f"fox_{grp['name']}_{cls}", grid=(B, tpc),
            in_specs=[qrow(W), qrow(LANES), pl.BlockSpec((1, H, pos0 + n_new), lambda b, i: (b, 0, 0))]
            + ([past, past] if pos0 else []) + [new, new],
            inputs=[q, Fq, Fk_t] + ([k_past, v_past] if pos0 else []) + [k3, v3],
            out_specs=pl.BlockSpec((tq, W), lambda b, i: (row0 // tq + b * nq + t0 + i, 2)),
            out_shape=ybuf_shape, carries=[(0, ybuf)])
    return ybuf


def _pack_w_tail(w_in_t, W, H_C):
    sizes = (W, W, W, W, W, W, IDX_HEADS * IDX_DIM, IDX_DIM, IDX_HEADS, W, W, W, H_C, W, W)
    offs = [0]
    for sz in sizes:
        offs.append(offs[-1] + sz)
    seg = lambda k: w_in_t[:, offs[k]:offs[k + 1], :]
    L, _, D = w_in_t.shape
    pad = jnp.zeros((L, SMALL_W - IDX_DIM - IDX_HEADS - H_C, D), w_in_t.dtype)
    return jnp.concatenate([seg(k) for k in (9, 10, 11, 13, 14, 7, 8, 12)] + [pad], axis=1)


def _token_mix(xb, lw, groups, pasts, layer, depth, state_bufs):
    W = lw["W"]
    H = W // HEAD_DIM
    M = xb.shape[0]
    fcol0 = IDX_DIM + IDX_HEADS
    tms = (1536, 768, 384, 128, 64)
    P1 = matmul(xb, lw["w_in_t"], layer, 7 * W, F32, "in_proj_a", tms, 512, w_transposed=True)
    P2 = matmul(xb, lw["w_tail_t"], layer, 5 * W + SMALL_W, F32, "in_proj_b", tms, 512,
                w_transposed=True)
    bf_row = jnp.zeros((1, LANES), F32).at[0, fcol0:fcol0 + H].set(lw["fox_b_f"])
    ybuf_shape = jax.ShapeDtypeStruct((M, 4 * W), BF16)
    ybuf = None
    small_states, new_bufs = [], []
    for gi, (grp, past) in enumerate(zip(groups, pasts)):
        B, T, S, S_pad = grp["B"], grp["T"], grp["S"], grp["S_pad"]
        pk_b, pv_b, pki_b, pk_c, pv_c, plf_c, st_a, st_d, h0 = past
        npast = S - T
        (q_b, k_b, v_x, qi_b, sm_x, kiki, cq_b, ck_b, cv_b, lf), bufs = prep_group(
            P1, P2, bf_row, grp, W, layer, depth, None if state_bufs is None else state_bufs[gi])
        new_bufs.append(bufs)

        ybuf, nst_a = conv_a_group(P1, st_a, lw["conv_a_w"], grp, W, ybuf, ybuf_shape)
        ybuf, nst_d, h_last = lru_group(P2, st_d, h0, lw, grp, W, ybuf, ybuf_shape)

        lf3 = lf.reshape(B, T, LANES)
        if grp["transposed"]:
            ybuf = dsa_cols_group(q_b, qi_b, sm_x, k_b, v_x, kiki, grp, W, ybuf, ybuf_shape)
            lf_all = lf3
        else:
            pki = pki_b[layer]
            kiki_past = jnp.concatenate([pki, pki], axis=-1).reshape(B * npast, LANES)
            ybuf = dsa_rows_group(q_b, qi_b, sm_x, k_b, v_x, kiki, pk_b, pv_b, kiki_past, layer, grp,
                                  W, ybuf, ybuf_shape)
            plf = jnp.pad(plf_c[layer].astype(F32), ((0, 0), (0, 0), (fcol0, LANES - fcol0 - H)))
            lf_all = jnp.concatenate([plf, lf3, jnp.zeros((B, S_pad - S, LANES), F32)], axis=1)

        F_all = cumsum_rows(lf_all.reshape(B * S_pad, LANES), B, S_pad, f"cumsum_{grp['name']}")
        F3 = F_all.reshape(B, S_pad, LANES)
        Fq = F3[:, npast:npast + T, :].reshape(B * T, LANES)
        Fk_t = jnp.swapaxes(F3[:, :, fcol0:fcol0 + H], 1, 2)
        ybuf = fox_group(cq_b, Fq, Fk_t, ck_b, cv_b, pk_c, pv_c, layer, grp, W, fcol0, ybuf, ybuf_shape)

        small_states.append((lf3[:, :, :IDX_DIM], lf3[:, :, fcol0:fcol0 + H],
                             nst_a, nst_d, h_last.reshape(B, W)))
    y = matmul(ybuf, lw["w_out"], layer, 4 * W, BF16, "out_proj", tms, 512)
    return y, small_states, new_bufs


def kernel(x_prompt, x_sample, cache_dsa_k, cache_dsa_v, cache_dsa_kidx, cache_fox_k, cache_fox_v,
           cache_fox_logf, state_conv_a, state_conv_d, state_lru, ln_g, ln_b, ffn_w13, ffn_w2, w_in,
           fox_b_f, conv_a_w, conv_d_w, conv_d_b, lru_wa, lru_ba, lru_wx, lru_bx, lru_lambda, w_out):
    Bp, Tp, D = x_prompt.shape
    Bs, Ts, _ = x_sample.shape
    depth = ln_g.shape[0]
    past_len = cache_dsa_k.shape[2]
    W = D // 4
    H = W // HEAD_DIM
    assert IDX_HEADS * IDX_DIM == W and W % HEAD_DIM == 0
    alpha = (2.0 * depth) ** 0.25
    dt = x_prompt.dtype

    def group(name, row0, B, T, pos0):
        S = pos0 + T
        S_pad = -(-S // LANES) * LANES
        pos = pos0 + jnp.arange(T, dtype=jnp.int32)
        tabs = jnp.stack(_rope_tables(pos, HEAD_DIM, ROPE_DIM // 2)
                         + _rope_tables(pos, IDX_DIM, IDX_ROPE_DIM // 2)
                         + _rope_tables(pos, LANES, IDX_ROPE_DIM // 2))
        return dict(name=name, row0=row0, B=B, T=T, pos0=pos0, S=S, S_pad=S_pad, tabs=tabs,
                    transposed=(pos0 == 0 and T % LANES == 0))

    groups = [group("prompt", 0, Bp, Tp, 0), group("sample", Bp * Tp, Bs, Ts, past_len)]
    Rp, Rs = Bp * Tp, Bs * Ts

    x = (x_prompt.reshape(Rp, D), x_sample.reshape(Rs, D))
    xb = cast_rows(x[0], x[1], BF16, "cast_x")

    w13 = ffn_w13.reshape((2 * depth,) + ffn_w13.shape[2:])
    w2 = ffn_w2.reshape((2 * depth,) + ffn_w2.shape[2:])
    w_in_t = jnp.swapaxes(w_in, 1, 2)
    w_tail_t = _pack_w_tail(w_in_t, W, H)
    rows2d = lambda c: c.reshape(-1, HEAD_DIM)
    cache_rows = (rows2d(cache_dsa_k), rows2d(cache_dsa_v), cache_dsa_kidx, rows2d(cache_fox_k),
                  rows2d(cache_fox_v), cache_fox_logf)
    down_tms = (768, 384, 128, 64)

    small = [[], []]
    state_bufs = None
    for l in range(depth):
        lw = dict(W=W, w_in_t=w_in_t, w_tail_t=w_tail_t, fox_b_f=fox_b_f[l], conv_a_w=conv_a_w[l],
                  conv_d_w=conv_d_w[l], conv_d_b=conv_d_b[l], lru_wa=lru_wa[l], lru_ba=lru_ba[l],
                  lru_wx=lru_wx[l], lru_bx=lru_bx[l], lru_lambda=lru_lambda[l], w_out=w_out)
        empty = (None,) * 6 + (jnp.zeros((Bp, CONV_A_W - 1, W), dt),
                               jnp.zeros((Bp, CONV_D_W - 1, W), dt), jnp.zeros((Bp, W), dt))
        sample_past = cache_rows + (state_conv_a[l], state_conv_d[l], state_lru[l])

        h = swiglu_up(xb, w13, 2 * l, "ffn_up")
        f = matmul(h, w2, 2 * l, D, BF16, "ffn_down", down_tms, 512, single_buffer_w=True)
        x, xb = residual_layer_norm(x, f, ln_g[l, 0], ln_b[l, 0], alpha, 0.5, "res_ln")

        y, small_states, state_bufs = _token_mix(xb, lw, groups, [empty, sample_past], l, depth,
                                                 state_bufs)
        x, xb = residual_layer_norm(x, y, ln_g[l, 1], ln_b[l, 1], alpha, 1.0, "res_ln")

        h = swiglu_up(xb, w13, 2 * l + 1, "ffn_up")
        f = matmul(h, w2, 2 * l + 1, D, BF16, "ffn_down", down_tms, 512, single_buffer_w=True)
        if l + 1 < depth:
            x, xb = residual_layer_norm(x, f, ln_g[l, 2], ln_b[l, 2], alpha, 0.5, "res_ln")
        else:
            outs = [residual_layer_norm(x, f, ln_g[l, 2], ln_b[l, 2], alpha, 0.5, "res_ln_out",
                                        rows=r)[0] for r in ((0, Rp), (Rp, Rs))]
        for g in range(2):
            small[g].append(small_states[g])

    res = [outs[0].reshape(Bp, Tp, D), outs[1].reshape(Bs, Ts, D)]
    for g, (B, T) in enumerate(((Bp, Tp), (Bs, Ts))):
        ks, vs, cks, cvs = (a.reshape(depth, B, T, H, HEAD_DIM) for a in state_bufs[g])
        kidx, logf, st_a, st_d, h_last = (jnp.stack([s[i] for s in small[g]]) for i in range(5))
        res += [ks, vs, kidx, cks, cvs, logf, st_a, st_d, h_last]
    return tuple(res)
```
